```python
import math
import jax, jax.numpy as jnp
from jax import lax
import numpy as np

D_MODEL = 4096
BATCH = 4
SEQ = 4096
DEPTH = 1

HEAD_DIM = 128
MIX_WIDTH = D_MODEL
MOBA_WIDTH = MIX_WIDTH // 2
MOBA_HEADS = MOBA_WIDTH // HEAD_DIM
DIFF_WIDTH = MIX_WIDTH - MOBA_WIDTH
DIFF_HEADS = DIFF_WIDTH // (2 * HEAD_DIM)
IN_WIDTH = 3 * MOBA_WIDTH + 3 * DIFF_WIDTH
MOBA_BLOCK = 256
MOBA_TOPK = 3
MOBA_QCHUNK = 32
DIFF_QBLOCK = 128
ROT_DIM = HEAD_DIM // 4
ROPE_THETA = 500000.0
MEM_LEN = 256
MEM_HEADS = 4
MEM_HEAD_DIM = 128
MEM_WIDTH = MEM_HEADS * MEM_HEAD_DIM
D_FF = 4 * D_MODEL
EPS = 1e-5

kernel_name = "hybrid_moba_diffattn_memxattn_sqrelu"


def rms_norm(x, g):
    xf = x.astype(jnp.float32)
    y = xf * lax.rsqrt(jnp.mean(xf * xf, axis=-1, keepdims=True) + EPS)
    return (y * g.astype(jnp.float32)).astype(x.dtype)


def rotary_tables(seq_len):
    pos = jnp.arange(seq_len, dtype=jnp.float32)
    inv_freq = ROPE_THETA ** (-jnp.arange(0, ROT_DIM, 2, dtype=jnp.float32) / ROT_DIM)
    ang = pos[:, None] * inv_freq[None, :]
    return jnp.cos(ang), jnp.sin(ang)


def apply_partial_rope(x, cos, sin):
    half = ROT_DIM // 2
    xr = x[..., :ROT_DIM].astype(jnp.float32)
    x1, x2 = xr[..., :half], xr[..., half:]
    rot = jnp.concatenate([x1 * cos - x2 * sin, x2 * cos + x1 * sin], axis=-1).astype(x.dtype)
    return jnp.concatenate([rot, x[..., ROT_DIM:]], axis=-1)


def to_heads(t, n_heads, dh):
    b, s, _ = t.shape
    return t.reshape(b, s, n_heads, dh).transpose(0, 2, 1, 3)


def from_heads(t):
    b, h, s, dh = t.shape
    return t.transpose(0, 2, 1, 3).reshape(b, s, h * dh)


def moba_attention(q, k, v):
    B, H, S, dh = q.shape
    nb = -(-S // MOBA_BLOCK)
    pad = nb * MOBA_BLOCK - S
    kp = jnp.pad(k, ((0, 0), (0, 0), (0, pad), (0, 0)))
    vp = jnp.pad(v, ((0, 0), (0, 0), (0, pad), (0, 0)))
    kb = kp.reshape(B, H, nb, MOBA_BLOCK, dh)
    vb = vp.reshape(B, H, nb, MOBA_BLOCK, dh)
    kmean = jnp.mean(kb.astype(jnp.float32), axis=3).astype(k.dtype)
    topk = min(MOBA_TOPK, nb)
    scale = dh ** -0.5
    bidx = jnp.arange(B)[:, None, None, None]
    hidx = jnp.arange(H)[None, :, None, None]

    def chunk(c):
        start = c * MOBA_QCHUNK
        cur = start // MOBA_BLOCK
        qc = lax.dynamic_slice_in_dim(q, start, MOBA_QCHUNK, axis=2)
        gate = jnp.einsum('bhqd,bhnd->bhqn', qc, kmean).astype(jnp.float32)
        past = jnp.arange(nb) < cur
        gate = jnp.where(past[None, None, None, :], gate, -jnp.inf)
        _, sel = lax.top_k(gate, topk)
        sel_valid = sel < cur
        k_sel = kb[bidx, hidx, sel]
        v_sel = vb[bidx, hidx, sel]
        s_sel = jnp.einsum('bhqd,bhqnkd->bhqnk', qc, k_sel).astype(jnp.float32) * scale
        s_sel = jnp.where(sel_valid[..., None], s_sel, -jnp.inf)
        k_own = lax.dynamic_slice_in_dim(kp, cur * MOBA_BLOCK, MOBA_BLOCK, axis=2)
        v_own = lax.dynamic_slice_in_dim(vp, cur * MOBA_BLOCK, MOBA_BLOCK, axis=2)
        s_own = jnp.einsum('bhqd,bhkd->bhqk', qc, k_own).astype(jnp.float32) * scale
        qpos = start + jnp.arange(MOBA_QCHUNK)
        kpos = cur * MOBA_BLOCK + jnp.arange(MOBA_BLOCK)
        s_own = jnp.where(kpos[None, :] <= qpos[:, None], s_own, -jnp.inf)
        s = jnp.concatenate([s_sel.reshape(B, H, MOBA_QCHUNK, topk * MOBA_BLOCK), s_own], axis=-1)
        p = jax.nn.softmax(s, axis=-1).astype(v.dtype)
        p_sel = p[..., :topk * MOBA_BLOCK].reshape(B, H, MOBA_QCHUNK, topk, MOBA_BLOCK)
        p_own = p[..., topk * MOBA_BLOCK:]
        return (jnp.einsum('bhqnk,bhqnkd->bhqd', p_sel, v_sel)
                + jnp.einsum('bhqk,bhkd->bhqd', p_own, v_own))

    out = lax.map(chunk, jnp.arange(S // MOBA_QCHUNK))
    return out.transpose(1, 2, 0, 3, 4).reshape(B, H, S, dh)


def diff_attention(q, k, v, lam, lambda_init, subln_g):
    B, H2, S, dh = q.shape
    H = H2 // 2
    scale = dh ** -0.5
    kpos = jnp.arange(S)

    def block(c):
        start = c * DIFF_QBLOCK
        qc = lax.dynamic_slice_in_dim(q, start, DIFF_QBLOCK, axis=2)
        s = jnp.einsum('bhqd,bhkd->bhqk', qc, k).astype(jnp.float32) * scale
        qpos = start + jnp.arange(DIFF_QBLOCK)
        s = jnp.where(kpos[None, :] <= qpos[:, None], s, -jnp.inf)
        p = jax.nn.softmax(s, axis=-1).reshape(B, H, 2, DIFF_QBLOCK, S)
        a = p[:, :, 0] - lam * p[:, :, 1]
        return jnp.einsum('bhqk,bhkd->bhqd', a.astype(v.dtype), v)

    out = lax.map(block, jnp.arange(S // DIFF_QBLOCK))
    out = out.transpose(1, 2, 0, 3, 4).reshape(B, H, S, 2 * dh)
    return (rms_norm(out, subln_g).astype(jnp.float32) * (1.0 - lambda_init)).astype(v.dtype)


def memory_cross_attention(h, m, w_cq, w_ckv, w_co):
    B, S, _ = h.shape
    q = to_heads(h @ w_cq, MEM_HEADS, MEM_HEAD_DIM)
    kv = (m @ w_ckv).reshape(B, MEM_LEN, 2, MEM_HEADS, MEM_HEAD_DIM)
    k = kv[:, :, 0].transpose(0, 2, 1, 3)
    v = kv[:, :, 1].transpose(0, 2, 1, 3)
    s = jnp.einsum('bhqd,bhmd->bhqm', q, k).astype(jnp.float32) * (MEM_HEAD_DIM ** -0.5)
    p = jax.nn.softmax(s, axis=-1).astype(v.dtype)
    o = jnp.einsum('bhqm,bhmd->bhqd', p, v)
    return from_heads(o) @ w_co


def setup_inputs(seed: int = 0) -> dict:
    key = jax.random.key(seed)
    ks = jax.random.split(key, 24)
    f32 = jnp.float32

    def nrm(k, shape, scale):
        return jax.random.normal(k, shape, f32) * scale

    def gain(k, shape):
        return 1.0 + 0.02 * jax.random.normal(k, shape, f32)

    return {
        "x": nrm(ks[0], (BATCH, SEQ, D_MODEL), 1.0),
        "mem": nrm(ks[1], (BATCH, MEM_LEN, D_MODEL), 1.0),
        "ln_mix_g": gain(ks[2], (DEPTH, D_MODEL)),
        "w_in": nrm(ks[3], (DEPTH, D_MODEL, IN_WIDTH), D_MODEL ** -0.5),
        "moba_out_g": gain(ks[4], (DEPTH, MOBA_WIDTH)),
        "lambda_q1": nrm(ks[5], (DEPTH, HEAD_DIM), 0.1),
        "lambda_k1": nrm(ks[6], (DEPTH, HEAD_DIM), 0.1),
        "lambda_q2": nrm(ks[7], (DEPTH, HEAD_DIM), 0.1),
        "lambda_k2": nrm(ks[8], (DEPTH, HEAD_DIM), 0.1),
        "diff_subln_g": gain(ks[9], (DEPTH, 2 * HEAD_DIM)),
        "w_out": nrm(ks[10], (DEPTH, MIX_WIDTH, D_MODEL), MIX_WIDTH ** -0.5),
        "ln_cross_g": gain(ks[11], (DEPTH, D_MODEL)),
        "ln_mem_g": gain(ks[12], (DEPTH, D_MODEL)),
        "w_cq": nrm(ks[13], (DEPTH, D_MODEL, MEM_WIDTH), D_MODEL ** -0.5),
        "w_ckv": nrm(ks[14], (DEPTH, D_MODEL, 2 * MEM_WIDTH), D_MODEL ** -0.5),
        "w_co": nrm(ks[15], (DEPTH, MEM_WIDTH, D_MODEL), MEM_WIDTH ** -0.5),
        "ln_mlp_g": gain(ks[16], (DEPTH, D_MODEL)),
        "w_up": nrm(ks[17], (DEPTH, D_MODEL, D_FF), D_MODEL ** -0.5),
        "w_down": nrm(ks[18], (DEPTH, D_FF, D_MODEL), D_FF ** -0.5),
        "final_g": gain(ks[19], (D_MODEL,)),
    }


def reference(x, mem, ln_mix_g, w_in, moba_out_g, lambda_q1, lambda_k1, lambda_q2, lambda_k2,
              diff_subln_g, w_out, ln_cross_g, ln_mem_g, w_cq, w_ckv, w_co, ln_mlp_g,
              w_up, w_down, final_g):
    B, S, _ = x.shape
    cos, sin = rotary_tables(S)
    splits = [MOBA_WIDTH, 2 * MOBA_WIDTH, 3 * MOBA_WIDTH,
              3 * MOBA_WIDTH + DIFF_WIDTH, 3 * MOBA_WIDTH + 2 * DIFF_WIDTH]
    for l in range(DEPTH):
        h = rms_norm(x, ln_mix_g[l])
        proj = h @ w_in[l]
        mq, mk, mv, dq, dk, dv = jnp.split(proj, splits, axis=-1)
        mq = apply_partial_rope(to_heads(mq, MOBA_HEADS, HEAD_DIM), cos, sin)
        mk = apply_partial_rope(to_heads(mk, MOBA_HEADS, HEAD_DIM), cos, sin)
        mv = to_heads(mv, MOBA_HEADS, HEAD_DIM)
        dq = apply_partial_rope(to_heads(dq, 2 * DIFF_HEADS, HEAD_DIM), cos, sin)
        dk = apply_partial_rope(to_heads(dk, 2 * DIFF_HEADS, HEAD_DIM), cos, sin)
        dv = to_heads(dv, DIFF_HEADS, 2 * HEAD_DIM)

        mo = moba_attention(mq, mk, mv)
        mo = rms_norm(mo, moba_out_g[l].reshape(MOBA_HEADS, 1, HEAD_DIM))

        lambda_init = 0.8 - 0.6 * math.exp(-0.3 * l)
        lam = (jnp.exp(jnp.sum(lambda_q1[l].astype(jnp.float32) * lambda_k1[l].astype(jnp.float32)))
               - jnp.exp(jnp.sum(lambda_q2[l].astype(jnp.float32) * lambda_k2[l].astype(jnp.float32)))
               + lambda_init)
        do = diff_attention(dq, dk, dv, lam, lambda_init, diff_subln_g[l])

        merged = jnp.concatenate([from_heads(mo), from_heads(do)], axis=-1)
        x = x + merged @ w_out[l]

        hc = rms_norm(x, ln_cross_g[l])
        m = rms_norm(mem, ln_mem_g[l])
        x = x + memory_cross_attention(hc, m, w_cq[l], w_ckv[l], w_co[l])

        hm = rms_norm(x, ln_mlp_g[l])
        x = x + jnp.square(jax.nn.relu(hm @ w_up[l])) @ w_down[l]
    return rms_norm(x, final_g)
```

```python
import functools
import math

import jax
import jax.numpy as jnp
from jax import lax
from jax.experimental import pallas as pl
from jax.experimental.pallas import tpu as pltpu

HEAD_DIM = 128
MOBA_BLOCK = 256
MOBA_TOPK = 3
ROT_DIM = HEAD_DIM // 4
ROPE_THETA = 500000.0
MEM_HEADS = 4
EPS = 1e-5

LANES = 128
VMEM_LIMIT_BYTES = 56 * 2**20
NEG_BIG = -1e30
LOG2E = 1.4426950408889634
NT_DIMS = (((1,), (1,)), ((), ()))


def _params(*sem):
    return pltpu.CompilerParams(dimension_semantics=sem, vmem_limit_bytes=VMEM_LIMIT_BYTES)


def _rms(x, g):
    return x * lax.rsqrt(jnp.mean(x * x, axis=-1, keepdims=True) + EPS) * g


def _rmsnorm_kernel(x_ref, g_ref, o_ref):
    o_ref[...] = _rms(x_ref[...], g_ref[...]).astype(o_ref.dtype)


def _rmsnorm(x, g, out_dtype, bm=256):
    n, d = x.shape
    return pl.pallas_call(
        _rmsnorm_kernel,
        out_shape=jax.ShapeDtypeStruct((n, d), out_dtype),
        grid=(n // bm,),
        in_specs=[pl.BlockSpec((bm, d), lambda i: (i, 0)),
                  pl.BlockSpec((1, d), lambda i: (0, 0))],
        out_specs=pl.BlockSpec((bm, d), lambda i: (i, 0)),
        compiler_params=_params("parallel"),
        name="rmsnorm",
    )(x, g.reshape(1, d))


def _mm_kernel(*refs, nk, act, has_res):
    a_ref, w_ref = refs[0], refs[1]
    res_ref = refs[2] if has_res else None
    o_ref = refs[2 + has_res]
    part = jnp.dot(a_ref[...], w_ref[...], preferred_element_type=jnp.float32)

    def finish(acc):
        if act == "relu2":
            acc = jnp.square(jnp.maximum(acc, 0.0))
        if has_res:
            acc = acc + res_ref[...]
        o_ref[...] = acc.astype(o_ref.dtype)

    if nk == 1:
        finish(part)
    else:
        acc_ref = refs[3 + has_res]
        k = pl.program_id(2)

        @pl.when(k == 0)
        def _():
            acc_ref[...] = part

        @pl.when(k > 0)
        def _():
            acc_ref[...] += part

        @pl.when(k == nk - 1)
        def _():
            finish(acc_ref[...])


def _matmul(a, w, *, bm, bn, bk, out_dtype, act=None, res=None, name):
    m, kdim = a.shape
    n = w.shape[1]
    nk = kdim // bk
    in_specs = [pl.BlockSpec((bm, bk), lambda i, j, k: (i, k)),
                pl.BlockSpec((bk, bn), lambda i, j, k: (k, j))]
    args = [a, w]
    if res is not None:
        in_specs.append(pl.BlockSpec((bm, bn), lambda i, j, k: (i, j)))
        args.append(res)
    scratch = [pltpu.VMEM((bm, bn), jnp.float32)] if nk > 1 else []
    return pl.pallas_call(
        functools.partial(_mm_kernel, nk=nk, act=act, has_res=res is not None),
        out_shape=jax.ShapeDtypeStruct((m, n), out_dtype),
        grid=(m // bm, n // bn, nk),
        in_specs=in_specs,
        out_specs=pl.BlockSpec((bm, bn), lambda i, j, k: (i, j)),
        scratch_shapes=scratch,
        compiler_params=_params("parallel", "parallel", "arbitrary"),
        name=name,
    )(*args)


def _inproj_kernel(h_ref, w_ref, cos_ref, sa_ref, sb_ref, o_ref, *, blocks_per_group):
    group = pl.program_id(1) // blocks_per_group
    is_rope = jnp.logical_and(group != 2, group != 5)
    acc = jnp.dot(h_ref[...], w_ref[...], preferred_element_type=jnp.float32)

    @pl.when(is_rope)
    def _():
        cos, sa, sb = cos_ref[...], sa_ref[...], sb_ref[...]
        for c in range(acc.shape[1] // HEAD_DIM):
            a = acc[:, c * HEAD_DIM:(c + 1) * HEAD_DIM]
            r = (a * cos + pltpu.roll(a, HEAD_DIM - ROT_DIM // 2, 1) * sa
                 + pltpu.roll(a, ROT_DIM // 2, 1) * sb)
            o_ref[:, c * HEAD_DIM:(c + 1) * HEAD_DIM] = r.astype(o_ref.dtype)

    @pl.when(jnp.logical_not(is_rope))
    def _():
        o_ref[...] = acc.astype(o_ref.dtype)


def _rope_tables(seq_len):
    half = ROT_DIM // 2
    pos = jnp.arange(seq_len, dtype=jnp.float32)
    inv_freq = ROPE_THETA ** (-jnp.arange(0, ROT_DIM, 2, dtype=jnp.float32) / ROT_DIM)
    ang = pos[:, None] * inv_freq[None, :]
    cos, sin = jnp.cos(ang), jnp.sin(ang)
    z = lambda w: jnp.zeros((seq_len, w), jnp.float32)
    cos_t = jnp.concatenate([cos, cos, jnp.ones((seq_len, HEAD_DIM - ROT_DIM), jnp.float32)], axis=1)
    sin_a = jnp.concatenate([-sin, z(HEAD_DIM - half)], axis=1)
    sin_b = jnp.concatenate([z(half), sin, z(HEAD_DIM - ROT_DIM)], axis=1)
    return cos_t, sin_a, sin_b


def _inproj(h, w, seq_len, group_width, bm=1024, bn=1024):
    m, kdim = h.shape
    n = w.shape[1]
    cos_t, sin_a, sin_b = _rope_tables(seq_len)
    tspec = pl.BlockSpec((bm, HEAD_DIM), lambda i, j: (i % (seq_len // bm), 0))
    return pl.pallas_call(
        functools.partial(_inproj_kernel, blocks_per_group=group_width // bn),
        out_shape=jax.ShapeDtypeStruct((m, n), jnp.bfloat16),
        grid=(m // bm, n // bn),
        in_specs=[pl.BlockSpec((bm, kdim), lambda i, j: (i, 0)),
                  pl.BlockSpec((kdim, bn), lambda i, j: (0, j)),
                  tspec, tspec, tspec],
        out_specs=pl.BlockSpec((bm, bn), lambda i, j: (i, j)),
        compiler_params=_params("parallel", "parallel"),
        name="inproj_rope",
    )(h, w, cos_t, sin_a, sin_b)


def _softmax_step(s, v, m, l, acc, c):
    m_new = jnp.maximum(m, jnp.max(s, axis=1, keepdims=True))
    alpha = jnp.exp2((m - m_new) * c)
    p = jnp.exp2((s - m_new) * c)
    l_new = alpha * l + jnp.sum(p, axis=1, keepdims=True)
    acc_new = alpha * acc + jnp.dot(p.astype(v.dtype), v, preferred_element_type=jnp.float32)
    return m_new, l_new, acc_new


def _causal_first_step(q, k, v, c):
    s = lax.dot_general(q, k, NT_DIMS, preferred_element_type=jnp.float32)
    row = lax.broadcasted_iota(jnp.int32, s.shape, 0)
    col = lax.broadcasted_iota(jnp.int32, s.shape, 1)
    s = jnp.where(col <= row, s, NEG_BIG)
    m = jnp.max(s, axis=1, keepdims=True)
    p = jnp.exp2((s - m) * c)
    l = jnp.sum(p, axis=1, keepdims=True)
    acc = jnp.dot(p.astype(v.dtype), v, preferred_element_type=jnp.float32)
    return m, l, acc


def _moba_kernel(q_ref, k_ref, v_ref, g_ref, o_ref, kmh_ref, kml_ref, *, n_blocks):
    qi = pl.program_id(2)
    c = HEAD_DIM ** -0.5 * LOG2E

    @pl.when(qi == 0)
    def _():
        kf = k_ref[...].astype(jnp.float32).reshape(n_blocks, MOBA_BLOCK, HEAD_DIM)
        km = jnp.sum(kf, axis=1) * (1.0 / MOBA_BLOCK)
        hi = km.astype(jnp.bfloat16)
        lo = (km - hi.astype(jnp.float32)).astype(jnp.bfloat16)
        pad = jnp.zeros((LANES - n_blocks, HEAD_DIM), jnp.bfloat16)
        kmh_ref[...] = jnp.concatenate([hi, pad], axis=0)
        kml_ref[...] = jnp.concatenate([lo, pad], axis=0)

    q = q_ref[...]
    gate = (lax.dot_general(q, kmh_ref[...], NT_DIMS, preferred_element_type=jnp.float32)
            + lax.dot_general(q, kml_ref[...], NT_DIMS, preferred_element_type=jnp.float32))
    blk = lax.broadcasted_iota(jnp.int32, gate.shape, 1)
    past = blk < qi
    g = jnp.where(past, gate, -jnp.inf)
    sel = jnp.zeros(gate.shape, jnp.float32)
    for _ in range(MOBA_TOPK):
        mx = jnp.max(g, axis=1, keepdims=True)
        first = jnp.min(jnp.where(g == mx, blk, LANES), axis=1, keepdims=True)
        hit = blk == first
        sel = jnp.where(hit, jnp.where(past, 1.0, 0.0), sel)
        g = jnp.where(hit, -jnp.inf, g)
    bias = jnp.where(sel > 0.0, 0.0, NEG_BIG).astype(jnp.bfloat16)
    q_aug = jnp.concatenate([q, bias], axis=1)

    off = pl.multiple_of(qi * MOBA_BLOCK, MOBA_BLOCK)
    m, l, acc = _causal_first_step(q, k_ref[pl.ds(off, MOBA_BLOCK), :],
                                   v_ref[pl.ds(off, MOBA_BLOCK), :], c)

    lane = lax.broadcasted_iota(jnp.int32, (MOBA_BLOCK, LANES), 1)

    def body(n, carry):
        o = pl.multiple_of(n * MOBA_BLOCK, MOBA_BLOCK)
        onehot = jnp.where(lane == n, 1.0, 0.0).astype(jnp.bfloat16)
        k_aug = jnp.concatenate([k_ref[pl.ds(o, MOBA_BLOCK), :], onehot], axis=1)
        s = lax.dot_general(q_aug, k_aug, NT_DIMS, preferred_element_type=jnp.float32)
        return _softmax_step(s, v_ref[pl.ds(o, MOBA_BLOCK), :], *carry, c)

    m, l, acc = lax.fori_loop(0, qi, body, (m, l, acc))
    o_ref[...] = _rms(acc / l, g_ref[...]).astype(o_ref.dtype)


def _moba_attention(proj, out_g, n_heads, mix_width):
    b, s, _ = proj.shape
    n_blocks = s // MOBA_BLOCK
    return pl.pallas_call(
        functools.partial(_moba_kernel, n_blocks=n_blocks),
        out_shape=jax.ShapeDtypeStruct((b, s, mix_width), jnp.bfloat16),
        grid=(b, n_heads, n_blocks),
        in_specs=[pl.BlockSpec((None, MOBA_BLOCK, HEAD_DIM), lambda bi, h, qi: (bi, qi, h)),
                  pl.BlockSpec((None, s, HEAD_DIM), lambda bi, h, qi: (bi, 0, n_heads + h)),
                  pl.BlockSpec((None, s, HEAD_DIM), lambda bi, h, qi: (bi, 0, 2 * n_heads + h)),
                  pl.BlockSpec((None, 1, HEAD_DIM), lambda bi, h, qi: (h, 0, 0))],
        out_specs=pl.BlockSpec((None, MOBA_BLOCK, HEAD_DIM), lambda bi, h, qi: (bi, qi, h)),
        scratch_shapes=[pltpu.VMEM((LANES, HEAD_DIM), jnp.bfloat16),
                        pltpu.VMEM((LANES, HEAD_DIM), jnp.bfloat16)],
        compiler_params=_params("parallel", "parallel", "arbitrary"),
        name="moba_attention",
    )(proj, proj, proj, out_g.reshape(n_heads, 1, HEAD_DIM))


def _diff_kernel(lq1_ref, lk1_ref, lq2_ref, lk2_ref, q_ref, k_ref, v_ref, g_ref, merged_ref,
                 o_ref, *, tq, lambda_init):
    del merged_ref
    qi = pl.program_id(2)
    c = HEAD_DIM ** -0.5 * LOG2E
    off = pl.multiple_of(qi * tq, tq)
    outs = []
    for j in range(2):
        cols = slice(j * HEAD_DIM, (j + 1) * HEAD_DIM)
        q = q_ref[:, cols]
        state = _causal_first_step(q, k_ref[pl.ds(off, tq), cols], v_ref[pl.ds(off, tq), :], c)

        def body(n, carry, q=q, cols=cols):
            o = pl.multiple_of(n * tq, tq)
            s = lax.dot_general(q, k_ref[pl.ds(o, tq), cols], NT_DIMS,
                                preferred_element_type=jnp.float32)
            return _softmax_step(s, v_ref[pl.ds(o, tq), :], *carry, c)

        _, l, acc = lax.fori_loop(0, qi, body, state)
        outs.append(acc / l)
    lam = (jnp.exp(jnp.sum(lq1_ref[...] * lk1_ref[...], axis=1, keepdims=True))
           - jnp.exp(jnp.sum(lq2_ref[...] * lk2_ref[...], axis=1, keepdims=True))
           + lambda_init)
    o = outs[0] - lam * outs[1]
    o_ref[...] = (_rms(o, g_ref[...]) * (1.0 - lambda_init)).astype(o_ref.dtype)


def _diff_attention(proj, merged, lq1, lk1, lq2, lk2, subln_g, n_heads, col0, out_col0,
                    lambda_init, tq=512):
    b, s, _ = proj.shape
    w = 2 * HEAD_DIM
    vec = lambda x: x.reshape(1, HEAD_DIM).astype(jnp.float32)
    vspec = pl.BlockSpec((1, HEAD_DIM), lambda bi, h, qi: (0, 0))
    qb, kb, vb, ob = col0 // w, (col0 + n_heads * w) // w, (col0 + 2 * n_heads * w) // w, out_col0 // w
    return pl.pallas_call(
        functools.partial(_diff_kernel, tq=tq, lambda_init=lambda_init),
        out_shape=jax.ShapeDtypeStruct(merged.shape, merged.dtype),
        grid=(b, n_heads, s // tq),
        in_specs=[vspec, vspec, vspec, vspec,
                  pl.BlockSpec((None, tq, w), lambda bi, h, qi: (bi, qi, qb + h)),
                  pl.BlockSpec((None, s, w), lambda bi, h, qi: (bi, 0, kb + h)),
                  pl.BlockSpec((None, s, w), lambda bi, h, qi: (bi, 0, vb + h)),
                  pl.BlockSpec((1, w), lambda bi, h, qi: (0, 0)),
                  pl.BlockSpec(memory_space=pl.ANY)],
        out_specs=pl.BlockSpec((None, tq, w), lambda bi, h, qi: (bi, qi, ob + h)),
        input_output_aliases={8: 0},
        compiler_params=_params("parallel", "parallel", "arbitrary"),
        name="diff_attention",
    )(vec(lq1), vec(lk1), vec(lq2), vec(lk2), proj, proj, proj, subln_g.reshape(1, w), merged)


def _cross_kernel(x_ref, gc_ref, wq_ref, kv_ref, wo_ref, gm_ref, x2_ref, hm_ref):
    c = HEAD_DIM ** -0.5 * LOG2E
    x = x_ref[...]
    hc = _rms(x, gc_ref[...]).astype(jnp.bfloat16)
    q = jnp.dot(hc, wq_ref[...], preferred_element_type=jnp.float32).astype(jnp.bfloat16)
    kv_w = MEM_HEADS * HEAD_DIM
    outs = []
    for h in range(MEM_HEADS):
        k = kv_ref[:, h * HEAD_DIM:(h + 1) * HEAD_DIM]
        v = kv_ref[:, kv_w + h * HEAD_DIM:kv_w + (h + 1) * HEAD_DIM]
        s = lax.dot_general(q[:, h * HEAD_DIM:(h + 1) * HEAD_DIM], k, NT_DIMS,
                            preferred_element_type=jnp.float32)
        m = jnp.max(s, axis=1, keepdims=True)
        p = jnp.exp2((s - m) * c)
        l = jnp.sum(p, axis=1, keepdims=True)
        o = jnp.dot(p.astype(jnp.bfloat16), v, preferred_element_type=jnp.float32) / l
        outs.append(o.astype(jnp.bfloat16))
    o = jnp.concatenate(outs, axis=1)
    x2 = x + jnp.dot(o, wo_ref[...], preferred_element_type=jnp.float32)
    x2_ref[...] = x2
    hm_ref[...] = _rms(x2, gm_ref[...]).astype(hm_ref.dtype)


def _cross_sublayer(x, kv, g_cross, w_cq, w_co, g_mlp, seq_len, mem_len, tq=256):
    n, d = x.shape
    qw = w_cq.shape[1]
    const = lambda i: (0, 0)
    return pl.pallas_call(
        _cross_kernel,
        out_shape=(jax.ShapeDtypeStruct((n, d), jnp.float32),
                   jax.ShapeDtypeStruct((n, d), jnp.bfloat16)),
        grid=(n // tq,),
        in_specs=[pl.BlockSpec((tq, d), lambda i: (i, 0)),
                  pl.BlockSpec((1, d), const),
                  pl.BlockSpec((d, qw), const),
                  pl.BlockSpec((mem_len, 2 * qw), lambda i: (i // (seq_len // tq), 0)),
                  pl.BlockSpec((qw, d), const),
                  pl.BlockSpec((1, d), const)],
        out_specs=(pl.BlockSpec((tq, d), lambda i: (i, 0)),
                   pl.BlockSpec((tq, d), lambda i: (i, 0))),
        compiler_params=_params("parallel"),
        name="cross_sublayer",
    )(x, g_cross.reshape(1, d), w_cq, kv, w_co, g_mlp.reshape(1, d))


def kernel(x, mem, ln_mix_g, w_in, moba_out_g, lambda_q1, lambda_k1, lambda_q2, lambda_k2,
           diff_subln_g, w_out, ln_cross_g, ln_mem_g, w_cq, w_ckv, w_co, ln_mlp_g,
           w_up, w_down, final_g):
    b, s, d = x.shape
    mem_len = mem.shape[1]
    depth = w_in.shape[0]
    mix_width = w_out.shape[1]
    moba_width = mix_width // 2
    moba_heads = moba_width // HEAD_DIM
    diff_heads = (mix_width - moba_width) // (2 * HEAD_DIM)
    bf = jnp.bfloat16

    xf = x.reshape(b * s, d)
    memf = mem.reshape(b * mem_len, d)
    for l in range(depth):
        lambda_init = 0.8 - 0.6 * math.exp(-0.3 * l)

        h = _rmsnorm(xf, ln_mix_g[l], bf)
        proj = _inproj(h, w_in[l].astype(bf), s, moba_width).reshape(b, s, -1)
        merged = _moba_attention(proj, moba_out_g[l], moba_heads, mix_width)
        merged = _diff_attention(proj, merged, lambda_q1[l], lambda_k1[l], lambda_q2[l],
                                 lambda_k2[l], diff_subln_g[l], diff_heads,
                                 col0=3 * moba_width, out_col0=moba_width,
                                 lambda_init=lambda_init)
        x1 = _matmul(merged.reshape(b * s, mix_width), w_out[l].astype(bf), bm=1024, bn=1024,
                     bk=2048, out_dtype=jnp.float32, res=xf, name="outproj")

        mn = _rmsnorm(memf, ln_mem_g[l], bf)
        kv = _matmul(mn, w_ckv[l].astype(bf), bm=b * mem_len, bn=512, bk=d, out_dtype=bf,
                     name="mem_kv")
        x2, hm = _cross_sublayer(x1, kv, ln_cross_g[l], w_cq[l].astype(bf), w_co[l].astype(bf),
                                 ln_mlp_g[l], s, mem_len)

        u = _matmul(hm, w_up[l].astype(bf), bm=1024, bn=1024, bk=d, out_dtype=bf, act="relu2",
                    name="mlp_up")
        xf = _matmul(u, w_down[l].astype(bf), bm=1024, bn=1024, bk=2048, out_dtype=jnp.float32,
                     res=x2, name="mlp_down")
    return _rmsnorm(xf, final_g, jnp.float32).reshape(b, s, d)
```

```python
import functools
import math

import jax
import jax.numpy as jnp
from jax import lax
from jax.experimental import pallas as pl
from jax.experimental.pallas import tpu as pltpu

HEAD_DIM = 128
MOBA_BLOCK = 256
MOBA_TOPK = 3
ROT_DIM = HEAD_DIM // 4
ROPE_THETA = 500000.0
MEM_HEADS = 4
EPS = 1e-5

LANES = 128
VMEM_LIMIT_BYTES = 56 * 2**20
NEG_BIG = -1e30
LOG2E = 1.4426950408889634
QK_SCALE = HEAD_DIM ** -0.5 * LOG2E
NT_DIMS = (((1,), (1,)), ((), ()))
ATT_BLOCK = 256
ATT_GROUP = 2


def _params(*sem):
    return pltpu.CompilerParams(dimension_semantics=sem, vmem_limit_bytes=VMEM_LIMIT_BYTES)


def _rms(x, g):
    return x * lax.rsqrt(jnp.mean(x * x, axis=-1, keepdims=True) + EPS) * g


def _rmsnorm_kernel(x_ref, g_ref, o_ref):
    o_ref[...] = _rms(x_ref[...], g_ref[...]).astype(o_ref.dtype)


def _rmsnorm(x, g, out_dtype, bm=256):
    n, d = x.shape
    return pl.pallas_call(
        _rmsnorm_kernel,
        out_shape=jax.ShapeDtypeStruct((n, d), out_dtype),
        grid=(n // bm,),
        in_specs=[pl.BlockSpec((bm, d), lambda i: (i, 0)),
                  pl.BlockSpec((1, d), lambda i: (0, 0))],
        out_specs=pl.BlockSpec((bm, d), lambda i: (i, 0)),
        compiler_params=_params("parallel"),
        name="rmsnorm",
    )(x, g.reshape(1, d))


def _mm_kernel(*refs, nk, act, has_res):
    a_ref, w_ref = refs[0], refs[1]
    res_ref = refs[2] if has_res else None
    o_ref = refs[2 + has_res]
    part = jnp.dot(a_ref[...], w_ref[...], preferred_element_type=jnp.float32)

    def finish(acc):
        if act == "relu2":
            acc = jnp.square(jnp.maximum(acc, 0.0))
        if has_res:
            acc = acc + res_ref[...]
        o_ref[...] = acc.astype(o_ref.dtype)

    if nk == 1:
        finish(part)
    else:
        acc_ref = refs[3 + has_res]
        k = pl.program_id(2)

        @pl.when(k == 0)
        def _():
            acc_ref[...] = part

        @pl.when(k > 0)
        def _():
            acc_ref[...] += part

        @pl.when(k == nk - 1)
        def _():
            finish(acc_ref[...])


def _matmul(a, w, *, bm, bn, bk, out_dtype, act=None, res=None, name):
    m, kdim = a.shape
    n = w.shape[1]
    nk = kdim // bk
    in_specs = [pl.BlockSpec((bm, bk), lambda i, j, k: (i, k)),
                pl.BlockSpec((bk, bn), lambda i, j, k: (k, j))]
    args = [a, w]
    if res is not None:
        in_specs.append(pl.BlockSpec((bm, bn), lambda i, j, k: (i, j)))
        args.append(res)
    scratch = [pltpu.VMEM((bm, bn), jnp.float32)] if nk > 1 else []
    return pl.pallas_call(
        functools.partial(_mm_kernel, nk=nk, act=act, has_res=res is not None),
        out_shape=jax.ShapeDtypeStruct((m, n), out_dtype),
        grid=(m // bm, n // bn, nk),
        in_specs=in_specs,
        out_specs=pl.BlockSpec((bm, bn), lambda i, j, k: (i, j)),
        scratch_shapes=scratch,
        compiler_params=_params("parallel", "parallel", "arbitrary"),
        name=name,
    )(*args)


def _inproj_kernel(h_ref, w_ref, cos_ref, sa_ref, sb_ref, o_ref, *, blocks_per_group):
    group = pl.program_id(1) // blocks_per_group
    is_rope = jnp.logical_and(group != 2, group != 5)
    acc = jnp.dot(h_ref[...], w_ref[...], preferred_element_type=jnp.float32)

    @pl.when(is_rope)
    def _():
        cos, sa, sb = cos_ref[...], sa_ref[...], sb_ref[...]
        for c in range(acc.shape[1] // HEAD_DIM):
            a = acc[:, c * HEAD_DIM:(c + 1) * HEAD_DIM]
            r = (a * cos + pltpu.roll(a, HEAD_DIM - ROT_DIM // 2, 1) * sa
                 + pltpu.roll(a, ROT_DIM // 2, 1) * sb)
            o_ref[:, c * HEAD_DIM:(c + 1) * HEAD_DIM] = r.astype(o_ref.dtype)

    @pl.when(jnp.logical_not(is_rope))
    def _():
        o_ref[...] = acc.astype(o_ref.dtype)


def _rope_tables(seq_len):
    half = ROT_DIM // 2
    pos = jnp.arange(seq_len, dtype=jnp.float32)
    inv_freq = ROPE_THETA ** (-jnp.arange(0, ROT_DIM, 2, dtype=jnp.float32) / ROT_DIM)
    ang = pos[:, None] * inv_freq[None, :]
    cos, sin = jnp.cos(ang), jnp.sin(ang)
    z = lambda w: jnp.zeros((seq_len, w), jnp.float32)
    cos_t = jnp.concatenate([cos, cos, jnp.ones((seq_len, HEAD_DIM - ROT_DIM), jnp.float32)], axis=1)
    sin_a = jnp.concatenate([-sin, z(HEAD_DIM - half)], axis=1)
    sin_b = jnp.concatenate([z(half), sin, z(HEAD_DIM - ROT_DIM)], axis=1)
    return cos_t, sin_a, sin_b


def _inproj(h, w, seq_len, group_width, bm=1024, bn=1024):
    m, kdim = h.shape
    n = w.shape[1]
    bpg = group_width // bn
    tables = [jnp.stack([t, t * QK_SCALE]) for t in _rope_tables(seq_len)]
    tspec = pl.BlockSpec(
        (None, bm, HEAD_DIM),
        lambda i, j: (jnp.where((j // bpg) % 3 == 0, 1, 0), i % (seq_len // bm), 0))
    cos_t, sin_a, sin_b = tables
    return pl.pallas_call(
        functools.partial(_inproj_kernel, blocks_per_group=bpg),
        out_shape=jax.ShapeDtypeStruct((m, n), jnp.bfloat16),
        grid=(m // bm, n // bn),
        in_specs=[pl.BlockSpec((bm, kdim), lambda i, j: (i, 0)),
                  pl.BlockSpec((kdim, bn), lambda i, j: (0, j)),
                  tspec, tspec, tspec],
        out_specs=pl.BlockSpec((bm, bn), lambda i, j: (i, j)),
        compiler_params=_params("parallel", "parallel"),
        name="inproj_rope",
    )(h, w, cos_t, sin_a, sin_b)


def _scores(k, q):
    return lax.dot_general(k, q, NT_DIMS, preferred_element_type=jnp.float32)


def _diag_state(s, vt):
    key = lax.broadcasted_iota(jnp.int32, s.shape, 0)
    qry = lax.broadcasted_iota(jnp.int32, s.shape, 1)
    s = jnp.where(key <= qry, s, NEG_BIG)
    m = jnp.max(s, axis=0, keepdims=True)
    p = jnp.exp2(s - m)
    l = jnp.sum(p, axis=0, keepdims=True)
    acc = jnp.dot(vt, p.astype(vt.dtype), preferred_element_type=jnp.float32)
    return m, l, acc


def _update_state(pieces, biases, vts, state):
    m, l, acc = state
    m_new = m
    for s, b in zip(pieces, biases):
        m_new = jnp.maximum(m_new, jnp.max(s, axis=0, keepdims=True) + b)
    alpha = jnp.exp2(m - m_new)
    l = alpha * l
    acc = alpha * acc
    for s, b, vt in zip(pieces, biases, vts):
        p = jnp.exp2(s - (m_new - b))
        l = l + jnp.sum(p, axis=0, keepdims=True)
        acc = acc + jnp.dot(vt, p.astype(vt.dtype), preferred_element_type=jnp.float32)
    return m_new, l, acc


def _key_rows(k_ref, first_block, n_blocks, cols=slice(None)):
    rows = n_blocks * ATT_BLOCK
    return k_ref[pl.ds(pl.multiple_of(first_block * ATT_BLOCK, ATT_BLOCK), rows), cols]


def _moba_kernel(q_ref, k_ref, v_ref, g_ref, o_ref, kmh_ref, kml_ref, vt_ref, bias_ref, *,
                 n_blocks, n_heads):
    qi = pl.program_id(2)
    heads = [slice(h * HEAD_DIM, (h + 1) * HEAD_DIM) for h in range(n_heads)]

    @pl.when(qi == 0)
    def _():
        for h, cols in enumerate(heads):
            kf = k_ref[:, cols].astype(jnp.float32).reshape(n_blocks, MOBA_BLOCK, HEAD_DIM)
            km = jnp.sum(kf, axis=1) * (1.0 / MOBA_BLOCK)
            hi = km.astype(jnp.bfloat16)
            kmh_ref[h] = hi
            kml_ref[h] = (km - hi.astype(jnp.float32)).astype(jnp.bfloat16)
            for n in range(n_blocks):
                blk = v_ref[n * ATT_BLOCK:(n + 1) * ATT_BLOCK, cols].astype(jnp.float32)
                vt_ref[h * n_blocks + n] = blk.T.astype(vt_ref.dtype)

    qs = [q_ref[:, cols] for cols in heads]
    for h, q in enumerate(qs):
        gate = (lax.dot_general(kmh_ref[h], q, NT_DIMS, preferred_element_type=jnp.float32)
                + lax.dot_general(kml_ref[h], q, NT_DIMS, preferred_element_type=jnp.float32))
        blk = lax.broadcasted_iota(jnp.int32, gate.shape, 0)
        past = blk < qi
        g = jnp.where(past, gate, -jnp.inf)
        sel = jnp.zeros(gate.shape, jnp.float32)
        for _ in range(MOBA_TOPK):
            mx = jnp.max(g, axis=0, keepdims=True)
            first = jnp.min(jnp.where(g == mx, blk, n_blocks), axis=0, keepdims=True)
            hit = blk == first
            sel = jnp.where(hit, jnp.where(past, 1.0, 0.0), sel)
            g = jnp.where(hit, -jnp.inf, g)
        bias_ref[h] = jnp.where(sel > 0.0, 0.0, NEG_BIG)

    diag = [_scores(_key_rows(k_ref, qi, 1, cols), q) for q, cols in zip(qs, heads)]
    states = tuple(_diag_state(s, vt_ref[h * n_blocks + qi]) for h, s in enumerate(diag))

    def group(i, states):
        first = i * ATT_GROUP
        scores = [_scores(_key_rows(k_ref, first, ATT_GROUP, cols), q)
                  for q, cols in zip(qs, heads)]
        out = []
        for h, (s, st) in enumerate(zip(scores, states)):
            pieces = [s[t * ATT_BLOCK:(t + 1) * ATT_BLOCK] for t in range(ATT_GROUP)]
            biases = [bias_ref[h, pl.ds(first + t, 1), :] for t in range(ATT_GROUP)]
            vts = [vt_ref[h * n_blocks + first + t] for t in range(ATT_GROUP)]
            out.append(_update_state(pieces, biases, vts, st))
        return tuple(out)

    states = lax.fori_loop(0, lax.div(qi + (ATT_GROUP - 1), ATT_GROUP), group, states)
    for cols, (_, l, acc) in zip(heads, states):
        o_ref[:, cols] = _rms((acc / l).T, g_ref[:, cols]).astype(o_ref.dtype)


def _moba_attention(proj, out_g, n_heads, mix_width, heads_per_step=4):
    b, s, _ = proj.shape
    n_blocks = s // MOBA_BLOCK
    hp = heads_per_step
    w = hp * HEAD_DIM
    n_steps = n_heads // hp
    return pl.pallas_call(
        functools.partial(_moba_kernel, n_blocks=n_blocks, n_heads=hp),
        out_shape=jax.ShapeDtypeStruct((b, s, mix_width), jnp.bfloat16),
        grid=(b, n_steps, n_blocks),
        in_specs=[pl.BlockSpec((None, MOBA_BLOCK, w), lambda bi, h, qi: (bi, qi, h)),
                  pl.BlockSpec((None, s, w), lambda bi, h, qi: (bi, 0, n_steps + h)),
                  pl.BlockSpec((None, s, w), lambda bi, h, qi: (bi, 0, 2 * n_steps + h)),
                  pl.BlockSpec((None, 1, w), lambda bi, h, qi: (h, 0, 0))],
        out_specs=pl.BlockSpec((None, MOBA_BLOCK, w), lambda bi, h, qi: (bi, qi, h)),
        scratch_shapes=[pltpu.VMEM((hp, n_blocks, HEAD_DIM), jnp.bfloat16),
                        pltpu.VMEM((hp, n_blocks, HEAD_DIM), jnp.bfloat16),
                        pltpu.VMEM((hp * n_blocks, HEAD_DIM, MOBA_BLOCK), jnp.bfloat16),
                        pltpu.VMEM((hp, n_blocks, MOBA_BLOCK), jnp.float32)],
        compiler_params=_params("parallel", "parallel", "arbitrary"),
        name="moba_attention",
    )(proj, proj, proj, out_g.reshape(n_steps, 1, w))


def _diff_kernel(lq1_ref, lk1_ref, lq2_ref, lk2_ref, q_ref, k_ref, v_ref, g_ref, merged_ref,
                 o_ref, vt_ref, *, lambda_init, n_heads):
    del merged_ref
    qi = pl.program_id(2)
    n_blocks = vt_ref.shape[0] // n_heads
    vw = 2 * HEAD_DIM

    @pl.when(qi == 0)
    def _():
        for h in range(n_heads):
            for n in range(n_blocks):
                blk = v_ref[n * ATT_BLOCK:(n + 1) * ATT_BLOCK, h * vw:(h + 1) * vw]
                vt_ref[h * n_blocks + n] = blk.astype(jnp.float32).T.astype(vt_ref.dtype)

    subs = [slice(j * HEAD_DIM, (j + 1) * HEAD_DIM) for j in range(2 * n_heads)]
    qs = [q_ref[:, cols] for cols in subs]
    diag = [_scores(_key_rows(k_ref, qi, 1, cols), q) for q, cols in zip(qs, subs)]
    states = tuple(_diag_state(s, vt_ref[(j // 2) * n_blocks + qi]) for j, s in enumerate(diag))

    def group(i, states):
        first = i * ATT_GROUP
        scores = [_scores(_key_rows(k_ref, first, ATT_GROUP, cols), q) for q, cols in zip(qs, subs)]
        biases = [jnp.where(first + t < qi, 0.0, NEG_BIG) for t in range(ATT_GROUP)]
        out = []
        for j, (s, st) in enumerate(zip(scores, states)):
            vts = [vt_ref[(j // 2) * n_blocks + first + t] for t in range(ATT_GROUP)]
            pieces = [s[t * ATT_BLOCK:(t + 1) * ATT_BLOCK] for t in range(ATT_GROUP)]
            out.append(_update_state(pieces, biases, vts, st))
        return tuple(out)

    states = lax.fori_loop(0, lax.div(qi + (ATT_GROUP - 1), ATT_GROUP), group, states)
    outs = [acc / l for _, l, acc in states]
    lam = (jnp.exp(jnp.sum(lq1_ref[...] * lk1_ref[...], axis=1, keepdims=True))
           - jnp.exp(jnp.sum(lq2_ref[...] * lk2_ref[...], axis=1, keepdims=True))
           + lambda_init)
    for h in range(n_heads):
        o = (outs[2 * h] - lam * outs[2 * h + 1]).T
        o_ref[:, h * vw:(h + 1) * vw] = (_rms(o, g_ref[...]) * (1.0 - lambda_init)).astype(o_ref.dtype)


def _diff_attention(proj, merged, lq1, lk1, lq2, lk2, subln_g, n_heads, col0, out_col0,
                    lambda_init, heads_per_step=2):
    b, s, _ = proj.shape
    hp = heads_per_step
    w = hp * 2 * HEAD_DIM
    n_steps = n_heads // hp
    tq = ATT_BLOCK
    vec = lambda x: x.reshape(1, HEAD_DIM).astype(jnp.float32)
    vspec = pl.BlockSpec((1, HEAD_DIM), lambda bi, h, qi: (0, 0))
    qb, kb, vb, ob = col0 // w, (col0 + n_steps * w) // w, (col0 + 2 * n_steps * w) // w, out_col0 // w
    return pl.pallas_call(
        functools.partial(_diff_kernel, lambda_init=lambda_init, n_heads=hp),
        out_shape=jax.ShapeDtypeStruct(merged.shape, merged.dtype),
        grid=(b, n_steps, s // tq),
        in_specs=[vspec, vspec, vspec, vspec,
                  pl.BlockSpec((None, tq, w), lambda bi, h, qi: (bi, qi, qb + h)),
                  pl.BlockSpec((None, s, w), lambda bi, h, qi: (bi, 0, kb + h)),
                  pl.BlockSpec((None, s, w), lambda bi, h, qi: (bi, 0, vb + h)),
                  pl.BlockSpec((1, 2 * HEAD_DIM), lambda bi, h, qi: (0, 0)),
                  pl.BlockSpec(memory_space=pl.ANY)],
        out_specs=pl.BlockSpec((None, tq, w), lambda bi, h, qi: (bi, qi, ob + h)),
        scratch_shapes=[pltpu.VMEM((hp * (s // ATT_BLOCK), 2 * HEAD_DIM, ATT_BLOCK), jnp.bfloat16)],
        input_output_aliases={8: 0},
        compiler_params=_params("parallel", "parallel", "arbitrary"),
        name="diff_attention",
    )(vec(lq1), vec(lk1), vec(lq2), vec(lk2), proj, proj, proj,
      subln_g.reshape(1, 2 * HEAD_DIM), merged)


def _cross_kernel(x_ref, gc_ref, wq_ref, kv_ref, wo_ref, gm_ref, x2_ref, hm_ref):
    c = QK_SCALE
    x = x_ref[...]
    hc = _rms(x, gc_ref[...]).astype(jnp.bfloat16)
    q = jnp.dot(hc, wq_ref[...], preferred_element_type=jnp.float32).astype(jnp.bfloat16)
    kv_w = MEM_HEADS * HEAD_DIM
    outs = []
    for h in range(MEM_HEADS):
        k = kv_ref[:, h * HEAD_DIM:(h + 1) * HEAD_DIM]
        v = kv_ref[:, kv_w + h * HEAD_DIM:kv_w + (h + 1) * HEAD_DIM]
        s = lax.dot_general(q[:, h * HEAD_DIM:(h + 1) * HEAD_DIM], k, NT_DIMS,
                            preferred_element_type=jnp.float32)
        m = jnp.max(s, axis=1, keepdims=True)
        p = jnp.exp2((s - m) * c)
        l = jnp.sum(p, axis=1, keepdims=True)
        o = jnp.dot(p.astype(jnp.bfloat16), v, preferred_element_type=jnp.float32) / l
        outs.append(o.astype(jnp.bfloat16))
    o = jnp.concatenate(outs, axis=1)
    x2 = x + jnp.dot(o, wo_ref[...], preferred_element_type=jnp.float32)
    x2_ref[...] = x2
    hm_ref[...] = _rms(x2, gm_ref[...]).astype(hm_ref.dtype)


def _cross_sublayer(x, kv, g_cross, w_cq, w_co, g_mlp, seq_len, mem_len, tq=256):
    n, d = x.shape
    qw = w_cq.shape[1]
    const = lambda i: (0, 0)
    return pl.pallas_call(
        _cross_kernel,
        out_shape=(jax.ShapeDtypeStruct((n, d), jnp.float32),
                   jax.ShapeDtypeStruct((n, d), jnp.bfloat16)),
        grid=(n // tq,),
        in_specs=[pl.BlockSpec((tq, d), lambda i: (i, 0)),
                  pl.BlockSpec((1, d), const),
                  pl.BlockSpec((d, qw), const),
                  pl.BlockSpec((mem_len, 2 * qw), lambda i: (i // (seq_len // tq), 0)),
                  pl.BlockSpec((qw, d), const),
                  pl.BlockSpec((1, d), const)],
        out_specs=(pl.BlockSpec((tq, d), lambda i: (i, 0)),
                   pl.BlockSpec((tq, d), lambda i: (i, 0))),
        compiler_params=_params("parallel"),
        name="cross_sublayer",
    )(x, g_cross.reshape(1, d), w_cq, kv, w_co, g_mlp.reshape(1, d))


def kernel(x, mem, ln_mix_g, w_in, moba_out_g, lambda_q1, lambda_k1, lambda_q2, lambda_k2,
           diff_subln_g, w_out, ln_cross_g, ln_mem_g, w_cq, w_ckv, w_co, ln_mlp_g,
           w_up, w_down, final_g):
    b, s, d = x.shape
    mem_len = mem.shape[1]
    depth = w_in.shape[0]
    mix_width = w_out.shape[1]
    moba_width = mix_width // 2
    moba_heads = moba_width // HEAD_DIM
    diff_heads = (mix_width - moba_width) // (2 * HEAD_DIM)
    bf = jnp.bfloat16

    xf = x.reshape(b * s, d)
    memf = mem.reshape(b * mem_len, d)
    for l in range(depth):
        lambda_init = 0.8 - 0.6 * math.exp(-0.3 * l)

        h = _rmsnorm(xf, ln_mix_g[l], bf)
        proj = _inproj(h, w_in[l].astype(bf), s, moba_width).reshape(b, s, -1)
        merged = _moba_attention(proj, moba_out_g[l], moba_heads, mix_width)
        merged = _diff_attention(proj, merged, lambda_q1[l], lambda_k1[l], lambda_q2[l],
                                 lambda_k2[l], diff_subln_g[l], diff_heads,
                                 col0=3 * moba_width, out_col0=moba_width,
                                 lambda_init=lambda_init)
        x1 = _matmul(merged.reshape(b * s, mix_width), w_out[l].astype(bf), bm=1024, bn=1024,
                     bk=2048, out_dtype=jnp.float32, res=xf, name="outproj")

        mn = _rmsnorm(memf, ln_mem_g[l], bf)
        kv = _matmul(mn, w_ckv[l].astype(bf), bm=b * mem_len, bn=512, bk=d, out_dtype=bf,
                     name="mem_kv")
        x2, hm = _cross_sublayer(x1, kv, ln_cross_g[l], w_cq[l].astype(bf), w_co[l].astype(bf),
                                 ln_mlp_g[l], s, mem_len)

        u = _matmul(hm, w_up[l].astype(bf), bm=1024, bn=1024, bk=d, out_dtype=bf, act="relu2",
                    name="mlp_up")
        xf = _matmul(u, w_down[l].astype(bf), bm=1024, bn=1024, bk=2048, out_dtype=jnp.float32,
                     res=x2, name="mlp_down")
    return _rmsnorm(xf, final_g, jnp.float32).reshape(b, s, d)
```

```python
import functools
import math

import jax
import jax.numpy as jnp
from jax import lax
from jax.experimental import pallas as pl
from jax.experimental.pallas import tpu as pltpu

HEAD_DIM = 128
MOBA_BLOCK = 256
MOBA_TOPK = 3
ROT_DIM = HEAD_DIM // 4
ROPE_THETA = 500000.0
MEM_HEADS = 4
EPS = 1e-5

LANES = 128
VMEM_LIMIT_BYTES = 56 * 2**20
NEG_BIG = -1e30
LOG2E = 1.4426950408889634
QK_SCALE = HEAD_DIM ** -0.5 * LOG2E
NT_DIMS = (((1,), (1,)), ((), ()))
ATT_BLOCK = 256
ATT_GROUP = 2
MM_CHUNK = 256


def _params(*sem):
    return pltpu.CompilerParams(dimension_semantics=sem, vmem_limit_bytes=VMEM_LIMIT_BYTES)


def _rms(x, g):
    return x * lax.rsqrt(jnp.mean(x * x, axis=-1, keepdims=True) + EPS) * g


def _rmsnorm_kernel(x_ref, g_ref, o_ref):
    o_ref[...] = _rms(x_ref[...], g_ref[...]).astype(o_ref.dtype)


def _rmsnorm(x, g, out_dtype, bm=256):
    n, d = x.shape
    return pl.pallas_call(
        _rmsnorm_kernel,
        out_shape=jax.ShapeDtypeStruct((n, d), out_dtype),
        grid=(n // bm,),
        in_specs=[pl.BlockSpec((bm, d), lambda i: (i, 0)),
                  pl.BlockSpec((1, d), lambda i: (0, 0))],
        out_specs=pl.BlockSpec((bm, d), lambda i: (i, 0)),
        compiler_params=_params("parallel"),
        name="rmsnorm",
    )(x, g.reshape(1, d))


def _mm_kernel(*refs, nk, act, has_res):
    a_ref, w_ref = refs[0], refs[1]
    res_ref = refs[2] if has_res else None
    o_ref = refs[2 + has_res]
    if nk > 1:
        assert act is None and o_ref.dtype == jnp.float32

        @pl.when(pl.program_id(2) == 0)
        def _():
            o_ref[...] = res_ref[...] if has_res else jnp.zeros_like(o_ref)

    for c in range(o_ref.shape[1] // MM_CHUNK):
        cols = slice(c * MM_CHUNK, (c + 1) * MM_CHUNK)
        acc = jnp.dot(a_ref[...], w_ref[:, cols], preferred_element_type=jnp.float32)
        if nk > 1:
            o_ref[:, cols] = acc + o_ref[:, cols]
        else:
            if act == "relu2":
                acc = jnp.square(jnp.maximum(acc, 0.0))
            if has_res:
                acc = acc + res_ref[:, cols]
            o_ref[:, cols] = acc.astype(o_ref.dtype)


def _matmul(a, w, *, bm, bn, bk, out_dtype, act=None, res=None, name):
    m, kdim = a.shape
    n = w.shape[1]
    nk = kdim // bk
    in_specs = [pl.BlockSpec((bm, bk), lambda i, j, k: (i, k)),
                pl.BlockSpec((bk, bn), lambda i, j, k: (k, j))]
    args = [a, w]
    if res is not None:
        in_specs.append(pl.BlockSpec((bm, bn), lambda i, j, k: (i, j)))
        args.append(res)
    return pl.pallas_call(
        functools.partial(_mm_kernel, nk=nk, act=act, has_res=res is not None),
        out_shape=jax.ShapeDtypeStruct((m, n), out_dtype),
        grid=(m // bm, n // bn, nk),
        in_specs=in_specs,
        out_specs=pl.BlockSpec((bm, bn), lambda i, j, k: (i, j)),
        compiler_params=_params("parallel", "parallel", "arbitrary"),
        name=name,
    )(*args)


def _inproj_kernel(h_ref, w_ref, cos_ref, sa_ref, sb_ref, o_ref):
    cos, sa, sb = cos_ref[...], sa_ref[...], sb_ref[...]
    for c in range(o_ref.shape[1] // MM_CHUNK):
        acc = jnp.dot(h_ref[...], w_ref[:, c * MM_CHUNK:(c + 1) * MM_CHUNK],
                      preferred_element_type=jnp.float32)
        for hd in range(MM_CHUNK // HEAD_DIM):
            a = acc[:, hd * HEAD_DIM:(hd + 1) * HEAD_DIM]
            r = (a * cos + pltpu.roll(a, HEAD_DIM - ROT_DIM // 2, 1) * sa
                 + pltpu.roll(a, ROT_DIM // 2, 1) * sb)
            col = c * MM_CHUNK + hd * HEAD_DIM
            o_ref[:, col:col + HEAD_DIM] = r.astype(o_ref.dtype)


def _rope_tables(seq_len):
    half = ROT_DIM // 2
    pos = jnp.arange(seq_len, dtype=jnp.float32)
    inv_freq = ROPE_THETA ** (-jnp.arange(0, ROT_DIM, 2, dtype=jnp.float32) / ROT_DIM)
    ang = pos[:, None] * inv_freq[None, :]
    cos, sin = jnp.cos(ang), jnp.sin(ang)
    z = lambda w: jnp.zeros((seq_len, w), jnp.float32)
    cos_t = jnp.concatenate([cos, cos, jnp.ones((seq_len, HEAD_DIM - ROT_DIM), jnp.float32)], axis=1)
    sin_a = jnp.concatenate([-sin, z(HEAD_DIM - half)], axis=1)
    sin_b = jnp.concatenate([z(half), sin, z(HEAD_DIM - ROT_DIM)], axis=1)
    return cos_t, sin_a, sin_b


def _inproj(h, w, seq_len, group_width, bm=1024, bn=1024):
    m, kdim = h.shape
    n = w.shape[1]
    bpg = group_width // bn
    cos_t, sin_a, sin_b = _rope_tables(seq_len)
    cos_t = jnp.stack([cos_t * QK_SCALE, cos_t, jnp.ones_like(cos_t)])
    sin_a = jnp.stack([sin_a * QK_SCALE, sin_a, jnp.zeros_like(sin_a)])
    sin_b = jnp.stack([sin_b * QK_SCALE, sin_b, jnp.zeros_like(sin_b)])
    tspec = pl.BlockSpec((None, bm, HEAD_DIM),
                         lambda i, j: ((j // bpg) % 3, i % (seq_len // bm), 0))
    return pl.pallas_call(
        _inproj_kernel,
        out_shape=jax.ShapeDtypeStruct((m, n), jnp.bfloat16),
        grid=(m // bm, n // bn),
        in_specs=[pl.BlockSpec((bm, kdim), lambda i, j: (i, 0)),
                  pl.BlockSpec((kdim, bn), lambda i, j: (0, j)),
                  tspec, tspec, tspec],
        out_specs=pl.BlockSpec((bm, bn), lambda i, j: (i, j)),
        compiler_params=_params("parallel", "parallel"),
        name="inproj_rope",
    )(h, w, cos_t, sin_a, sin_b)


def _scores(k, q):
    return lax.dot_general(k, q, NT_DIMS, preferred_element_type=jnp.float32)


def _diag_state(s, vt):
    key = lax.broadcasted_iota(jnp.int32, s.shape, 0)
    qry = lax.broadcasted_iota(jnp.int32, s.shape, 1)
    s = jnp.where(key <= qry, s, NEG_BIG)
    m = jnp.max(s, axis=0, keepdims=True)
    p = jnp.exp2(s - m)
    l = jnp.sum(p, axis=0, keepdims=True)
    acc = jnp.dot(vt, p.astype(vt.dtype), preferred_element_type=jnp.float32)
    return m, l, acc


def _update_state(pieces, biases, vts, state):
    m, l, acc = state
    m_new = m
    for s, b in zip(pieces, biases):
        m_new = jnp.maximum(m_new, jnp.max(s, axis=0, keepdims=True) + b)
    alpha = jnp.exp2(m - m_new)
    l = alpha * l
    acc = alpha * acc
    for s, b, vt in zip(pieces, biases, vts):
        p = jnp.exp2(s - (m_new - b))
        l = l + jnp.sum(p, axis=0, keepdims=True)
        acc = acc + jnp.dot(vt, p.astype(vt.dtype), preferred_element_type=jnp.float32)
    return m_new, l, acc


def _key_rows(k_ref, first_block, n_blocks, cols=slice(None)):
    rows = n_blocks * ATT_BLOCK
    return k_ref[pl.ds(pl.multiple_of(first_block * ATT_BLOCK, ATT_BLOCK), rows), cols]


def _moba_kernel(q_ref, k_ref, v_ref, g_ref, o_ref, kmh_ref, kml_ref, vt_ref, bias_ref, *,
                 n_blocks, n_heads):
    qi = pl.program_id(2)
    heads = [slice(h * HEAD_DIM, (h + 1) * HEAD_DIM) for h in range(n_heads)]

    @pl.when(qi == 0)
    def _():
        for h, cols in enumerate(heads):
            kf = k_ref[:, cols].astype(jnp.float32).reshape(n_blocks, MOBA_BLOCK, HEAD_DIM)
            km = jnp.sum(kf, axis=1) * (1.0 / MOBA_BLOCK)
            hi = km.astype(jnp.bfloat16)
            kmh_ref[h] = hi
            kml_ref[h] = (km - hi.astype(jnp.float32)).astype(jnp.bfloat16)
            for n in range(n_blocks):
                blk = v_ref[n * ATT_BLOCK:(n + 1) * ATT_BLOCK, cols].astype(jnp.float32)
                vt_ref[h * n_blocks + n] = blk.T.astype(vt_ref.dtype)

    qs = [q_ref[:, cols] for cols in heads]
    for h, q in enumerate(qs):
        gate = (lax.dot_general(kmh_ref[h], q, NT_DIMS, preferred_element_type=jnp.float32)
                + lax.dot_general(kml_ref[h], q, NT_DIMS, preferred_element_type=jnp.float32))
        blk = lax.broadcasted_iota(jnp.int32, gate.shape, 0)
        past = blk < qi
        g = jnp.where(past, gate, -jnp.inf)
        sel = jnp.zeros(gate.shape, jnp.float32)
        for _ in range(MOBA_TOPK):
            mx = jnp.max(g, axis=0, keepdims=True)
            first = jnp.min(jnp.where(g == mx, blk, n_blocks), axis=0, keepdims=True)
            hit = blk == first
            sel = jnp.where(hit, jnp.where(past, 1.0, 0.0), sel)
            g = jnp.where(hit, -jnp.inf, g)
        bias_ref[h] = jnp.where(sel > 0.0, 0.0, NEG_BIG)

    diag = [_scores(_key_rows(k_ref, qi, 1, cols), q) for q, cols in zip(qs, heads)]
    states = tuple(_diag_state(s, vt_ref[h * n_blocks + qi]) for h, s in enumerate(diag))

    def group(i, states):
        first = i * ATT_GROUP
        scores = [_scores(_key_rows(k_ref, first, ATT_GROUP, cols), q)
                  for q, cols in zip(qs, heads)]
        out = []
        for h, (s, st) in enumerate(zip(scores, states)):
            pieces = [s[t * ATT_BLOCK:(t + 1) * ATT_BLOCK] for t in range(ATT_GROUP)]
            biases = [bias_ref[h, pl.ds(first + t, 1), :] for t in range(ATT_GROUP)]
            vts = [vt_ref[h * n_blocks + first + t] for t in range(ATT_GROUP)]
            out.append(_update_state(pieces, biases, vts, st))
        return tuple(out)

    states = lax.fori_loop(0, lax.div(qi + (ATT_GROUP - 1), ATT_GROUP), group, states)
    for cols, (_, l, acc) in zip(heads, states):
        o_ref[:, cols] = _rms((acc / l).T, g_ref[:, cols]).astype(o_ref.dtype)


def _moba_attention(proj, out_g, n_heads, mix_width, heads_per_step=4):
    b, s, _ = proj.shape
    n_blocks = s // MOBA_BLOCK
    hp = heads_per_step
    w = hp * HEAD_DIM
    n_steps = n_heads // hp
    return pl.pallas_call(
        functools.partial(_moba_kernel, n_blocks=n_blocks, n_heads=hp),
        out_shape=jax.ShapeDtypeStruct((b, s, mix_width), jnp.bfloat16),
        grid=(b, n_steps, n_blocks),
        in_specs=[pl.BlockSpec((None, MOBA_BLOCK, w), lambda bi, h, qi: (bi, qi, h)),
                  pl.BlockSpec((None, s, w), lambda bi, h, qi: (bi, 0, n_steps + h)),
                  pl.BlockSpec((None, s, w), lambda bi, h, qi: (bi, 0, 2 * n_steps + h)),
                  pl.BlockSpec((None, 1, w), lambda bi, h, qi: (h, 0, 0))],
        out_specs=pl.BlockSpec((None, MOBA_BLOCK, w), lambda bi, h, qi: (bi, qi, h)),
        scratch_shapes=[pltpu.VMEM((hp, n_blocks, HEAD_DIM), jnp.bfloat16),
                        pltpu.VMEM((hp, n_blocks, HEAD_DIM), jnp.bfloat16),
                        pltpu.VMEM((hp * n_blocks, HEAD_DIM, MOBA_BLOCK), jnp.bfloat16),
                        pltpu.VMEM((hp, n_blocks, MOBA_BLOCK), jnp.float32)],
        compiler_params=_params("parallel", "parallel", "arbitrary"),
        name="moba_attention",
    )(proj, proj, proj, out_g.reshape(n_steps, 1, w))


def _diff_kernel(lq1_ref, lk1_ref, lq2_ref, lk2_ref, q_ref, k_ref, v_ref, g_ref, merged_ref,
                 o_ref, vt_ref, *, lambda_init, n_heads):
    del merged_ref
    qi = pl.program_id(2)
    n_blocks = vt_ref.shape[0] // n_heads
    vw = 2 * HEAD_DIM

    @pl.when(qi == 0)
    def _():
        for h in range(n_heads):
            for n in range(n_blocks):
                blk = v_ref[n * ATT_BLOCK:(n + 1) * ATT_BLOCK, h * vw:(h + 1) * vw]
                vt_ref[h * n_blocks + n] = blk.astype(jnp.float32).T.astype(vt_ref.dtype)

    subs = [slice(j * HEAD_DIM, (j + 1) * HEAD_DIM) for j in range(2 * n_heads)]
    qs = [q_ref[:, cols] for cols in subs]
    diag = [_scores(_key_rows(k_ref, qi, 1, cols), q) for q, cols in zip(qs, subs)]
    states = tuple(_diag_state(s, vt_ref[(j // 2) * n_blocks + qi]) for j, s in enumerate(diag))

    def group(i, states):
        first = i * ATT_GROUP
        scores = [_scores(_key_rows(k_ref, first, ATT_GROUP, cols), q) for q, cols in zip(qs, subs)]
        biases = [jnp.where(first + t < qi, 0.0, NEG_BIG) for t in range(ATT_GROUP)]
        out = []
        for j, (s, st) in enumerate(zip(scores, states)):
            vts = [vt_ref[(j // 2) * n_blocks + first + t] for t in range(ATT_GROUP)]
            pieces = [s[t * ATT_BLOCK:(t + 1) * ATT_BLOCK] for t in range(ATT_GROUP)]
            out.append(_update_state(pieces, biases, vts, st))
        return tuple(out)

    states = lax.fori_loop(0, lax.div(qi + (ATT_GROUP - 1), ATT_GROUP), group, states)
    outs = [acc / l for _, l, acc in states]
    lam = (jnp.exp(jnp.sum(lq1_ref[...] * lk1_ref[...], axis=1, keepdims=True))
           - jnp.exp(jnp.sum(lq2_ref[...] * lk2_ref[...], axis=1, keepdims=True))
           + lambda_init)
    for h in range(n_heads):
        o = (outs[2 * h] - lam * outs[2 * h + 1]).T
        o_ref[:, h * vw:(h + 1) * vw] = (_rms(o, g_ref[...]) * (1.0 - lambda_init)).astype(o_ref.dtype)


def _diff_attention(proj, merged, lq1, lk1, lq2, lk2, subln_g, n_heads, col0, out_col0,
                    lambda_init, heads_per_step=2):
    b, s, _ = proj.shape
    hp = heads_per_step
    w = hp * 2 * HEAD_DIM
    n_steps = n_heads // hp
    tq = ATT_BLOCK
    vec = lambda x: x.reshape(1, HEAD_DIM).astype(jnp.float32)
    vspec = pl.BlockSpec((1, HEAD_DIM), lambda bi, h, qi: (0, 0))
    qb, kb, vb, ob = col0 // w, (col0 + n_steps * w) // w, (col0 + 2 * n_steps * w) // w, out_col0 // w
    return pl.pallas_call(
        functools.partial(_diff_kernel, lambda_init=lambda_init, n_heads=hp),
        out_shape=jax.ShapeDtypeStruct(merged.shape, merged.dtype),
        grid=(b, n_steps, s // tq),
        in_specs=[vspec, vspec, vspec, vspec,
                  pl.BlockSpec((None, tq, w), lambda bi, h, qi: (bi, qi, qb + h)),
                  pl.BlockSpec((None, s, w), lambda bi, h, qi: (bi, 0, kb + h)),
                  pl.BlockSpec((None, s, w), lambda bi, h, qi: (bi, 0, vb + h)),
                  pl.BlockSpec((1, 2 * HEAD_DIM), lambda bi, h, qi: (0, 0)),
                  pl.BlockSpec(memory_space=pl.ANY)],
        out_specs=pl.BlockSpec((None, tq, w), lambda bi, h, qi: (bi, qi, ob + h)),
        scratch_shapes=[pltpu.VMEM((hp * (s // ATT_BLOCK), 2 * HEAD_DIM, ATT_BLOCK), jnp.bfloat16)],
        input_output_aliases={8: 0},
        compiler_params=_params("parallel", "parallel", "arbitrary"),
        name="diff_attention",
    )(vec(lq1), vec(lk1), vec(lq2), vec(lk2), proj, proj, proj,
      subln_g.reshape(1, 2 * HEAD_DIM), merged)


def _cross_kernel(x_ref, gc_ref, wq_ref, kv_ref, wo_ref, gm_ref, x2_ref, hm_ref):
    c = QK_SCALE
    x = x_ref[...]
    hc = _rms(x, gc_ref[...]).astype(jnp.bfloat16)
    q = jnp.dot(hc, wq_ref[...], preferred_element_type=jnp.float32).astype(jnp.bfloat16)
    kv_w = MEM_HEADS * HEAD_DIM
    outs = []
    for h in range(MEM_HEADS):
        k = kv_ref[:, h * HEAD_DIM:(h + 1) * HEAD_DIM]
        v = kv_ref[:, kv_w + h * HEAD_DIM:kv_w + (h + 1) * HEAD_DIM]
        s = lax.dot_general(q[:, h * HEAD_DIM:(h + 1) * HEAD_DIM], k, NT_DIMS,
                            preferred_element_type=jnp.float32)
        m = jnp.max(s, axis=1, keepdims=True)
        p = jnp.exp2((s - m) * c)
        l = jnp.sum(p, axis=1, keepdims=True)
        o = jnp.dot(p.astype(jnp.bfloat16), v, preferred_element_type=jnp.float32) / l
        outs.append(o.astype(jnp.bfloat16))
    o = jnp.concatenate(outs, axis=1)
    x2 = x + jnp.dot(o, wo_ref[...], preferred_element_type=jnp.float32)
    x2_ref[...] = x2
    hm_ref[...] = _rms(x2, gm_ref[...]).astype(hm_ref.dtype)


def _cross_sublayer(x, kv, g_cross, w_cq, w_co, g_mlp, seq_len, mem_len, tq=256):
    n, d = x.shape
    qw = w_cq.shape[1]
    const = lambda i: (0, 0)
    return pl.pallas_call(
        _cross_kernel,
        out_shape=(jax.ShapeDtypeStruct((n, d), jnp.float32),
                   jax.ShapeDtypeStruct((n, d), jnp.bfloat16)),
        grid=(n // tq,),
        in_specs=[pl.BlockSpec((tq, d), lambda i: (i, 0)),
                  pl.BlockSpec((1, d), const),
                  pl.BlockSpec((d, qw), const),
                  pl.BlockSpec((mem_len, 2 * qw), lambda i: (i // (seq_len // tq), 0)),
                  pl.BlockSpec((qw, d), const),
                  pl.BlockSpec((1, d), const)],
        out_specs=(pl.BlockSpec((tq, d), lambda i: (i, 0)),
                   pl.BlockSpec((tq, d), lambda i: (i, 0))),
        compiler_params=_params("parallel"),
        name="cross_sublayer",
    )(x, g_cross.reshape(1, d), w_cq, kv, w_co, g_mlp.reshape(1, d))


def kernel(x, mem, ln_mix_g, w_in, moba_out_g, lambda_q1, lambda_k1, lambda_q2, lambda_k2,
           diff_subln_g, w_out, ln_cross_g, ln_mem_g, w_cq, w_ckv, w_co, ln_mlp_g,
           w_up, w_down, final_g):
    b, s, d = x.shape
    mem_len = mem.shape[1]
    depth = w_in.shape[0]
    mix_width = w_out.shape[1]
    moba_width = mix_width // 2
    moba_heads = moba_width // HEAD_DIM
    diff_heads = (mix_width - moba_width) // (2 * HEAD_DIM)
    bf = jnp.bfloat16

    xf = x.reshape(b * s, d)
    memf = mem.reshape(b * mem_len, d)
    for l in range(depth):
        lambda_init = 0.8 - 0.6 * math.exp(-0.3 * l)

        h = _rmsnorm(xf, ln_mix_g[l], bf)
        proj = _inproj(h, w_in[l].astype(bf), s, moba_width).reshape(b, s, -1)
        merged = _moba_attention(proj, moba_out_g[l], moba_heads, mix_width)
        merged = _diff_attention(proj, merged, lambda_q1[l], lambda_k1[l], lambda_q2[l],
                                 lambda_k2[l], diff_subln_g[l], diff_heads,
                                 col0=3 * moba_width, out_col0=moba_width,
                                 lambda_init=lambda_init)
        x1 = _matmul(merged.reshape(b * s, mix_width), w_out[l].astype(bf), bm=512, bn=1024,
                     bk=mix_width, out_dtype=jnp.float32, res=xf, name="outproj")

        mn = _rmsnorm(memf, ln_mem_g[l], bf)
        kv = _matmul(mn, w_ckv[l].astype(bf), bm=b * mem_len, bn=512, bk=d, out_dtype=bf,
                     name="mem_kv")
        x2, hm = _cross_sublayer(x1, kv, ln_cross_g[l], w_cq[l].astype(bf), w_co[l].astype(bf),
                                 ln_mlp_g[l], s, mem_len)

        u = _matmul(hm, w_up[l].astype(bf), bm=1024, bn=1024, bk=d, out_dtype=bf, act="relu2",
                    name="mlp_up")
        xf = _matmul(u, w_down[l].astype(bf), bm=1024, bn=1024, bk=2048, out_dtype=jnp.float32,
                     res=x2, name="mlp_down")
    return _rmsnorm(xf, final_g, jnp.float32).reshape(b, s, d)
```

```python
import functools
import math

import jax
import jax.numpy as jnp
from jax import lax
from jax.experimental import pallas as pl
from jax.experimental.pallas import tpu as pltpu

HEAD_DIM = 128
MOBA_BLOCK = 256
MOBA_TOPK = 3
ROT_DIM = HEAD_DIM // 4
ROPE_THETA = 500000.0
MEM_HEADS = 4
EPS = 1e-5

LANES = 128
VMEM_LIMIT_BYTES = 56 * 2**20
NEG_BIG = -1e30
LOG2E = 1.4426950408889634
QK_SCALE = HEAD_DIM ** -0.5 * LOG2E
NT_DIMS = (((1,), (1,)), ((), ()))
ATT_BLOCK = 256
ATT_GROUP = 2
MM_CHUNK = 256


def _params(*sem):
    return pltpu.CompilerParams(dimension_semantics=sem, vmem_limit_bytes=VMEM_LIMIT_BYTES)


def _rms(x, g):
    return x * lax.rsqrt(jnp.mean(x * x, axis=-1, keepdims=True) + EPS) * g


def _rmsnorm_kernel(x_ref, g_ref, o_ref):
    o_ref[...] = _rms(x_ref[...], g_ref[...]).astype(o_ref.dtype)


def _rmsnorm(x, g, out_dtype, bm=256):
    n, d = x.shape
    return pl.pallas_call(
        _rmsnorm_kernel,
        out_shape=jax.ShapeDtypeStruct((n, d), out_dtype),
        grid=(n // bm,),
        in_specs=[pl.BlockSpec((bm, d), lambda i: (i, 0)),
                  pl.BlockSpec((1, d), lambda i: (0, 0))],
        out_specs=pl.BlockSpec((bm, d), lambda i: (i, 0)),
        compiler_params=_params("parallel"),
        name="rmsnorm",
    )(x, g.reshape(1, d))


def _mm_kernel(*refs, nk, act, has_res):
    a_ref, w_ref = refs[0], refs[1]
    res_ref = refs[2] if has_res else None
    o_ref = refs[2 + has_res]
    if nk > 1:
        assert act is None and o_ref.dtype == jnp.float32

        @pl.when(pl.program_id(2) == 0)
        def _():
            o_ref[...] = res_ref[...] if has_res else jnp.zeros_like(o_ref)

    for c in range(o_ref.shape[1] // MM_CHUNK):
        cols = slice(c * MM_CHUNK, (c + 1) * MM_CHUNK)
        acc = jnp.dot(a_ref[...], w_ref[:, cols], preferred_element_type=jnp.float32)
        if nk > 1:
            o_ref[:, cols] = acc + o_ref[:, cols]
        else:
            if act == "relu2":
                acc = jnp.square(jnp.maximum(acc, 0.0))
            if has_res:
                acc = acc + res_ref[:, cols]
            o_ref[:, cols] = acc.astype(o_ref.dtype)


def _matmul(a, w, *, bm, bn, bk, out_dtype, act=None, res=None, name):
    m, kdim = a.shape
    n = w.shape[1]
    nk = kdim // bk
    in_specs = [pl.BlockSpec((bm, bk), lambda i, j, k: (i, k)),
                pl.BlockSpec((bk, bn), lambda i, j, k: (k, j))]
    args = [a, w]
    if res is not None:
        in_specs.append(pl.BlockSpec((bm, bn), lambda i, j, k: (i, j)))
        args.append(res)
    return pl.pallas_call(
        functools.partial(_mm_kernel, nk=nk, act=act, has_res=res is not None),
        out_shape=jax.ShapeDtypeStruct((m, n), out_dtype),
        grid=(m // bm, n // bn, nk),
        in_specs=in_specs,
        out_specs=pl.BlockSpec((bm, bn), lambda i, j, k: (i, j)),
        compiler_params=_params("parallel", "parallel", "arbitrary"),
        name=name,
    )(*args)


def _inproj_kernel(h_ref, w_ref, cos_ref, sa_ref, sb_ref, o_ref):
    cos, sa, sb = cos_ref[...], sa_ref[...], sb_ref[...]
    for c in range(o_ref.shape[1] // MM_CHUNK):
        acc = jnp.dot(h_ref[...], w_ref[:, c * MM_CHUNK:(c + 1) * MM_CHUNK],
                      preferred_element_type=jnp.float32)
        for hd in range(MM_CHUNK // HEAD_DIM):
            a = acc[:, hd * HEAD_DIM:(hd + 1) * HEAD_DIM]
            r = (a * cos + pltpu.roll(a, HEAD_DIM - ROT_DIM // 2, 1) * sa
                 + pltpu.roll(a, ROT_DIM // 2, 1) * sb)
            col = c * MM_CHUNK + hd * HEAD_DIM
            o_ref[:, col:col + HEAD_DIM] = r.astype(o_ref.dtype)


def _rope_tables(seq_len):
    half = ROT_DIM // 2
    pos = jnp.arange(seq_len, dtype=jnp.float32)
    inv_freq = ROPE_THETA ** (-jnp.arange(0, ROT_DIM, 2, dtype=jnp.float32) / ROT_DIM)
    ang = pos[:, None] * inv_freq[None, :]
    cos, sin = jnp.cos(ang), jnp.sin(ang)
    z = lambda w: jnp.zeros((seq_len, w), jnp.float32)
    cos_t = jnp.concatenate([cos, cos, jnp.ones((seq_len, HEAD_DIM - ROT_DIM), jnp.float32)], axis=1)
    sin_a = jnp.concatenate([-sin, z(HEAD_DIM - half)], axis=1)
    sin_b = jnp.concatenate([z(half), sin, z(HEAD_DIM - ROT_DIM)], axis=1)
    return cos_t, sin_a, sin_b


def _inproj(h, w, seq_len, group_width, bm=1024, bn=1024):
    m, kdim = h.shape
    n = w.shape[1]
    bpg = group_width // bn
    cos_t, sin_a, sin_b = _rope_tables(seq_len)
    cos_t = jnp.stack([cos_t * QK_SCALE, cos_t, jnp.ones_like(cos_t)])
    sin_a = jnp.stack([sin_a * QK_SCALE, sin_a, jnp.zeros_like(sin_a)])
    sin_b = jnp.stack([sin_b * QK_SCALE, sin_b, jnp.zeros_like(sin_b)])
    tspec = pl.BlockSpec((None, bm, HEAD_DIM),
                         lambda i, j: ((j // bpg) % 3, i % (seq_len // bm), 0))
    return pl.pallas_call(
        _inproj_kernel,
        out_shape=jax.ShapeDtypeStruct((m, n), jnp.bfloat16),
        grid=(m // bm, n // bn),
        in_specs=[pl.BlockSpec((bm, kdim), lambda i, j: (i, 0)),
                  pl.BlockSpec((kdim, bn), lambda i, j: (0, j)),
                  tspec, tspec, tspec],
        out_specs=pl.BlockSpec((bm, bn), lambda i, j: (i, j)),
        compiler_params=_params("parallel", "parallel"),
        name="inproj_rope",
    )(h, w, cos_t, sin_a, sin_b)


def _scores(k, q):
    return lax.dot_general(k, q, NT_DIMS, preferred_element_type=jnp.float32)


def _diag_state(s, vt):
    key = lax.broadcasted_iota(jnp.int32, s.shape, 0)
    qry = lax.broadcasted_iota(jnp.int32, s.shape, 1)
    s = jnp.where(key <= qry, s, NEG_BIG)
    m = jnp.max(s, axis=0, keepdims=True)
    p = jnp.exp2(s - m)
    l = jnp.sum(p, axis=0, keepdims=True)
    acc = jnp.dot(vt, p.astype(vt.dtype), preferred_element_type=jnp.float32)
    return m, l, acc


def _update_state(pieces, biases, vts, state):
    m, l, acc = state
    m_new = m
    for s, b in zip(pieces, biases):
        m_new = jnp.maximum(m_new, jnp.max(s, axis=0, keepdims=True) + b)
    alpha = jnp.exp2(m - m_new)
    l = alpha * l
    acc = alpha * acc
    for s, b, vt in zip(pieces, biases, vts):
        p = jnp.exp2(s - (m_new - b))
        l = l + jnp.sum(p, axis=0, keepdims=True)
        acc = acc + jnp.dot(vt, p.astype(vt.dtype), preferred_element_type=jnp.float32)
    return m_new, l, acc


def _key_rows(k_ref, first_block, n_blocks, cols=slice(None)):
    rows = n_blocks * ATT_BLOCK
    return k_ref[pl.ds(pl.multiple_of(first_block * ATT_BLOCK, ATT_BLOCK), rows), cols]


def _moba_kernel(q_ref, k_ref, v_ref, g_ref, o_ref, kmh_ref, kml_ref, vt_ref, bias_ref, *,
                 n_blocks, n_heads):
    qi = pl.program_id(2)
    heads = [slice(h * HEAD_DIM, (h + 1) * HEAD_DIM) for h in range(n_heads)]

    @pl.when(qi == 0)
    def _():
        for h, cols in enumerate(heads):
            kf = k_ref[:, cols].astype(jnp.float32).reshape(n_blocks, MOBA_BLOCK, HEAD_DIM)
            km = jnp.sum(kf, axis=1) * (1.0 / MOBA_BLOCK)
            hi = km.astype(jnp.bfloat16)
            kmh_ref[h] = hi
            kml_ref[h] = (km - hi.astype(jnp.float32)).astype(jnp.bfloat16)
            for n in range(n_blocks):
                blk = v_ref[n * ATT_BLOCK:(n + 1) * ATT_BLOCK, cols].astype(jnp.float32)
                vt_ref[h * n_blocks + n] = blk.T.astype(vt_ref.dtype)

    qs = [q_ref[:, cols] for cols in heads]
    for h, q in enumerate(qs):
        gate = (lax.dot_general(kmh_ref[h], q, NT_DIMS, preferred_element_type=jnp.float32)
                + lax.dot_general(kml_ref[h], q, NT_DIMS, preferred_element_type=jnp.float32))
        blk = lax.broadcasted_iota(jnp.int32, gate.shape, 0)
        past = blk < qi
        g = jnp.where(past, gate, -jnp.inf)
        sel = jnp.zeros(gate.shape, jnp.float32)
        for _ in range(MOBA_TOPK):
            mx = jnp.max(g, axis=0, keepdims=True)
            first = jnp.min(jnp.where(g == mx, blk, n_blocks), axis=0, keepdims=True)
            hit = blk == first
            sel = jnp.where(hit, jnp.where(past, 1.0, 0.0), sel)
            g = jnp.where(hit, -jnp.inf, g)
        bias_ref[h] = jnp.where(sel > 0.0, 0.0, NEG_BIG)

    diag = [_scores(_key_rows(k_ref, qi, 1, cols), q) for q, cols in zip(qs, heads)]
    states = tuple(_diag_state(s, vt_ref[h * n_blocks + qi]) for h, s in enumerate(diag))

    def group(i, states):
        first = i * ATT_GROUP
        scores = [_scores(_key_rows(k_ref, first, ATT_GROUP, cols), q)
                  for q, cols in zip(qs, heads)]
        out = []
        for h, (s, st) in enumerate(zip(scores, states)):
            pieces = [s[t * ATT_BLOCK:(t + 1) * ATT_BLOCK] for t in range(ATT_GROUP)]
            biases = [bias_ref[h, pl.ds(first + t, 1), :] for t in range(ATT_GROUP)]
            vts = [vt_ref[h * n_blocks + first + t] for t in range(ATT_GROUP)]
            out.append(_update_state(pieces, biases, vts, st))
        return tuple(out)

    states = lax.fori_loop(0, lax.div(qi + (ATT_GROUP - 1), ATT_GROUP), group, states)
    for cols, (_, l, acc) in zip(heads, states):
        o_ref[:, cols] = _rms((acc / l).T, g_ref[:, cols]).astype(o_ref.dtype)


def _moba_attention(proj, out_g, n_heads, mix_width, heads_per_step=8):
    b, s, _ = proj.shape
    n_blocks = s // MOBA_BLOCK
    hp = heads_per_step
    w = hp * HEAD_DIM
    n_steps = n_heads // hp
    return pl.pallas_call(
        functools.partial(_moba_kernel, n_blocks=n_blocks, n_heads=hp),
        out_shape=jax.ShapeDtypeStruct((b, s, mix_width), jnp.bfloat16),
        grid=(b, n_steps, n_blocks),
        in_specs=[pl.BlockSpec((None, MOBA_BLOCK, w), lambda bi, h, qi: (bi, qi, h)),
                  pl.BlockSpec((None, s, w), lambda bi, h, qi: (bi, 0, n_steps + h)),
                  pl.BlockSpec((None, s, w), lambda bi, h, qi: (bi, 0, 2 * n_steps + h)),
                  pl.BlockSpec((None, 1, w), lambda bi, h, qi: (h, 0, 0))],
        out_specs=pl.BlockSpec((None, MOBA_BLOCK, w), lambda bi, h, qi: (bi, qi, h)),
        scratch_shapes=[pltpu.VMEM((hp, n_blocks, HEAD_DIM), jnp.bfloat16),
                        pltpu.VMEM((hp, n_blocks, HEAD_DIM), jnp.bfloat16),
                        pltpu.VMEM((hp * n_blocks, HEAD_DIM, MOBA_BLOCK), jnp.bfloat16),
                        pltpu.VMEM((hp, n_blocks, MOBA_BLOCK), jnp.float32)],
        compiler_params=_params("parallel", "parallel", "arbitrary"),
        name="moba_attention",
    )(proj, proj, proj, out_g.reshape(n_steps, 1, w))


def _diff_kernel(lq1_ref, lk1_ref, lq2_ref, lk2_ref, q_ref, k_ref, v_ref, g_ref, merged_ref,
                 o_ref, vt_ref, *, lambda_init, n_heads):
    del merged_ref
    qi = pl.program_id(2)
    n_blocks = vt_ref.shape[0] // n_heads
    vw = 2 * HEAD_DIM

    @pl.when(qi == 0)
    def _():
        for h in range(n_heads):
            for n in range(n_blocks):
                blk = v_ref[n * ATT_BLOCK:(n + 1) * ATT_BLOCK, h * vw:(h + 1) * vw]
                vt_ref[h * n_blocks + n] = blk.astype(jnp.float32).T.astype(vt_ref.dtype)

    subs = [slice(j * HEAD_DIM, (j + 1) * HEAD_DIM) for j in range(2 * n_heads)]
    qs = [q_ref[:, cols] for cols in subs]
    diag = [_scores(_key_rows(k_ref, qi, 1, cols), q) for q, cols in zip(qs, subs)]
    states = tuple(_diag_state(s, vt_ref[(j // 2) * n_blocks + qi]) for j, s in enumerate(diag))

    def group(i, states):
        first = i * ATT_GROUP
        scores = [_scores(_key_rows(k_ref, first, ATT_GROUP, cols), q) for q, cols in zip(qs, subs)]
        biases = [jnp.where(first + t < qi, 0.0, NEG_BIG) for t in range(ATT_GROUP)]
        out = []
        for j, (s, st) in enumerate(zip(scores, states)):
            vts = [vt_ref[(j // 2) * n_blocks + first + t] for t in range(ATT_GROUP)]
            pieces = [s[t * ATT_BLOCK:(t + 1) * ATT_BLOCK] for t in range(ATT_GROUP)]
            out.append(_update_state(pieces, biases, vts, st))
        return tuple(out)

    states = lax.fori_loop(0, lax.div(qi + (ATT_GROUP - 1), ATT_GROUP), group, states)
    outs = [acc / l for _, l, acc in states]
    lam = (jnp.exp(jnp.sum(lq1_ref[...] * lk1_ref[...], axis=1, keepdims=True))
           - jnp.exp(jnp.sum(lq2_ref[...] * lk2_ref[...], axis=1, keepdims=True))
           + lambda_init)
    for h in range(n_heads):
        o = (outs[2 * h] - lam * outs[2 * h + 1]).T
        o_ref[:, h * vw:(h + 1) * vw] = (_rms(o, g_ref[...]) * (1.0 - lambda_init)).astype(o_ref.dtype)


def _diff_attention(proj, merged, lq1, lk1, lq2, lk2, subln_g, n_heads, col0, out_col0,
                    lambda_init, heads_per_step=4):
    b, s, _ = proj.shape
    hp = heads_per_step
    w = hp * 2 * HEAD_DIM
    n_steps = n_heads // hp
    tq = ATT_BLOCK
    vec = lambda x: x.reshape(1, HEAD_DIM).astype(jnp.float32)
    vspec = pl.BlockSpec((1, HEAD_DIM), lambda bi, h, qi: (0, 0))
    qb, kb, vb, ob = col0 // w, (col0 + n_steps * w) // w, (col0 + 2 * n_steps * w) // w, out_col0 // w
    return pl.pallas_call(
        functools.partial(_diff_kernel, lambda_init=lambda_init, n_heads=hp),
        out_shape=jax.ShapeDtypeStruct(merged.shape, merged.dtype),
        grid=(b, n_steps, s // tq),
        in_specs=[vspec, vspec, vspec, vspec,
                  pl.BlockSpec((None, tq, w), lambda bi, h, qi: (bi, qi, qb + h)),
                  pl.BlockSpec((None, s, w), lambda bi, h, qi: (bi, 0, kb + h)),
                  pl.BlockSpec((None, s, w), lambda bi, h, qi: (bi, 0, vb + h)),
                  pl.BlockSpec((1, 2 * HEAD_DIM), lambda bi, h, qi: (0, 0)),
                  pl.BlockSpec(memory_space=pl.ANY)],
        out_specs=pl.BlockSpec((None, tq, w), lambda bi, h, qi: (bi, qi, ob + h)),
        scratch_shapes=[pltpu.VMEM((hp * (s // ATT_BLOCK), 2 * HEAD_DIM, ATT_BLOCK), jnp.bfloat16)],
        input_output_aliases={8: 0},
        compiler_params=_params("parallel", "parallel", "arbitrary"),
        name="diff_attention",
    )(vec(lq1), vec(lk1), vec(lq2), vec(lk2), proj, proj, proj,
      subln_g.reshape(1, 2 * HEAD_DIM), merged)


def _cross_kernel(x_ref, gc_ref, wq_ref, kv_ref, wo_ref, gm_ref, x2_ref, hm_ref):
    c = QK_SCALE
    x = x_ref[...]
    hc = _rms(x, gc_ref[...]).astype(jnp.bfloat16)
    q = jnp.dot(hc, wq_ref[...], preferred_element_type=jnp.float32).astype(jnp.bfloat16)
    kv_w = MEM_HEADS * HEAD_DIM
    outs = []
    for h in range(MEM_HEADS):
        k = kv_ref[:, h * HEAD_DIM:(h + 1) * HEAD_DIM]
        v = kv_ref[:, kv_w + h * HEAD_DIM:kv_w + (h + 1) * HEAD_DIM]
        s = lax.dot_general(q[:, h * HEAD_DIM:(h + 1) * HEAD_DIM], k, NT_DIMS,
                            preferred_element_type=jnp.float32)
        m = jnp.max(s, axis=1, keepdims=True)
        p = jnp.exp2((s - m) * c)
        l = jnp.sum(p, axis=1, keepdims=True)
        o = jnp.dot(p.astype(jnp.bfloat16), v, preferred_element_type=jnp.float32) / l
        outs.append(o.astype(jnp.bfloat16))
    o = jnp.concatenate(outs, axis=1)
    x2 = x + jnp.dot(o, wo_ref[...], preferred_element_type=jnp.float32)
    x2_ref[...] = x2
    hm_ref[...] = _rms(x2, gm_ref[...]).astype(hm_ref.dtype)


def _cross_sublayer(x, kv, g_cross, w_cq, w_co, g_mlp, seq_len, mem_len, tq=256):
    n, d = x.shape
    qw = w_cq.shape[1]
    const = lambda i: (0, 0)
    return pl.pallas_call(
        _cross_kernel,
        out_shape=(jax.ShapeDtypeStruct((n, d), jnp.float32),
                   jax.ShapeDtypeStruct((n, d), jnp.bfloat16)),
        grid=(n // tq,),
        in_specs=[pl.BlockSpec((tq, d), lambda i: (i, 0)),
                  pl.BlockSpec((1, d), const),
                  pl.BlockSpec((d, qw), const),
                  pl.BlockSpec((mem_len, 2 * qw), lambda i: (i // (seq_len // tq), 0)),
                  pl.BlockSpec((qw, d), const),
                  pl.BlockSpec((1, d), const)],
        out_specs=(pl.BlockSpec((tq, d), lambda i: (i, 0)),
                   pl.BlockSpec((tq, d), lambda i: (i, 0))),
        compiler_params=_params("parallel"),
        name="cross_sublayer",
    )(x, g_cross.reshape(1, d), w_cq, kv, w_co, g_mlp.reshape(1, d))


def kernel(x, mem, ln_mix_g, w_in, moba_out_g, lambda_q1, lambda_k1, lambda_q2, lambda_k2,
           diff_subln_g, w_out, ln_cross_g, ln_mem_g, w_cq, w_ckv, w_co, ln_mlp_g,
           w_up, w_down, final_g):
    b, s, d = x.shape
    mem_len = mem.shape[1]
    depth = w_in.shape[0]
    mix_width = w_out.shape[1]
    moba_width = mix_width // 2
    moba_heads = moba_width // HEAD_DIM
    diff_heads = (mix_width - moba_width) // (2 * HEAD_DIM)
    bf = jnp.bfloat16

    xf = x.reshape(b * s, d)
    memf = mem.reshape(b * mem_len, d)
    for l in range(depth):
        lambda_init = 0.8 - 0.6 * math.exp(-0.3 * l)

        h = _rmsnorm(xf, ln_mix_g[l], bf)
        proj = _inproj(h, w_in[l].astype(bf), s, moba_width).reshape(b, s, -1)
        merged = _moba_attention(proj, moba_out_g[l], moba_heads, mix_width)
        merged = _diff_attention(proj, merged, lambda_q1[l], lambda_k1[l], lambda_q2[l],
                                 lambda_k2[l], diff_subln_g[l], diff_heads,
                                 col0=3 * moba_width, out_col0=moba_width,
                                 lambda_init=lambda_init)
        x1 = _matmul(merged.reshape(b * s, mix_width), w_out[l].astype(bf), bm=1024, bn=512,
                     bk=mix_width, out_dtype=jnp.float32, res=xf, name="outproj")

        mn = _rmsnorm(memf, ln_mem_g[l], bf)
        kv = _matmul(mn, w_ckv[l].astype(bf), bm=b * mem_len, bn=512, bk=d, out_dtype=bf,
                     name="mem_kv")
        x2, hm = _cross_sublayer(x1, kv, ln_cross_g[l], w_cq[l].astype(bf), w_co[l].astype(bf),
                                 ln_mlp_g[l], s, mem_len)

        u = _matmul(hm, w_up[l].astype(bf), bm=1024, bn=1024, bk=d, out_dtype=bf, act="relu2",
                    name="mlp_up")
        xf = _matmul(u, w_down[l].astype(bf), bm=1024, bn=1024, bk=2048, out_dtype=jnp.float32,
                     res=x2, name="mlp_down")
    return _rmsnorm(xf, final_g, jnp.float32).reshape(b, s, d)
```

```python
import functools
import math

import jax
import jax.numpy as jnp
from jax import lax
from jax.experimental import pallas as pl
from jax.experimental.pallas import tpu as pltpu

HEAD_DIM = 128
MOBA_BLOCK = 256
MOBA_TOPK = 3
ROT_DIM = HEAD_DIM // 4
ROPE_THETA = 500000.0
MEM_HEADS = 4
EPS = 1e-5

LANES = 128
VMEM_LIMIT_BYTES = 56 * 2**20
NEG_BIG = -1e30
LOG2E = 1.4426950408889634
QK_SCALE = HEAD_DIM ** -0.5 * LOG2E
NT_DIMS = (((1,), (1,)), ((), ()))
ATT_BLOCK = 256
ATT_GROUP = 2
MM_CHUNK = 256


def _params(*sem):
    return pltpu.CompilerParams(dimension_semantics=sem, vmem_limit_bytes=VMEM_LIMIT_BYTES)


def _rms(x, g):
    return x * lax.rsqrt(jnp.mean(x * x, axis=-1, keepdims=True) + EPS) * g


def _rmsnorm_kernel(x_ref, g_ref, o_ref):
    o_ref[...] = _rms(x_ref[...], g_ref[...]).astype(o_ref.dtype)


def _rmsnorm(x, g, out_dtype, bm=256):
    n, d = x.shape
    return pl.pallas_call(
        _rmsnorm_kernel,
        out_shape=jax.ShapeDtypeStruct((n, d), out_dtype),
        grid=(n // bm,),
        in_specs=[pl.BlockSpec((bm, d), lambda i: (i, 0)),
                  pl.BlockSpec((1, d), lambda i: (0, 0))],
        out_specs=pl.BlockSpec((bm, d), lambda i: (i, 0)),
        compiler_params=_params("parallel"),
        name="rmsnorm",
    )(x, g.reshape(1, d))


def _weight_cols(w_ref, cols):
    return w_ref[:, cols].astype(jnp.bfloat16)


def _weight_spec(w, layer, bk, bn, index):
    if w.ndim == 2:
        return pl.BlockSpec((bk, bn), index)
    return pl.BlockSpec((None, bk, bn), lambda *g: (layer,) + tuple(index(*g)))


def _mm_kernel(*refs, nk, act, has_res):
    a_ref, w_ref = refs[0], refs[1]
    res_ref = refs[2] if has_res else None
    o_ref = refs[2 + has_res]
    if nk > 1:
        assert act is None and o_ref.dtype == jnp.float32

        @pl.when(pl.program_id(2) == 0)
        def _():
            o_ref[...] = res_ref[...] if has_res else jnp.zeros_like(o_ref)

    for c in range(o_ref.shape[1] // MM_CHUNK):
        cols = slice(c * MM_CHUNK, (c + 1) * MM_CHUNK)
        acc = jnp.dot(a_ref[...], _weight_cols(w_ref, cols), preferred_element_type=jnp.float32)
        if nk > 1:
            o_ref[:, cols] = acc + o_ref[:, cols]
        else:
            if act == "relu2":
                acc = jnp.square(jnp.maximum(acc, 0.0))
            if has_res:
                acc = acc + res_ref[:, cols]
            o_ref[:, cols] = acc.astype(o_ref.dtype)


def _matmul(a, w, *, bm, bn, bk, out_dtype, act=None, res=None, layer=0, name):
    m, kdim = a.shape
    n = w.shape[-1]
    nk = kdim // bk
    in_specs = [pl.BlockSpec((bm, bk), lambda i, j, k: (i, k)),
                _weight_spec(w, layer, bk, bn, lambda i, j, k: (k, j))]
    args = [a, w]
    if res is not None:
        in_specs.append(pl.BlockSpec((bm, bn), lambda i, j, k: (i, j)))
        args.append(res)
    return pl.pallas_call(
        functools.partial(_mm_kernel, nk=nk, act=act, has_res=res is not None),
        out_shape=jax.ShapeDtypeStruct((m, n), out_dtype),
        grid=(m // bm, n // bn, nk),
        in_specs=in_specs,
        out_specs=pl.BlockSpec((bm, bn), lambda i, j, k: (i, j)),
        compiler_params=_params("parallel", "parallel", "arbitrary"),
        name=name,
    )(*args)


def _inproj_kernel(h_ref, w_ref, cos_ref, sa_ref, sb_ref, o_ref):
    cos, sa, sb = cos_ref[...], sa_ref[...], sb_ref[...]
    for c in range(o_ref.shape[1] // MM_CHUNK):
        acc = jnp.dot(h_ref[...], _weight_cols(w_ref, slice(c * MM_CHUNK, (c + 1) * MM_CHUNK)),
                      preferred_element_type=jnp.float32)
        for hd in range(MM_CHUNK // HEAD_DIM):
            a = acc[:, hd * HEAD_DIM:(hd + 1) * HEAD_DIM]
            r = (a * cos + pltpu.roll(a, HEAD_DIM - ROT_DIM // 2, 1) * sa
                 + pltpu.roll(a, ROT_DIM // 2, 1) * sb)
            col = c * MM_CHUNK + hd * HEAD_DIM
            o_ref[:, col:col + HEAD_DIM] = r.astype(o_ref.dtype)


def _rope_tables(seq_len):
    half = ROT_DIM // 2
    pos = jnp.arange(seq_len, dtype=jnp.float32)
    inv_freq = ROPE_THETA ** (-jnp.arange(0, ROT_DIM, 2, dtype=jnp.float32) / ROT_DIM)
    ang = pos[:, None] * inv_freq[None, :]
    cos, sin = jnp.cos(ang), jnp.sin(ang)
    z = lambda w: jnp.zeros((seq_len, w), jnp.float32)
    cos_t = jnp.concatenate([cos, cos, jnp.ones((seq_len, HEAD_DIM - ROT_DIM), jnp.float32)], axis=1)
    sin_a = jnp.concatenate([-sin, z(HEAD_DIM - half)], axis=1)
    sin_b = jnp.concatenate([z(half), sin, z(HEAD_DIM - ROT_DIM)], axis=1)
    return cos_t, sin_a, sin_b


def _inproj(h, w, layer, seq_len, group_width, bm=1024, bn=1024):
    m, kdim = h.shape
    n = w.shape[-1]
    bpg = group_width // bn
    cos_t, sin_a, sin_b = _rope_tables(seq_len)
    cos_t = jnp.stack([cos_t * QK_SCALE, cos_t, jnp.ones_like(cos_t)])
    sin_a = jnp.stack([sin_a * QK_SCALE, sin_a, jnp.zeros_like(sin_a)])
    sin_b = jnp.stack([sin_b * QK_SCALE, sin_b, jnp.zeros_like(sin_b)])
    tspec = pl.BlockSpec((None, bm, HEAD_DIM),
                         lambda i, j: ((j // bpg) % 3, i % (seq_len // bm), 0))
    return pl.pallas_call(
        _inproj_kernel,
        out_shape=jax.ShapeDtypeStruct((m, n), jnp.bfloat16),
        grid=(m // bm, n // bn),
        in_specs=[pl.BlockSpec((bm, kdim), lambda i, j: (i, 0)),
                  _weight_spec(w, layer, kdim, bn, lambda i, j: (0, j)),
                  tspec, tspec, tspec],
        out_specs=pl.BlockSpec((bm, bn), lambda i, j: (i, j)),
        compiler_params=_params("parallel", "parallel"),
        name="inproj_rope",
    )(h, w, cos_t, sin_a, sin_b)


VT_PAD = 16


def _scores(k, q):
    return lax.dot_general(k, q, NT_DIMS, preferred_element_type=jnp.float32)


def _transposed_values(v_blk):
    vt = v_blk.astype(jnp.float32).T.astype(jnp.bfloat16)
    row = lax.broadcasted_iota(jnp.int32, (VT_PAD, v_blk.shape[0]), 0)
    return jnp.concatenate([vt, jnp.where(row == 0, 1.0, 0.0).astype(jnp.bfloat16)], axis=0)


def _weights(x):
    return jnp.exp2(x).astype(jnp.bfloat16)


def _diag_state(s, vt):
    key = lax.broadcasted_iota(jnp.int32, s.shape, 0)
    qry = lax.broadcasted_iota(jnp.int32, s.shape, 1)
    s = jnp.where(key <= qry, s, NEG_BIG)
    m = jnp.max(s, axis=0, keepdims=True)
    return m, jnp.dot(vt, _weights(s - m), preferred_element_type=jnp.float32)


def _normalised(acc):
    v_dim = acc.shape[0] - VT_PAD
    return acc[:v_dim] / acc[v_dim:v_dim + 1]


def _update_state(pieces, biases, vts, state):
    m, acc = state
    m_new = m
    for s, b in zip(pieces, biases):
        m_new = jnp.maximum(m_new, jnp.max(s, axis=0, keepdims=True) + b)
    acc = jnp.exp2(m - m_new) * acc
    for s, b, vt in zip(pieces, biases, vts):
        acc = acc + jnp.dot(vt, _weights(s - (m_new - b)), preferred_element_type=jnp.float32)
    return m_new, acc


def _key_rows(k_ref, first_block, n_blocks, cols=slice(None)):
    rows = n_blocks * ATT_BLOCK
    return k_ref[pl.ds(pl.multiple_of(first_block * ATT_BLOCK, ATT_BLOCK), rows), cols]


def _moba_kernel(q_ref, k_ref, v_ref, g_ref, o_ref, kmh_ref, kml_ref, vt_ref, bias_ref, *,
                 n_blocks, n_heads):
    qi = pl.program_id(2)
    heads = [slice(h * HEAD_DIM, (h + 1) * HEAD_DIM) for h in range(n_heads)]

    @pl.when(qi == 0)
    def _():
        for h, cols in enumerate(heads):
            kf = k_ref[:, cols].astype(jnp.float32).reshape(n_blocks, MOBA_BLOCK, HEAD_DIM)
            km = jnp.sum(kf, axis=1) * (1.0 / MOBA_BLOCK)
            hi = km.astype(jnp.bfloat16)
            kmh_ref[h] = hi
            kml_ref[h] = (km - hi.astype(jnp.float32)).astype(jnp.bfloat16)
            for n in range(n_blocks):
                vt_ref[h * n_blocks + n] = _transposed_values(
                    v_ref[n * ATT_BLOCK:(n + 1) * ATT_BLOCK, cols])

    qs = [q_ref[:, cols] for cols in heads]
    for h, q in enumerate(qs):
        gate = (lax.dot_general(kmh_ref[h], q, NT_DIMS, preferred_element_type=jnp.float32)
                + lax.dot_general(kml_ref[h], q, NT_DIMS, preferred_element_type=jnp.float32))
        blk = lax.broadcasted_iota(jnp.int32, gate.shape, 0)
        past = blk < qi
        g = jnp.where(past, gate, -jnp.inf)
        sel = jnp.zeros(gate.shape, jnp.float32)
        for _ in range(MOBA_TOPK):
            mx = jnp.max(g, axis=0, keepdims=True)
            first = jnp.min(jnp.where(g == mx, blk, n_blocks), axis=0, keepdims=True)
            hit = blk == first
            sel = jnp.where(hit, jnp.where(past, 1.0, 0.0), sel)
            g = jnp.where(hit, -jnp.inf, g)
        bias_ref[h] = jnp.where(sel > 0.0, 0.0, NEG_BIG)

    diag = [_scores(_key_rows(k_ref, qi, 1, cols), q) for q, cols in zip(qs, heads)]
    states = tuple(_diag_state(s, vt_ref[h * n_blocks + qi]) for h, s in enumerate(diag))

    def group(i, states):
        first = i * ATT_GROUP
        scores = [_scores(_key_rows(k_ref, first, ATT_GROUP, cols), q)
                  for q, cols in zip(qs, heads)]
        out = []
        for h, (s, st) in enumerate(zip(scores, states)):
            pieces = [s[t * ATT_BLOCK:(t + 1) * ATT_BLOCK] for t in range(ATT_GROUP)]
            biases = [bias_ref[h, pl.ds(first + t, 1), :] for t in range(ATT_GROUP)]
            vts = [vt_ref[h * n_blocks + first + t] for t in range(ATT_GROUP)]
            out.append(_update_state(pieces, biases, vts, st))
        return tuple(out)

    states = lax.fori_loop(0, lax.div(qi + (ATT_GROUP - 1), ATT_GROUP), group, states)
    for cols, (_, acc) in zip(heads, states):
        o_ref[:, cols] = _rms(_normalised(acc).T, g_ref[:, cols]).astype(o_ref.dtype)


def _moba_attention(proj, out_g, n_heads, mix_width, heads_per_step=8):
    b, s, _ = proj.shape
    n_blocks = s // MOBA_BLOCK
    hp = heads_per_step
    w = hp * HEAD_DIM
    n_steps = n_heads // hp
    return pl.pallas_call(
        functools.partial(_moba_kernel, n_blocks=n_blocks, n_heads=hp),
        out_shape=jax.ShapeDtypeStruct((b, s, mix_width), jnp.bfloat16),
        grid=(b, n_steps, n_blocks),
        in_specs=[pl.BlockSpec((None, MOBA_BLOCK, w), lambda bi, h, qi: (bi, qi, h)),
                  pl.BlockSpec((None, s, w), lambda bi, h, qi: (bi, 0, n_steps + h)),
                  pl.BlockSpec((None, s, w), lambda bi, h, qi: (bi, 0, 2 * n_steps + h)),
                  pl.BlockSpec((None, 1, w), lambda bi, h, qi: (h, 0, 0))],
        out_specs=pl.BlockSpec((None, MOBA_BLOCK, w), lambda bi, h, qi: (bi, qi, h)),
        scratch_shapes=[pltpu.VMEM((hp, n_blocks, HEAD_DIM), jnp.bfloat16),
                        pltpu.VMEM((hp, n_blocks, HEAD_DIM), jnp.bfloat16),
                        pltpu.VMEM((hp * n_blocks, HEAD_DIM + VT_PAD, MOBA_BLOCK), jnp.bfloat16),
                        pltpu.VMEM((hp, n_blocks, MOBA_BLOCK), jnp.float32)],
        compiler_params=_params("parallel", "parallel", "arbitrary"),
        name="moba_attention",
    )(proj, proj, proj, out_g.reshape(n_steps, 1, w))


def _diff_kernel(lq1_ref, lk1_ref, lq2_ref, lk2_ref, q_ref, k_ref, v_ref, g_ref, merged_ref,
                 o_ref, vt_ref, *, lambda_init, n_heads):
    del merged_ref
    qi = pl.program_id(2)
    n_blocks = vt_ref.shape[0] // n_heads
    vw = 2 * HEAD_DIM

    @pl.when(qi == 0)
    def _():
        for h in range(n_heads):
            for n in range(n_blocks):
                vt_ref[h * n_blocks + n] = _transposed_values(
                    v_ref[n * ATT_BLOCK:(n + 1) * ATT_BLOCK, h * vw:(h + 1) * vw])

    subs = [slice(j * HEAD_DIM, (j + 1) * HEAD_DIM) for j in range(2 * n_heads)]
    qs = [q_ref[:, cols] for cols in subs]
    diag = [_scores(_key_rows(k_ref, qi, 1, cols), q) for q, cols in zip(qs, subs)]
    states = tuple(_diag_state(s, vt_ref[(j // 2) * n_blocks + qi]) for j, s in enumerate(diag))

    def group(i, states):
        first = i * ATT_GROUP
        scores = [_scores(_key_rows(k_ref, first, ATT_GROUP, cols), q) for q, cols in zip(qs, subs)]
        biases = [jnp.where(first + t < qi, 0.0, NEG_BIG) for t in range(ATT_GROUP)]
        out = []
        for j, (s, st) in enumerate(zip(scores, states)):
            vts = [vt_ref[(j // 2) * n_blocks + first + t] for t in range(ATT_GROUP)]
            pieces = [s[t * ATT_BLOCK:(t + 1) * ATT_BLOCK] for t in range(ATT_GROUP)]
            out.append(_update_state(pieces, biases, vts, st))
        return tuple(out)

    states = lax.fori_loop(0, lax.div(qi + (ATT_GROUP - 1), ATT_GROUP), group, states)
    outs = [_normalised(acc) for _, acc in states]
    lam = (jnp.exp(jnp.sum(lq1_ref[...] * lk1_ref[...], axis=1, keepdims=True))
           - jnp.exp(jnp.sum(lq2_ref[...] * lk2_ref[...], axis=1, keepdims=True))
           + lambda_init)
    for h in range(n_heads):
        o = (outs[2 * h] - lam * outs[2 * h + 1]).T
        o_ref[:, h * vw:(h + 1) * vw] = (_rms(o, g_ref[...]) * (1.0 - lambda_init)).astype(o_ref.dtype)


def _diff_attention(proj, merged, lq1, lk1, lq2, lk2, subln_g, n_heads, col0, out_col0,
                    lambda_init, heads_per_step=4):
    b, s, _ = proj.shape
    hp = heads_per_step
    w = hp * 2 * HEAD_DIM
    n_steps = n_heads // hp
    tq = ATT_BLOCK
    vec = lambda x: x.reshape(1, HEAD_DIM).astype(jnp.float32)
    vspec = pl.BlockSpec((1, HEAD_DIM), lambda bi, h, qi: (0, 0))
    qb, kb, vb, ob = col0 // w, (col0 + n_steps * w) // w, (col0 + 2 * n_steps * w) // w, out_col0 // w
    return pl.pallas_call(
        functools.partial(_diff_kernel, lambda_init=lambda_init, n_heads=hp),
        out_shape=jax.ShapeDtypeStruct(merged.shape, merged.dtype),
        grid=(b, n_steps, s // tq),
        in_specs=[vspec, vspec, vspec, vspec,
                  pl.BlockSpec((None, tq, w), lambda bi, h, qi: (bi, qi, qb + h)),
                  pl.BlockSpec((None, s, w), lambda bi, h, qi: (bi, 0, kb + h)),
                  pl.BlockSpec((None, s, w), lambda bi, h, qi: (bi, 0, vb + h)),
                  pl.BlockSpec((1, 2 * HEAD_DIM), lambda bi, h, qi: (0, 0)),
                  pl.BlockSpec(memory_space=pl.ANY)],
        out_specs=pl.BlockSpec((None, tq, w), lambda bi, h, qi: (bi, qi, ob + h)),
        scratch_shapes=[pltpu.VMEM((hp * (s // ATT_BLOCK), 2 * HEAD_DIM + VT_PAD, ATT_BLOCK),
                                   jnp.bfloat16)],
        input_output_aliases={8: 0},
        compiler_params=_params("parallel", "parallel", "arbitrary"),
        name="diff_attention",
    )(vec(lq1), vec(lk1), vec(lq2), vec(lk2), proj, proj, proj,
      subln_g.reshape(1, 2 * HEAD_DIM), merged)


def _cross_kernel(x_ref, gc_ref, wq_ref, kv_ref, wo_ref, gm_ref, x2_ref, hm_ref):
    c = QK_SCALE
    x = x_ref[...]
    hc = _rms(x, gc_ref[...]).astype(jnp.bfloat16)
    q = jnp.dot(hc, wq_ref[...], preferred_element_type=jnp.float32).astype(jnp.bfloat16)
    kv_w = MEM_HEADS * HEAD_DIM
    outs = []
    for h in range(MEM_HEADS):
        k = kv_ref[:, h * HEAD_DIM:(h + 1) * HEAD_DIM]
        v = kv_ref[:, kv_w + h * HEAD_DIM:kv_w + (h + 1) * HEAD_DIM]
        s = lax.dot_general(q[:, h * HEAD_DIM:(h + 1) * HEAD_DIM], k, NT_DIMS,
                            preferred_element_type=jnp.float32)
        m = jnp.max(s, axis=1, keepdims=True)
        p = jnp.exp2((s - m) * c)
        l = jnp.sum(p, axis=1, keepdims=True)
        o = jnp.dot(p.astype(jnp.bfloat16), v, preferred_element_type=jnp.float32) / l
        outs.append(o.astype(jnp.bfloat16))
    o = jnp.concatenate(outs, axis=1)
    x2 = x + jnp.dot(o, wo_ref[...], preferred_element_type=jnp.float32)
    x2_ref[...] = x2
    hm_ref[...] = _rms(x2, gm_ref[...]).astype(hm_ref.dtype)


def _cross_sublayer(x, kv, g_cross, w_cq, w_co, g_mlp, seq_len, mem_len, tq=256):
    n, d = x.shape
    qw = w_cq.shape[1]
    const = lambda i: (0, 0)
    return pl.pallas_call(
        _cross_kernel,
        out_shape=(jax.ShapeDtypeStruct((n, d), jnp.float32),
                   jax.ShapeDtypeStruct((n, d), jnp.bfloat16)),
        grid=(n // tq,),
        in_specs=[pl.BlockSpec((tq, d), lambda i: (i, 0)),
                  pl.BlockSpec((1, d), const),
                  pl.BlockSpec((d, qw), const),
                  pl.BlockSpec((mem_len, 2 * qw), lambda i: (i // (seq_len // tq), 0)),
                  pl.BlockSpec((qw, d), const),
                  pl.BlockSpec((1, d), const)],
        out_specs=(pl.BlockSpec((tq, d), lambda i: (i, 0)),
                   pl.BlockSpec((tq, d), lambda i: (i, 0))),
        compiler_params=_params("parallel"),
        name="cross_sublayer",
    )(x, g_cross.reshape(1, d), w_cq, kv, w_co, g_mlp.reshape(1, d))


def kernel(x, mem, ln_mix_g, w_in, moba_out_g, lambda_q1, lambda_k1, lambda_q2, lambda_k2,
           diff_subln_g, w_out, ln_cross_g, ln_mem_g, w_cq, w_ckv, w_co, ln_mlp_g,
           w_up, w_down, final_g):
    b, s, d = x.shape
    mem_len = mem.shape[1]
    depth = w_in.shape[0]
    mix_width = w_out.shape[1]
    moba_width = mix_width // 2
    moba_heads = moba_width // HEAD_DIM
    diff_heads = (mix_width - moba_width) // (2 * HEAD_DIM)
    bf = jnp.bfloat16

    xf = x.reshape(b * s, d)
    memf = mem.reshape(b * mem_len, d)
    for l in range(depth):
        lambda_init = 0.8 - 0.6 * math.exp(-0.3 * l)

        h = _rmsnorm(xf, ln_mix_g[l], bf)
        proj = _inproj(h, w_in[l].astype(bf), 0, s, moba_width).reshape(b, s, -1)
        merged = _moba_attention(proj, moba_out_g[l], moba_heads, mix_width)
        merged = _diff_attention(proj, merged, lambda_q1[l], lambda_k1[l], lambda_q2[l],
                                 lambda_k2[l], diff_subln_g[l], diff_heads,
                                 col0=3 * moba_width, out_col0=moba_width,
                                 lambda_init=lambda_init)
        x1 = _matmul(merged.reshape(b * s, mix_width), w_out, layer=l, bm=1024, bn=512,
                     bk=mix_width, out_dtype=jnp.float32, res=xf, name="outproj")

        mn = _rmsnorm(memf, ln_mem_g[l], bf)
        kv = _matmul(mn, w_ckv, layer=l, bm=b * mem_len, bn=512, bk=d, out_dtype=bf,
                     name="mem_kv")
        x2, hm = _cross_sublayer(x1, kv, ln_cross_g[l], w_cq[l].astype(bf), w_co[l].astype(bf),
                                 ln_mlp_g[l], s, mem_len)

        u = _matmul(hm, w_up, layer=l, bm=1024, bn=512, bk=d, out_dtype=bf, act="relu2",
                    name="mlp_up")
        xf = _matmul(u, w_down[l].astype(bf), bm=1024, bn=1024, bk=2048, out_dtype=jnp.float32,
                     res=x2, name="mlp_down")
    return _rmsnorm(xf, final_g, jnp.float32).reshape(b, s, d)
```

```python
import functools
import math

import jax
import jax.numpy as jnp
from jax import lax
from jax.experimental import pallas as pl
from jax.experimental.pallas import tpu as pltpu

HEAD_DIM = 128
MOBA_BLOCK = 256
MOBA_TOPK = 3
ROT_DIM = HEAD_DIM // 4
ROPE_THETA = 500000.0
MEM_HEADS = 4
EPS = 1e-5

LANES = 128
VMEM_LIMIT_BYTES = 56 * 2**20
NEG_BIG = -1e30
LOG2E = 1.4426950408889634
QK_SCALE = HEAD_DIM ** -0.5 * LOG2E
NT_DIMS = (((1,), (1,)), ((), ()))
ATT_BLOCK = 256
ATT_GROUP = 2
MM_CHUNK = 256


def _params(*sem):
    return pltpu.CompilerParams(dimension_semantics=sem, vmem_limit_bytes=VMEM_LIMIT_BYTES)


def _rms(x, g):
    return x * lax.rsqrt(jnp.mean(x * x, axis=-1, keepdims=True) + EPS) * g


def _rmsnorm_kernel(x_ref, g_ref, o_ref):
    o_ref[...] = _rms(x_ref[...], g_ref[...]).astype(o_ref.dtype)


def _rmsnorm(x, g, out_dtype, bm=256):
    n, d = x.shape
    return pl.pallas_call(
        _rmsnorm_kernel,
        out_shape=jax.ShapeDtypeStruct((n, d), out_dtype),
        grid=(n // bm,),
        in_specs=[pl.BlockSpec((bm, d), lambda i: (i, 0)),
                  pl.BlockSpec((1, d), lambda i: (0, 0))],
        out_specs=pl.BlockSpec((bm, d), lambda i: (i, 0)),
        compiler_params=_params("parallel"),
        name="rmsnorm",
    )(x, g.reshape(1, d))


def _weight_cols(w_ref, cols):
    return w_ref[:, cols].astype(jnp.bfloat16)


def _weight_spec(w, layer, bk, bn, index):
    if w.ndim == 2:
        return pl.BlockSpec((bk, bn), index)
    return pl.BlockSpec((None, bk, bn), lambda *g: (layer,) + tuple(index(*g)))


def _mm_kernel(*refs, nk, act, has_res):
    a_ref, w_ref = refs[0], refs[1]
    res_ref = refs[2] if has_res else None
    o_ref = refs[2 + has_res]
    if nk > 1:
        assert act is None and o_ref.dtype == jnp.float32

        @pl.when(pl.program_id(2) == 0)
        def _():
            o_ref[...] = res_ref[...] if has_res else jnp.zeros_like(o_ref)

    for c in range(o_ref.shape[1] // MM_CHUNK):
        cols = slice(c * MM_CHUNK, (c + 1) * MM_CHUNK)
        acc = jnp.dot(a_ref[...], _weight_cols(w_ref, cols), preferred_element_type=jnp.float32)
        if nk > 1:
            o_ref[:, cols] = acc + o_ref[:, cols]
        else:
            if act == "relu2":
                acc = jnp.square(jnp.maximum(acc, 0.0))
            if has_res:
                acc = acc + res_ref[:, cols]
            o_ref[:, cols] = acc.astype(o_ref.dtype)


def _matmul(a, w, *, bm, bn, bk, out_dtype, act=None, res=None, layer=0, n_out=None,
            w_col_block=lambda j: j, name):
    m, kdim = a.shape
    n = w.shape[-1] if n_out is None else n_out
    nk = kdim // bk
    in_specs = [pl.BlockSpec((bm, bk), lambda i, j, k: (i, k)),
                _weight_spec(w, layer, bk, bn, lambda i, j, k: (k, w_col_block(j)))]
    args = [a, w]
    if res is not None:
        in_specs.append(pl.BlockSpec((bm, bn), lambda i, j, k: (i, j)))
        args.append(res)
    return pl.pallas_call(
        functools.partial(_mm_kernel, nk=nk, act=act, has_res=res is not None),
        out_shape=jax.ShapeDtypeStruct((m, n), out_dtype),
        grid=(m // bm, n // bn, nk),
        in_specs=in_specs,
        out_specs=pl.BlockSpec((bm, bn), lambda i, j, k: (i, j)),
        compiler_params=_params("parallel", "parallel", "arbitrary"),
        name=name,
    )(*args)


def _inproj_kernel(h_ref, w_ref, cos_ref, sa_ref, sb_ref, o_ref):
    cos, sa, sb = cos_ref[...], sa_ref[...], sb_ref[...]
    for c in range(o_ref.shape[1] // MM_CHUNK):
        acc = jnp.dot(h_ref[...], _weight_cols(w_ref, slice(c * MM_CHUNK, (c + 1) * MM_CHUNK)),
                      preferred_element_type=jnp.float32)
        for hd in range(MM_CHUNK // HEAD_DIM):
            a = acc[:, hd * HEAD_DIM:(hd + 1) * HEAD_DIM]
            r = (a * cos + pltpu.roll(a, HEAD_DIM - ROT_DIM // 2, 1) * sa
                 + pltpu.roll(a, ROT_DIM // 2, 1) * sb)
            col = c * MM_CHUNK + hd * HEAD_DIM
            o_ref[:, col:col + HEAD_DIM] = r.astype(o_ref.dtype)


def _rope_tables(seq_len):
    half = ROT_DIM // 2
    pos = jnp.arange(seq_len, dtype=jnp.float32)
    inv_freq = ROPE_THETA ** (-jnp.arange(0, ROT_DIM, 2, dtype=jnp.float32) / ROT_DIM)
    ang = pos[:, None] * inv_freq[None, :]
    cos, sin = jnp.cos(ang), jnp.sin(ang)
    z = lambda w: jnp.zeros((seq_len, w), jnp.float32)
    cos_t = jnp.concatenate([cos, cos, jnp.ones((seq_len, HEAD_DIM - ROT_DIM), jnp.float32)], axis=1)
    sin_a = jnp.concatenate([-sin, z(HEAD_DIM - half)], axis=1)
    sin_b = jnp.concatenate([z(half), sin, z(HEAD_DIM - ROT_DIM)], axis=1)
    return cos_t, sin_a, sin_b


def _inproj_qk(h, w, seq_len, group_width, bm=1024, bn=1024):
    m, kdim = h.shape
    bpg = group_width // bn
    cos_t, sin_a, sin_b = (jnp.stack([t * QK_SCALE, t]) for t in _rope_tables(seq_len))
    tspec = pl.BlockSpec((None, bm, HEAD_DIM),
                         lambda i, j: ((j // bpg) % 2, i % (seq_len // bm), 0))

    def w_block(i, j):
        out_group = j // bpg
        return 0, (out_group + out_group // 2) * bpg + j % bpg

    return pl.pallas_call(
        _inproj_kernel,
        out_shape=jax.ShapeDtypeStruct((m, 4 * group_width), jnp.bfloat16),
        grid=(m // bm, 4 * bpg),
        in_specs=[pl.BlockSpec((bm, kdim), lambda i, j: (i, 0)),
                  pl.BlockSpec((kdim, bn), w_block),
                  tspec, tspec, tspec],
        out_specs=pl.BlockSpec((bm, bn), lambda i, j: (i, j)),
        compiler_params=_params("parallel", "parallel"),
        name="inproj_qk_rope",
    )(h, w, cos_t, sin_a, sin_b)


VT_PAD = 16


def _scores(k, q):
    return lax.dot_general(k, q, NT_DIMS, preferred_element_type=jnp.float32)


def _transposed_values(v_blk):
    vt = v_blk.astype(jnp.float32).T.astype(jnp.bfloat16)
    row = lax.broadcasted_iota(jnp.int32, (VT_PAD, v_blk.shape[0]), 0)
    return jnp.concatenate([vt, jnp.where(row == 0, 1.0, 0.0).astype(jnp.bfloat16)], axis=0)


def _weights(x):
    return jnp.exp2(x).astype(jnp.bfloat16)


def _diag_state(s, vt):
    key = lax.broadcasted_iota(jnp.int32, s.shape, 0)
    qry = lax.broadcasted_iota(jnp.int32, s.shape, 1)
    s = jnp.where(key <= qry, s, NEG_BIG)
    m = jnp.max(s, axis=0, keepdims=True)
    return m, jnp.dot(vt, _weights(s - m), preferred_element_type=jnp.float32)


def _normalised(acc):
    v_dim = acc.shape[0] - VT_PAD
    return acc[:v_dim] / acc[v_dim:v_dim + 1]


def _update_state(pieces, biases, vts, state):
    m, acc = state
    m_new = m
    for s, b in zip(pieces, biases):
        m_new = jnp.maximum(m_new, jnp.max(s, axis=0, keepdims=True) + b)
    acc = jnp.exp2(m - m_new) * acc
    for s, b, vt in zip(pieces, biases, vts):
        acc = acc + jnp.dot(vt, _weights(s - (m_new - b)), preferred_element_type=jnp.float32)
    return m_new, acc


def _key_rows(k_ref, first_block, n_blocks, cols=slice(None)):
    rows = n_blocks * ATT_BLOCK
    return k_ref[pl.ds(pl.multiple_of(first_block * ATT_BLOCK, ATT_BLOCK), rows), cols]


def _moba_kernel(q_ref, k_ref, v_ref, g_ref, o_ref, kmh_ref, kml_ref, vt_ref, bias_ref, *,
                 n_blocks, n_heads):
    qi = pl.program_id(2)
    heads = [slice(h * HEAD_DIM, (h + 1) * HEAD_DIM) for h in range(n_heads)]

    @pl.when(qi == 0)
    def _():
        for h, cols in enumerate(heads):
            kf = k_ref[:, cols].astype(jnp.float32).reshape(n_blocks, MOBA_BLOCK, HEAD_DIM)
            km = jnp.sum(kf, axis=1) * (1.0 / MOBA_BLOCK)
            hi = km.astype(jnp.bfloat16)
            kmh_ref[h] = hi
            kml_ref[h] = (km - hi.astype(jnp.float32)).astype(jnp.bfloat16)
            for n in range(n_blocks):
                vt_ref[h * n_blocks + n] = _transposed_values(
                    v_ref[n * ATT_BLOCK:(n + 1) * ATT_BLOCK, cols])

    qs = [q_ref[:, cols] for cols in heads]
    for h, q in enumerate(qs):
        gate = (lax.dot_general(kmh_ref[h], q, NT_DIMS, preferred_element_type=jnp.float32)
                + lax.dot_general(kml_ref[h], q, NT_DIMS, preferred_element_type=jnp.float32))
        blk = lax.broadcasted_iota(jnp.int32, gate.shape, 0)
        past = blk < qi
        g = jnp.where(past, gate, -jnp.inf)
        sel = jnp.zeros(gate.shape, jnp.float32)
        for _ in range(MOBA_TOPK):
            mx = jnp.max(g, axis=0, keepdims=True)
            first = jnp.min(jnp.where(g == mx, blk, n_blocks), axis=0, keepdims=True)
            hit = blk == first
            sel = jnp.where(hit, jnp.where(past, 1.0, 0.0), sel)
            g = jnp.where(hit, -jnp.inf, g)
        bias_ref[h] = jnp.where(sel > 0.0, 0.0, NEG_BIG)

    diag = [_scores(_key_rows(k_ref, qi, 1, cols), q) for q, cols in zip(qs, heads)]
    states = tuple(_diag_state(s, vt_ref[h * n_blocks + qi]) for h, s in enumerate(diag))

    def group(i, states):
        first = i * ATT_GROUP
        scores = [_scores(_key_rows(k_ref, first, ATT_GROUP, cols), q)
                  for q, cols in zip(qs, heads)]
        out = []
        for h, (s, st) in enumerate(zip(scores, states)):
            pieces = [s[t * ATT_BLOCK:(t + 1) * ATT_BLOCK] for t in range(ATT_GROUP)]
            biases = [bias_ref[h, pl.ds(first + t, 1), :] for t in range(ATT_GROUP)]
            vts = [vt_ref[h * n_blocks + first + t] for t in range(ATT_GROUP)]
            out.append(_update_state(pieces, biases, vts, st))
        return tuple(out)

    states = lax.fori_loop(0, lax.div(qi + (ATT_GROUP - 1), ATT_GROUP), group, states)
    for cols, (_, acc) in zip(heads, states):
        o_ref[:, cols] = _rms(_normalised(acc).T, g_ref[:, cols]).astype(o_ref.dtype)


def _moba_attention(qk, vproj, out_g, n_heads, mix_width, heads_per_step=8):
    b, s, _ = qk.shape
    n_blocks = s // MOBA_BLOCK
    hp = heads_per_step
    w = hp * HEAD_DIM
    n_steps = n_heads // hp
    return pl.pallas_call(
        functools.partial(_moba_kernel, n_blocks=n_blocks, n_heads=hp),
        out_shape=jax.ShapeDtypeStruct((b, s, mix_width), jnp.bfloat16),
        grid=(b, n_steps, n_blocks),
        in_specs=[pl.BlockSpec((None, MOBA_BLOCK, w), lambda bi, h, qi: (bi, qi, h)),
                  pl.BlockSpec((None, s, w), lambda bi, h, qi: (bi, 0, n_steps + h)),
                  pl.BlockSpec((None, s, w), lambda bi, h, qi: (bi, 0, h)),
                  pl.BlockSpec((None, 1, w), lambda bi, h, qi: (h, 0, 0))],
        out_specs=pl.BlockSpec((None, MOBA_BLOCK, w), lambda bi, h, qi: (bi, qi, h)),
        scratch_shapes=[pltpu.VMEM((hp, n_blocks, HEAD_DIM), jnp.bfloat16),
                        pltpu.VMEM((hp, n_blocks, HEAD_DIM), jnp.bfloat16),
                        pltpu.VMEM((hp * n_blocks, HEAD_DIM + VT_PAD, MOBA_BLOCK), jnp.bfloat16),
                        pltpu.VMEM((hp, n_blocks, MOBA_BLOCK), jnp.float32)],
        compiler_params=_params("parallel", "parallel", "arbitrary"),
        name="moba_attention",
    )(qk, qk, vproj, out_g.reshape(n_steps, 1, w))


def _diff_kernel(lq1_ref, lk1_ref, lq2_ref, lk2_ref, q_ref, k_ref, v_ref, g_ref, merged_ref,
                 o_ref, vt_ref, *, lambda_init, n_heads):
    del merged_ref
    qi = pl.program_id(2)
    n_blocks = vt_ref.shape[0] // n_heads
    vw = 2 * HEAD_DIM

    @pl.when(qi == 0)
    def _():
        for h in range(n_heads):
            for n in range(n_blocks):
                vt_ref[h * n_blocks + n] = _transposed_values(
                    v_ref[n * ATT_BLOCK:(n + 1) * ATT_BLOCK, h * vw:(h + 1) * vw])

    subs = [slice(j * HEAD_DIM, (j + 1) * HEAD_DIM) for j in range(2 * n_heads)]
    qs = [q_ref[:, cols] for cols in subs]
    diag = [_scores(_key_rows(k_ref, qi, 1, cols), q) for q, cols in zip(qs, subs)]
    states = tuple(_diag_state(s, vt_ref[(j // 2) * n_blocks + qi]) for j, s in enumerate(diag))

    def group(i, states):
        first = i * ATT_GROUP
        scores = [_scores(_key_rows(k_ref, first, ATT_GROUP, cols), q) for q, cols in zip(qs, subs)]
        biases = [jnp.where(first + t < qi, 0.0, NEG_BIG) for t in range(ATT_GROUP)]
        out = []
        for j, (s, st) in enumerate(zip(scores, states)):
            vts = [vt_ref[(j // 2) * n_blocks + first + t] for t in range(ATT_GROUP)]
            pieces = [s[t * ATT_BLOCK:(t + 1) * ATT_BLOCK] for t in range(ATT_GROUP)]
            out.append(_update_state(pieces, biases, vts, st))
        return tuple(out)

    states = lax.fori_loop(0, lax.div(qi + (ATT_GROUP - 1), ATT_GROUP), group, states)
    outs = [_normalised(acc) for _, acc in states]
    lam = (jnp.exp(jnp.sum(lq1_ref[...] * lk1_ref[...], axis=1, keepdims=True))
           - jnp.exp(jnp.sum(lq2_ref[...] * lk2_ref[...], axis=1, keepdims=True))
           + lambda_init)
    for h in range(n_heads):
        o = (outs[2 * h] - lam * outs[2 * h + 1]).T
        o_ref[:, h * vw:(h + 1) * vw] = (_rms(o, g_ref[...]) * (1.0 - lambda_init)).astype(o_ref.dtype)


def _diff_attention(qk, vproj, merged, lq1, lk1, lq2, lk2, subln_g, n_heads, lambda_init,
                    heads_per_step=4):
    b, s, qk_width = qk.shape
    v_col0 = out_col0 = vproj.shape[2] // 2
    col0 = qk_width // 2
    hp = heads_per_step
    w = hp * 2 * HEAD_DIM
    n_steps = n_heads // hp
    tq = ATT_BLOCK
    vec = lambda x: x.reshape(1, HEAD_DIM).astype(jnp.float32)
    vspec = pl.BlockSpec((1, HEAD_DIM), lambda bi, h, qi: (0, 0))
    qb, kb, vb, ob = col0 // w, col0 // w + n_steps, v_col0 // w, out_col0 // w
    return pl.pallas_call(
        functools.partial(_diff_kernel, lambda_init=lambda_init, n_heads=hp),
        out_shape=jax.ShapeDtypeStruct(merged.shape, merged.dtype),
        grid=(b, n_steps, s // tq),
        in_specs=[vspec, vspec, vspec, vspec,
                  pl.BlockSpec((None, tq, w), lambda bi, h, qi: (bi, qi, qb + h)),
                  pl.BlockSpec((None, s, w), lambda bi, h, qi: (bi, 0, kb + h)),
                  pl.BlockSpec((None, s, w), lambda bi, h, qi: (bi, 0, vb + h)),
                  pl.BlockSpec((1, 2 * HEAD_DIM), lambda bi, h, qi: (0, 0)),
                  pl.BlockSpec(memory_space=pl.ANY)],
        out_specs=pl.BlockSpec((None, tq, w), lambda bi, h, qi: (bi, qi, ob + h)),
        scratch_shapes=[pltpu.VMEM((hp * (s // ATT_BLOCK), 2 * HEAD_DIM + VT_PAD, ATT_BLOCK),
                                   jnp.bfloat16)],
        input_output_aliases={8: 0},
        compiler_params=_params("parallel", "parallel", "arbitrary"),
        name="diff_attention",
    )(vec(lq1), vec(lk1), vec(lq2), vec(lk2), qk, qk, vproj,
      subln_g.reshape(1, 2 * HEAD_DIM), merged)


def _cross_kernel(x_ref, gc_ref, wq_ref, kv_ref, wo_ref, gm_ref, x2_ref, hm_ref):
    c = QK_SCALE
    x = x_ref[...]
    hc = _rms(x, gc_ref[...]).astype(jnp.bfloat16)
    q = jnp.dot(hc, wq_ref[...], preferred_element_type=jnp.float32).astype(jnp.bfloat16)
    kv_w = MEM_HEADS * HEAD_DIM
    outs = []
    for h in range(MEM_HEADS):
        k = kv_ref[:, h * HEAD_DIM:(h + 1) * HEAD_DIM]
        v = kv_ref[:, kv_w + h * HEAD_DIM:kv_w + (h + 1) * HEAD_DIM]
        s = lax.dot_general(q[:, h * HEAD_DIM:(h + 1) * HEAD_DIM], k, NT_DIMS,
                            preferred_element_type=jnp.float32)
        m = jnp.max(s, axis=1, keepdims=True)
        p = jnp.exp2((s - m) * c)
        l = jnp.sum(p, axis=1, keepdims=True)
        o = jnp.dot(p.astype(jnp.bfloat16), v, preferred_element_type=jnp.float32) / l
        outs.append(o.astype(jnp.bfloat16))
    o = jnp.concatenate(outs, axis=1)
    x2 = x + jnp.dot(o, wo_ref[...], preferred_element_type=jnp.float32)
    x2_ref[...] = x2
    hm_ref[...] = _rms(x2, gm_ref[...]).astype(hm_ref.dtype)


def _cross_sublayer(x, kv, g_cross, w_cq, w_co, g_mlp, seq_len, mem_len, tq=256):
    n, d = x.shape
    qw = w_cq.shape[1]
    const = lambda i: (0, 0)
    return pl.pallas_call(
        _cross_kernel,
        out_shape=(jax.ShapeDtypeStruct((n, d), jnp.float32),
                   jax.ShapeDtypeStruct((n, d), jnp.bfloat16)),
        grid=(n // tq,),
        in_specs=[pl.BlockSpec((tq, d), lambda i: (i, 0)),
                  pl.BlockSpec((1, d), const),
                  pl.BlockSpec((d, qw), const),
                  pl.BlockSpec((mem_len, 2 * qw), lambda i: (i // (seq_len // tq), 0)),
                  pl.BlockSpec((qw, d), const),
                  pl.BlockSpec((1, d), const)],
        out_specs=(pl.BlockSpec((tq, d), lambda i: (i, 0)),
                   pl.BlockSpec((tq, d), lambda i: (i, 0))),
        compiler_params=_params("parallel"),
        name="cross_sublayer",
    )(x, g_cross.reshape(1, d), w_cq, kv, w_co, g_mlp.reshape(1, d))


def kernel(x, mem, ln_mix_g, w_in, moba_out_g, lambda_q1, lambda_k1, lambda_q2, lambda_k2,
           diff_subln_g, w_out, ln_cross_g, ln_mem_g, w_cq, w_ckv, w_co, ln_mlp_g,
           w_up, w_down, final_g):
    b, s, d = x.shape
    mem_len = mem.shape[1]
    depth = w_in.shape[0]
    mix_width = w_out.shape[1]
    moba_width = mix_width // 2
    moba_heads = moba_width // HEAD_DIM
    diff_heads = (mix_width - moba_width) // (2 * HEAD_DIM)
    bf = jnp.bfloat16

    xf = x.reshape(b * s, d)
    memf = mem.reshape(b * mem_len, d)
    for l in range(depth):
        lambda_init = 0.8 - 0.6 * math.exp(-0.3 * l)

        h = _rmsnorm(xf, ln_mix_g[l], bf)
        w_in_bf = w_in[l].astype(bf)
        qk = _inproj_qk(h, w_in_bf, s, moba_width).reshape(b, s, -1)
        bpg = moba_width // 1024
        vproj = _matmul(h, w_in_bf, bm=1024, bn=1024, bk=d, out_dtype=bf, n_out=2 * moba_width,
                        w_col_block=lambda j: (2 + 3 * (j // bpg)) * bpg + j % bpg,
                        name="inproj_v").reshape(b, s, -1)
        merged = _moba_attention(qk, vproj, moba_out_g[l], moba_heads, mix_width)
        merged = _diff_attention(qk, vproj, merged, lambda_q1[l], lambda_k1[l], lambda_q2[l],
                                 lambda_k2[l], diff_subln_g[l], diff_heads, lambda_init)
        x1 = _matmul(merged.reshape(b * s, mix_width), w_out, layer=l, bm=1024, bn=512,
                     bk=mix_width, out_dtype=jnp.float32, res=xf, name="outproj")

        mn = _rmsnorm(memf, ln_mem_g[l], bf)
        kv = _matmul(mn, w_ckv, layer=l, bm=b * mem_len, bn=512, bk=d, out_dtype=bf,
                     name="mem_kv")
        x2, hm = _cross_sublayer(x1, kv, ln_cross_g[l], w_cq[l].astype(bf), w_co[l].astype(bf),
                                 ln_mlp_g[l], s, mem_len)

        u = _matmul(hm, w_up, layer=l, bm=1024, bn=512, bk=d, out_dtype=bf, act="relu2",
                    name="mlp_up")
        xf = _matmul(u, w_down[l].astype(bf), bm=1024, bn=1024, bk=4096, out_dtype=jnp.float32,
                     res=x2, name="mlp_down")
    return _rmsnorm(xf, final_g, jnp.float32).reshape(b, s, d)
```

```python
import functools
import math

import jax
import jax.numpy as jnp
from jax import lax
from jax.experimental import pallas as pl
from jax.experimental.pallas import tpu as pltpu

HEAD_DIM = 128
MOBA_BLOCK = 256
MOBA_TOPK = 3
ROT_DIM = HEAD_DIM // 4
ROPE_THETA = 500000.0
MEM_HEADS = 4
EPS = 1e-5

LANES = 128
VMEM_LIMIT_BYTES = 56 * 2**20
NEG_BIG = -1e30
LOG2E = 1.4426950408889634
QK_SCALE = HEAD_DIM ** -0.5 * LOG2E
NT_DIMS = (((1,), (1,)), ((), ()))
ATT_BLOCK = 256
ATT_GROUP = 2
MM_CHUNK = 256


def _params(*sem):
    return pltpu.CompilerParams(dimension_semantics=sem, vmem_limit_bytes=VMEM_LIMIT_BYTES)


def _rms(x, g):
    return x * lax.rsqrt(jnp.mean(x * x, axis=-1, keepdims=True) + EPS) * g


def _rmsnorm_kernel(x_ref, g_ref, o_ref):
    o_ref[...] = _rms(x_ref[...], g_ref[...]).astype(o_ref.dtype)


def _rmsnorm(x, g, out_dtype, bm=256):
    n, d = x.shape
    return pl.pallas_call(
        _rmsnorm_kernel,
        out_shape=jax.ShapeDtypeStruct((n, d), out_dtype),
        grid=(n // bm,),
        in_specs=[pl.BlockSpec((bm, d), lambda i: (i, 0)),
                  pl.BlockSpec((1, d), lambda i: (0, 0))],
        out_specs=pl.BlockSpec((bm, d), lambda i: (i, 0)),
        compiler_params=_params("parallel"),
        name="rmsnorm",
    )(x, g.reshape(1, d))


def _weight_cols(w_ref, cols):
    return w_ref[:, cols].astype(jnp.bfloat16)


def _weight_spec(w, layer, bk, bn, index):
    if w.ndim == 2:
        return pl.BlockSpec((bk, bn), index)
    return pl.BlockSpec((None, bk, bn), lambda *g: (layer,) + tuple(index(*g)))


def _mm_kernel(*refs, nk, act, has_res, has_side):
    a_ref, w_ref = refs[0], refs[1]
    res_ref = refs[2] if has_res else None
    o_ref = refs[2 + has_res + has_side]
    if has_side:
        refs[3 + has_res + has_side][...] = refs[2 + has_res][...].astype(jnp.bfloat16)
    if nk > 1:
        assert act is None and o_ref.dtype == jnp.float32

        @pl.when(pl.program_id(2) == 0)
        def _():
            o_ref[...] = res_ref[...] if has_res else jnp.zeros_like(o_ref)

    for c in range(o_ref.shape[1] // MM_CHUNK):
        cols = slice(c * MM_CHUNK, (c + 1) * MM_CHUNK)
        acc = jnp.dot(a_ref[...], _weight_cols(w_ref, cols), preferred_element_type=jnp.float32)
        if nk > 1:
            o_ref[:, cols] = acc + o_ref[:, cols]
        else:
            if act == "relu2":
                acc = jnp.square(jnp.maximum(acc, 0.0))
            if has_res:
                acc = acc + res_ref[:, cols]
            o_ref[:, cols] = acc.astype(o_ref.dtype)


def _matmul(a, w, *, bm, bn, bk, out_dtype, act=None, res=None, layer=0, n_out=None,
            w_col_block=lambda j: j, round_also=None, name):
    m, kdim = a.shape
    n = w.shape[-1] if n_out is None else n_out
    nk = kdim // bk
    nj = n // bn
    in_specs = [pl.BlockSpec((bm, bk), lambda i, j, k: (i, k)),
                _weight_spec(w, layer, bk, bn, lambda i, j, k: (k, w_col_block(j)))]
    args = [a, w]
    if res is not None:
        in_specs.append(pl.BlockSpec((bm, bn), lambda i, j, k: (i, j)))
        args.append(res)
    out_shape = jax.ShapeDtypeStruct((m, n), out_dtype)
    out_specs = pl.BlockSpec((bm, bn), lambda i, j, k: (i, j))
    if round_also is not None:
        _, rows, cols = round_also.shape
        slab = rows // ((m // bm) * nj * nk)
        step = lambda i, j, k: (i * nj + j) * nk + k
        in_specs.append(pl.BlockSpec((None, slab, cols), lambda i, j, k: (layer, step(i, j, k), 0)))
        args.append(round_also)
        out_shape = (out_shape, jax.ShapeDtypeStruct((rows, cols), jnp.bfloat16))
        out_specs = (out_specs, pl.BlockSpec((slab, cols), lambda i, j, k: (step(i, j, k), 0)))
    return pl.pallas_call(
        functools.partial(_mm_kernel, nk=nk, act=act, has_res=res is not None,
                          has_side=round_also is not None),
        out_shape=out_shape,
        grid=(m // bm, nj, nk),
        in_specs=in_specs,
        out_specs=out_specs,
        compiler_params=_params("parallel", "parallel", "arbitrary"),
        name=name,
    )(*args)


def _inproj_kernel(h_ref, w_ref, cos_ref, sa_ref, sb_ref, o_ref):
    cos, sa, sb = cos_ref[...], sa_ref[...], sb_ref[...]
    for c in range(o_ref.shape[1] // MM_CHUNK):
        acc = jnp.dot(h_ref[...], _weight_cols(w_ref, slice(c * MM_CHUNK, (c + 1) * MM_CHUNK)),
                      preferred_element_type=jnp.float32)
        for hd in range(MM_CHUNK // HEAD_DIM):
            a = acc[:, hd * HEAD_DIM:(hd + 1) * HEAD_DIM]
            r = (a * cos + pltpu.roll(a, HEAD_DIM - ROT_DIM // 2, 1) * sa
                 + pltpu.roll(a, ROT_DIM // 2, 1) * sb)
            col = c * MM_CHUNK + hd * HEAD_DIM
            o_ref[:, col:col + HEAD_DIM] = r.astype(o_ref.dtype)


def _rope_tables(seq_len):
    half = ROT_DIM // 2
    pos = jnp.arange(seq_len, dtype=jnp.float32)
    inv_freq = ROPE_THETA ** (-jnp.arange(0, ROT_DIM, 2, dtype=jnp.float32) / ROT_DIM)
    ang = pos[:, None] * inv_freq[None, :]
    cos, sin = jnp.cos(ang), jnp.sin(ang)
    z = lambda w: jnp.zeros((seq_len, w), jnp.float32)
    cos_t = jnp.concatenate([cos, cos, jnp.ones((seq_len, HEAD_DIM - ROT_DIM), jnp.float32)], axis=1)
    sin_a = jnp.concatenate([-sin, z(HEAD_DIM - half)], axis=1)
    sin_b = jnp.concatenate([z(half), sin, z(HEAD_DIM - ROT_DIM)], axis=1)
    return cos_t, sin_a, sin_b


def _inproj_qk(h, w, seq_len, group_width, bm=1024, bn=1024):
    m, kdim = h.shape
    bpg = group_width // bn
    cos_t, sin_a, sin_b = (jnp.stack([t * QK_SCALE, t]) for t in _rope_tables(seq_len))
    tspec = pl.BlockSpec((None, bm, HEAD_DIM),
                         lambda i, j: ((j // bpg) % 2, i % (seq_len // bm), 0))

    def w_block(i, j):
        out_group = j // bpg
        return 0, (out_group + out_group // 2) * bpg + j % bpg

    return pl.pallas_call(
        _inproj_kernel,
        out_shape=jax.ShapeDtypeStruct((m, 4 * group_width), jnp.bfloat16),
        grid=(m // bm, 4 * bpg),
        in_specs=[pl.BlockSpec((bm, kdim), lambda i, j: (i, 0)),
                  pl.BlockSpec((kdim, bn), w_block),
                  tspec, tspec, tspec],
        out_specs=pl.BlockSpec((bm, bn), lambda i, j: (i, j)),
        compiler_params=_params("parallel", "parallel"),
        name="inproj_qk_rope",
    )(h, w, cos_t, sin_a, sin_b)


VT_PAD = 16


def _scores(k, q):
    return lax.dot_general(k, q, NT_DIMS, preferred_element_type=jnp.float32)


def _transposed_values(v_blk):
    vt = v_blk.astype(jnp.float32).T.astype(jnp.bfloat16)
    row = lax.broadcasted_iota(jnp.int32, (VT_PAD, v_blk.shape[0]), 0)
    return jnp.concatenate([vt, jnp.where(row == 0, 1.0, 0.0).astype(jnp.bfloat16)], axis=0)


def _weights(x):
    return jnp.exp2(x).astype(jnp.bfloat16)


def _diag_state(s, vt):
    key = lax.broadcasted_iota(jnp.int32, s.shape, 0)
    qry = lax.broadcasted_iota(jnp.int32, s.shape, 1)
    s = jnp.where(key <= qry, s, NEG_BIG)
    m = jnp.max(s, axis=0, keepdims=True)
    return m, jnp.dot(vt, _weights(s - m), preferred_element_type=jnp.float32)


def _normalised(acc):
    v_dim = acc.shape[0] - VT_PAD
    return acc[:v_dim] / acc[v_dim:v_dim + 1]


def _update_state(pieces, biases, vts, state):
    m, acc = state
    m_new = m
    for s, b in zip(pieces, biases):
        m_new = jnp.maximum(m_new, jnp.max(s, axis=0, keepdims=True) + b)
    acc = jnp.exp2(m - m_new) * acc
    for s, b, vt in zip(pieces, biases, vts):
        acc = acc + jnp.dot(vt, _weights(s - (m_new - b)), preferred_element_type=jnp.float32)
    return m_new, acc


def _key_rows(k_ref, first_block, n_blocks, cols=slice(None)):
    rows = n_blocks * ATT_BLOCK
    return k_ref[pl.ds(pl.multiple_of(first_block * ATT_BLOCK, ATT_BLOCK), rows), cols]


def _moba_kernel(q_ref, k_ref, v_ref, g_ref, o_ref, kmh_ref, kml_ref, vt_ref, bias_ref, *,
                 n_blocks, n_heads):
    qi = pl.program_id(2)
    heads = [slice(h * HEAD_DIM, (h + 1) * HEAD_DIM) for h in range(n_heads)]

    @pl.when(qi == 0)
    def _():
        for h, cols in enumerate(heads):
            kf = k_ref[:, cols].astype(jnp.float32).reshape(n_blocks, MOBA_BLOCK, HEAD_DIM)
            km = jnp.sum(kf, axis=1) * (1.0 / MOBA_BLOCK)
            hi = km.astype(jnp.bfloat16)
            kmh_ref[h] = hi
            kml_ref[h] = (km - hi.astype(jnp.float32)).astype(jnp.bfloat16)
            for n in range(n_blocks):
                vt_ref[h * n_blocks + n] = _transposed_values(
                    v_ref[n * ATT_BLOCK:(n + 1) * ATT_BLOCK, cols])

    qs = [q_ref[:, cols] for cols in heads]
    for h, q in enumerate(qs):
        gate = (lax.dot_general(kmh_ref[h], q, NT_DIMS, preferred_element_type=jnp.float32)
                + lax.dot_general(kml_ref[h], q, NT_DIMS, preferred_element_type=jnp.float32))
        blk = lax.broadcasted_iota(jnp.int32, gate.shape, 0)
        past = blk < qi
        g = jnp.where(past, gate, -jnp.inf)
        sel = jnp.zeros(gate.shape, jnp.float32)
        for _ in range(MOBA_TOPK):
            mx = jnp.max(g, axis=0, keepdims=True)
            first = jnp.min(jnp.where(g == mx, blk, n_blocks), axis=0, keepdims=True)
            hit = blk == first
            sel = jnp.where(hit, jnp.where(past, 1.0, 0.0), sel)
            g = jnp.where(hit, -jnp.inf, g)
        bias_ref[h] = jnp.where(sel > 0.0, 0.0, NEG_BIG)

    diag = [_scores(_key_rows(k_ref, qi, 1, cols), q) for q, cols in zip(qs, heads)]
    states = tuple(_diag_state(s, vt_ref[h * n_blocks + qi]) for h, s in enumerate(diag))

    def group(i, states):
        first = i * ATT_GROUP
        scores = [_scores(_key_rows(k_ref, first, ATT_GROUP, cols), q)
                  for q, cols in zip(qs, heads)]
        out = []
        for h, (s, st) in enumerate(zip(scores, states)):
            pieces = [s[t * ATT_BLOCK:(t + 1) * ATT_BLOCK] for t in range(ATT_GROUP)]
            biases = [bias_ref[h, pl.ds(first + t, 1), :] for t in range(ATT_GROUP)]
            vts = [vt_ref[h * n_blocks + first + t] for t in range(ATT_GROUP)]
            out.append(_update_state(pieces, biases, vts, st))
        return tuple(out)

    states = lax.fori_loop(0, lax.div(qi + (ATT_GROUP - 1), ATT_GROUP), group, states)
    for cols, (_, acc) in zip(heads, states):
        o_ref[:, cols] = _rms(_normalised(acc).T, g_ref[:, cols]).astype(o_ref.dtype)


def _moba_attention(qk, vproj, out_g, n_heads, mix_width, heads_per_step=8):
    b, s, _ = qk.shape
    n_blocks = s // MOBA_BLOCK
    hp = heads_per_step
    w = hp * HEAD_DIM
    n_steps = n_heads // hp
    return pl.pallas_call(
        functools.partial(_moba_kernel, n_blocks=n_blocks, n_heads=hp),
        out_shape=jax.ShapeDtypeStruct((b, s, mix_width), jnp.bfloat16),
        grid=(b, n_steps, n_blocks),
        in_specs=[pl.BlockSpec((None, MOBA_BLOCK, w), lambda bi, h, qi: (bi, qi, h)),
                  pl.BlockSpec((None, s, w), lambda bi, h, qi: (bi, 0, n_steps + h)),
                  pl.BlockSpec((None, s, w), lambda bi, h, qi: (bi, 0, h)),
                  pl.BlockSpec((None, 1, w), lambda bi, h, qi: (h, 0, 0))],
        out_specs=pl.BlockSpec((None, MOBA_BLOCK, w), lambda bi, h, qi: (bi, qi, h)),
        scratch_shapes=[pltpu.VMEM((hp, n_blocks, HEAD_DIM), jnp.bfloat16),
                        pltpu.VMEM((hp, n_blocks, HEAD_DIM), jnp.bfloat16),
                        pltpu.VMEM((hp * n_blocks, HEAD_DIM + VT_PAD, MOBA_BLOCK), jnp.bfloat16),
                        pltpu.VMEM((hp, n_blocks, MOBA_BLOCK), jnp.float32)],
        compiler_params=_params("parallel", "parallel", "arbitrary"),
        name="moba_attention",
    )(qk, qk, vproj, out_g.reshape(n_steps, 1, w))


def _diff_kernel(lq1_ref, lk1_ref, lq2_ref, lk2_ref, q_ref, k_ref, v_ref, g_ref, merged_ref,
                 o_ref, vt_ref, *, lambda_init, n_heads):
    del merged_ref
    qi = pl.program_id(2)
    n_blocks = vt_ref.shape[0] // n_heads
    vw = 2 * HEAD_DIM

    @pl.when(qi == 0)
    def _():
        for h in range(n_heads):
            for n in range(n_blocks):
                vt_ref[h * n_blocks + n] = _transposed_values(
                    v_ref[n * ATT_BLOCK:(n + 1) * ATT_BLOCK, h * vw:(h + 1) * vw])

    subs = [slice(j * HEAD_DIM, (j + 1) * HEAD_DIM) for j in range(2 * n_heads)]
    qs = [q_ref[:, cols] for cols in subs]
    diag = [_scores(_key_rows(k_ref, qi, 1, cols), q) for q, cols in zip(qs, subs)]
    states = tuple(_diag_state(s, vt_ref[(j // 2) * n_blocks + qi]) for j, s in enumerate(diag))

    def group(i, states):
        first = i * ATT_GROUP
        scores = [_scores(_key_rows(k_ref, first, ATT_GROUP, cols), q) for q, cols in zip(qs, subs)]
        biases = [jnp.where(first + t < qi, 0.0, NEG_BIG) for t in range(ATT_GROUP)]
        out = []
        for j, (s, st) in enumerate(zip(scores, states)):
            vts = [vt_ref[(j // 2) * n_blocks + first + t] for t in range(ATT_GROUP)]
            pieces = [s[t * ATT_BLOCK:(t + 1) * ATT_BLOCK] for t in range(ATT_GROUP)]
            out.append(_update_state(pieces, biases, vts, st))
        return tuple(out)

    states = lax.fori_loop(0, lax.div(qi + (ATT_GROUP - 1), ATT_GROUP), group, states)
    outs = [_normalised(acc) for _, acc in states]
    lam = (jnp.exp(jnp.sum(lq1_ref[...] * lk1_ref[...], axis=1, keepdims=True))
           - jnp.exp(jnp.sum(lq2_ref[...] * lk2_ref[...], axis=1, keepdims=True))
           + lambda_init)
    for h in range(n_heads):
        o = (outs[2 * h] - lam * outs[2 * h + 1]).T
        o_ref[:, h * vw:(h + 1) * vw] = (_rms(o, g_ref[...]) * (1.0 - lambda_init)).astype(o_ref.dtype)


def _diff_attention(qk, vproj, merged, lq1, lk1, lq2, lk2, subln_g, n_heads, lambda_init,
                    heads_per_step=4):
    b, s, qk_width = qk.shape
    v_col0 = out_col0 = vproj.shape[2] // 2
    col0 = qk_width // 2
    hp = heads_per_step
    w = hp * 2 * HEAD_DIM
    n_steps = n_heads // hp
    tq = ATT_BLOCK
    vec = lambda x: x.reshape(1, HEAD_DIM).astype(jnp.float32)
    vspec = pl.BlockSpec((1, HEAD_DIM), lambda bi, h, qi: (0, 0))
    qb, kb, vb, ob = col0 // w, col0 // w + n_steps, v_col0 // w, out_col0 // w
    return pl.pallas_call(
        functools.partial(_diff_kernel, lambda_init=lambda_init, n_heads=hp),
        out_shape=jax.ShapeDtypeStruct(merged.shape, merged.dtype),
        grid=(b, n_steps, s // tq),
        in_specs=[vspec, vspec, vspec, vspec,
                  pl.BlockSpec((None, tq, w), lambda bi, h, qi: (bi, qi, qb + h)),
                  pl.BlockSpec((None, s, w), lambda bi, h, qi: (bi, 0, kb + h)),
                  pl.BlockSpec((None, s, w), lambda bi, h, qi: (bi, 0, vb + h)),
                  pl.BlockSpec((1, 2 * HEAD_DIM), lambda bi, h, qi: (0, 0)),
                  pl.BlockSpec(memory_space=pl.ANY)],
        out_specs=pl.BlockSpec((None, tq, w), lambda bi, h, qi: (bi, qi, ob + h)),
        scratch_shapes=[pltpu.VMEM((hp * (s // ATT_BLOCK), 2 * HEAD_DIM + VT_PAD, ATT_BLOCK),
                                   jnp.bfloat16)],
        input_output_aliases={8: 0},
        compiler_params=_params("parallel", "parallel", "arbitrary"),
        name="diff_attention",
    )(vec(lq1), vec(lk1), vec(lq2), vec(lk2), qk, qk, vproj,
      subln_g.reshape(1, 2 * HEAD_DIM), merged)


def _cross_kernel(x_ref, gc_ref, wq_ref, kv_ref, wo_ref, gm_ref, x2_ref, hm_ref):
    c = QK_SCALE
    x = x_ref[...]
    hc = _rms(x, gc_ref[...]).astype(jnp.bfloat16)
    q = jnp.dot(hc, wq_ref[...], preferred_element_type=jnp.float32).astype(jnp.bfloat16)
    kv_w = MEM_HEADS * HEAD_DIM
    outs = []
    for h in range(MEM_HEADS):
        k = kv_ref[:, h * HEAD_DIM:(h + 1) * HEAD_DIM]
        v = kv_ref[:, kv_w + h * HEAD_DIM:kv_w + (h + 1) * HEAD_DIM]
        s = lax.dot_general(q[:, h * HEAD_DIM:(h + 1) * HEAD_DIM], k, NT_DIMS,
                            preferred_element_type=jnp.float32)
        m = jnp.max(s, axis=1, keepdims=True)
        p = jnp.exp2((s - m) * c)
        l = jnp.sum(p, axis=1, keepdims=True)
        o = jnp.dot(p.astype(jnp.bfloat16), v, preferred_element_type=jnp.float32) / l
        outs.append(o.astype(jnp.bfloat16))
    o = jnp.concatenate(outs, axis=1)
    x2 = x + jnp.dot(o, wo_ref[...], preferred_element_type=jnp.float32)
    x2_ref[...] = x2
    hm_ref[...] = _rms(x2, gm_ref[...]).astype(hm_ref.dtype)


def _cross_sublayer(x, kv, g_cross, w_cq, w_co, g_mlp, seq_len, mem_len, tq=256):
    n, d = x.shape
    qw = w_cq.shape[1]
    const = lambda i: (0, 0)
    return pl.pallas_call(
        _cross_kernel,
        out_shape=(jax.ShapeDtypeStruct((n, d), jnp.float32),
                   jax.ShapeDtypeStruct((n, d), jnp.bfloat16)),
        grid=(n // tq,),
        in_specs=[pl.BlockSpec((tq, d), lambda i: (i, 0)),
                  pl.BlockSpec((1, d), const),
                  pl.BlockSpec((d, qw), const),
                  pl.BlockSpec((mem_len, 2 * qw), lambda i: (i // (seq_len // tq), 0)),
                  pl.BlockSpec((qw, d), const),
                  pl.BlockSpec((1, d), const)],
        out_specs=(pl.BlockSpec((tq, d), lambda i: (i, 0)),
                   pl.BlockSpec((tq, d), lambda i: (i, 0))),
        compiler_params=_params("parallel"),
        name="cross_sublayer",
    )(x, g_cross.reshape(1, d), w_cq, kv, w_co, g_mlp.reshape(1, d))


def kernel(x, mem, ln_mix_g, w_in, moba_out_g, lambda_q1, lambda_k1, lambda_q2, lambda_k2,
           diff_subln_g, w_out, ln_cross_g, ln_mem_g, w_cq, w_ckv, w_co, ln_mlp_g,
           w_up, w_down, final_g):
    b, s, d = x.shape
    mem_len = mem.shape[1]
    depth = w_in.shape[0]
    mix_width = w_out.shape[1]
    moba_width = mix_width // 2
    moba_heads = moba_width // HEAD_DIM
    diff_heads = (mix_width - moba_width) // (2 * HEAD_DIM)
    bf = jnp.bfloat16

    xf = x.reshape(b * s, d)
    memf = mem.reshape(b * mem_len, d)
    for l in range(depth):
        lambda_init = 0.8 - 0.6 * math.exp(-0.3 * l)

        h = _rmsnorm(xf, ln_mix_g[l], bf)
        w_in_bf = w_in[l].astype(bf)
        qk = _inproj_qk(h, w_in_bf, s, moba_width).reshape(b, s, -1)
        bpg = moba_width // 1024
        vproj = _matmul(h, w_in_bf, bm=1024, bn=1024, bk=d, out_dtype=bf, n_out=2 * moba_width,
                        w_col_block=lambda j: (2 + 3 * (j // bpg)) * bpg + j % bpg,
                        name="inproj_v").reshape(b, s, -1)
        merged = _moba_attention(qk, vproj, moba_out_g[l], moba_heads, mix_width)
        merged = _diff_attention(qk, vproj, merged, lambda_q1[l], lambda_k1[l], lambda_q2[l],
                                 lambda_k2[l], diff_subln_g[l], diff_heads, lambda_init)
        x1 = _matmul(merged.reshape(b * s, mix_width), w_out, layer=l, bm=1024, bn=512,
                     bk=mix_width, out_dtype=jnp.float32, res=xf, name="outproj")

        mn = _rmsnorm(memf, ln_mem_g[l], bf)
        kv = _matmul(mn, w_ckv, layer=l, bm=b * mem_len, bn=512, bk=d, out_dtype=bf,
                     name="mem_kv")
        x2, hm = _cross_sublayer(x1, kv, ln_cross_g[l], w_cq[l].astype(bf), w_co[l].astype(bf),
                                 ln_mlp_g[l], s, mem_len)

        u, w_down_bf = _matmul(hm, w_up, layer=l, bm=1024, bn=512, bk=d, out_dtype=bf,
                               act="relu2", round_also=w_down, name="mlp_up")
        xf = _matmul(u, w_down_bf, bm=1024, bn=1024, bk=4096, out_dtype=jnp.float32,
                     res=x2, name="mlp_down")
    return _rmsnorm(xf, final_g, jnp.float32).reshape(b, s, d)
```

```python
import functools
import math

import jax
import jax.numpy as jnp
from jax import lax
from jax.experimental import pallas as pl
from jax.experimental.pallas import tpu as pltpu

HEAD_DIM = 128
MOBA_BLOCK = 256
MOBA_TOPK = 3
ROT_DIM = HEAD_DIM // 4
ROPE_THETA = 500000.0
MEM_HEADS = 4
EPS = 1e-5

LANES = 128
VMEM_LIMIT_BYTES = 56 * 2**20
NEG_BIG = -1e30
LOG2E = 1.4426950408889634
QK_SCALE = HEAD_DIM ** -0.5 * LOG2E
NT_DIMS = (((1,), (1,)), ((), ()))
ATT_BLOCK = 256
ATT_GROUP = 2
MM_CHUNK = 256


def _params(*sem):
    return pltpu.CompilerParams(dimension_semantics=sem, vmem_limit_bytes=VMEM_LIMIT_BYTES)


def _rms(x, g):
    return x * lax.rsqrt(jnp.mean(x * x, axis=-1, keepdims=True) + EPS) * g


def _rmsnorm_kernel(x_ref, g_ref, o_ref):
    o_ref[...] = _rms(x_ref[...], g_ref[...]).astype(o_ref.dtype)


def _rmsnorm(x, g, out_dtype, bm=256):
    n, d = x.shape
    return pl.pallas_call(
        _rmsnorm_kernel,
        out_shape=jax.ShapeDtypeStruct((n, d), out_dtype),
        grid=(n // bm,),
        in_specs=[pl.BlockSpec((bm, d), lambda i: (i, 0)),
                  pl.BlockSpec((1, d), lambda i: (0, 0))],
        out_specs=pl.BlockSpec((bm, d), lambda i: (i, 0)),
        compiler_params=_params("parallel"),
        name="rmsnorm",
    )(x, g.reshape(1, d))


def _weight_cols(w_ref, cols):
    return w_ref[:, cols].astype(jnp.bfloat16)


def _weight_spec(w, layer, bk, bn, index):
    if w.ndim == 2:
        return pl.BlockSpec((bk, bn), index)
    return pl.BlockSpec((None, bk, bn), lambda *g: (layer,) + tuple(index(*g)))


def _mm_kernel(*refs, nk, act, has_res, has_side):
    a_ref, w_ref = refs[0], refs[1]
    res_ref = refs[2] if has_res else None
    o_ref = refs[2 + has_res + has_side]
    if has_side:
        refs[3 + has_res + has_side][...] = refs[2 + has_res][...].astype(jnp.bfloat16)
    if nk > 1:
        assert act is None and o_ref.dtype == jnp.float32

        @pl.when(pl.program_id(2) == 0)
        def _():
            o_ref[...] = res_ref[...] if has_res else jnp.zeros_like(o_ref)

    for c in range(o_ref.shape[1] // MM_CHUNK):
        cols = slice(c * MM_CHUNK, (c + 1) * MM_CHUNK)
        acc = jnp.dot(a_ref[...], _weight_cols(w_ref, cols), preferred_element_type=jnp.float32)
        if nk > 1:
            o_ref[:, cols] = acc + o_ref[:, cols]
        else:
            if act == "relu2":
                acc = jnp.square(jnp.maximum(acc, 0.0))
            if has_res:
                acc = acc + res_ref[:, cols]
            o_ref[:, cols] = acc.astype(o_ref.dtype)


def _matmul(a, w, *, bm, bn, bk, out_dtype, act=None, res=None, layer=0, n_out=None,
            w_col_block=lambda j: j, round_also=None, name):
    m, kdim = a.shape
    n = w.shape[-1] if n_out is None else n_out
    nk = kdim // bk
    nj = n // bn
    in_specs = [pl.BlockSpec((bm, bk), lambda i, j, k: (i, k)),
                _weight_spec(w, layer, bk, bn, lambda i, j, k: (k, w_col_block(j)))]
    args = [a, w]
    if res is not None:
        in_specs.append(pl.BlockSpec((bm, bn), lambda i, j, k: (i, j)))
        args.append(res)
    out_shape = jax.ShapeDtypeStruct((m, n), out_dtype)
    out_specs = pl.BlockSpec((bm, bn), lambda i, j, k: (i, j))
    if round_also is not None:
        _, rows, cols = round_also.shape
        slab = rows // ((m // bm) * nj * nk)
        step = lambda i, j, k: (i * nj + j) * nk + k
        in_specs.append(pl.BlockSpec((None, slab, cols), lambda i, j, k: (layer, step(i, j, k), 0)))
        args.append(round_also)
        out_shape = (out_shape, jax.ShapeDtypeStruct((rows, cols), jnp.bfloat16))
        out_specs = (out_specs, pl.BlockSpec((slab, cols), lambda i, j, k: (step(i, j, k), 0)))
    return pl.pallas_call(
        functools.partial(_mm_kernel, nk=nk, act=act, has_res=res is not None,
                          has_side=round_also is not None),
        out_shape=out_shape,
        grid=(m // bm, nj, nk),
        in_specs=in_specs,
        out_specs=out_specs,
        compiler_params=_params("parallel", "parallel", "arbitrary"),
        name=name,
    )(*args)


def _inproj_kernel(h_ref, w_ref, cos_ref, sa_ref, sb_ref, o_ref):
    cos, sa, sb = cos_ref[...], sa_ref[...], sb_ref[...]
    for c in range(o_ref.shape[1] // MM_CHUNK):
        acc = jnp.dot(h_ref[...], _weight_cols(w_ref, slice(c * MM_CHUNK, (c + 1) * MM_CHUNK)),
                      preferred_element_type=jnp.float32)
        for hd in range(MM_CHUNK // HEAD_DIM):
            a = acc[:, hd * HEAD_DIM:(hd + 1) * HEAD_DIM]
            r = (a * cos + pltpu.roll(a, HEAD_DIM - ROT_DIM // 2, 1) * sa
                 + pltpu.roll(a, ROT_DIM // 2, 1) * sb)
            col = c * MM_CHUNK + hd * HEAD_DIM
            o_ref[:, col:col + HEAD_DIM] = r.astype(o_ref.dtype)


def _rope_tables(seq_len):
    half = ROT_DIM // 2
    pos = jnp.arange(seq_len, dtype=jnp.float32)
    inv_freq = ROPE_THETA ** (-jnp.arange(0, ROT_DIM, 2, dtype=jnp.float32) / ROT_DIM)
    ang = pos[:, None] * inv_freq[None, :]
    cos, sin = jnp.cos(ang), jnp.sin(ang)
    z = lambda w: jnp.zeros((seq_len, w), jnp.float32)
    cos_t = jnp.concatenate([cos, cos, jnp.ones((seq_len, HEAD_DIM - ROT_DIM), jnp.float32)], axis=1)
    sin_a = jnp.concatenate([-sin, z(HEAD_DIM - half)], axis=1)
    sin_b = jnp.concatenate([z(half), sin, z(HEAD_DIM - ROT_DIM)], axis=1)
    return cos_t, sin_a, sin_b


def _inproj_qk(h, w, seq_len, group_width, bm=1024, bn=1024):
    m, kdim = h.shape
    bpg = group_width // bn
    cos_t, sin_a, sin_b = (jnp.stack([t * QK_SCALE, t]) for t in _rope_tables(seq_len))
    tspec = pl.BlockSpec((None, bm, HEAD_DIM),
                         lambda i, j: ((j // bpg) % 2, i % (seq_len // bm), 0))

    def w_block(i, j):
        out_group = j // bpg
        return 0, (out_group + out_group // 2) * bpg + j % bpg

    return pl.pallas_call(
        _inproj_kernel,
        out_shape=jax.ShapeDtypeStruct((m, 4 * group_width), jnp.bfloat16),
        grid=(m // bm, 4 * bpg),
        in_specs=[pl.BlockSpec((bm, kdim), lambda i, j: (i, 0)),
                  pl.BlockSpec((kdim, bn), w_block),
                  tspec, tspec, tspec],
        out_specs=pl.BlockSpec((bm, bn), lambda i, j: (i, j)),
        compiler_params=_params("parallel", "parallel"),
        name="inproj_qk_rope",
    )(h, w, cos_t, sin_a, sin_b)


VT_PAD = 16


def _scores(k, q):
    return lax.dot_general(k, q, NT_DIMS, preferred_element_type=jnp.float32)


def _transposed_values(v_blk):
    vt = v_blk.astype(jnp.float32).T.astype(jnp.bfloat16)
    row = lax.broadcasted_iota(jnp.int32, (VT_PAD, v_blk.shape[0]), 0)
    return jnp.concatenate([vt, jnp.where(row == 0, 1.0, 0.0).astype(jnp.bfloat16)], axis=0)


def _weights(x):
    return jnp.exp2(x).astype(jnp.bfloat16)


def _diag_state(s, vt):
    key = lax.broadcasted_iota(jnp.int32, s.shape, 0)
    qry = lax.broadcasted_iota(jnp.int32, s.shape, 1)
    s = jnp.where(key <= qry, s, NEG_BIG)
    m = jnp.max(s, axis=0, keepdims=True)
    return m, jnp.dot(vt, _weights(s - m), preferred_element_type=jnp.float32)


def _normalised(acc):
    v_dim = acc.shape[0] - VT_PAD
    return acc[:v_dim] / acc[v_dim:v_dim + 1]


def _update_state(pieces, biases, vts, state):
    m, acc = state
    m_new = m
    for s, b in zip(pieces, biases):
        m_new = jnp.maximum(m_new, jnp.max(s, axis=0, keepdims=True) + b)
    acc = jnp.exp2(m - m_new) * acc
    for s, b, vt in zip(pieces, biases, vts):
        acc = acc + jnp.dot(vt, _weights(s - (m_new - b)), preferred_element_type=jnp.float32)
    return m_new, acc


def _key_rows(k_ref, first_block, n_blocks, cols=slice(None)):
    rows = n_blocks * ATT_BLOCK
    return k_ref[pl.ds(pl.multiple_of(first_block * ATT_BLOCK, ATT_BLOCK), rows), cols]


def _moba_kernel(q_ref, k_ref, v_ref, g_ref, o_ref, kmh_ref, kml_ref, vt_ref, bias_ref, *,
                 n_blocks, n_heads):
    qi = pl.program_id(2)
    heads = [slice(h * HEAD_DIM, (h + 1) * HEAD_DIM) for h in range(n_heads)]

    @pl.when(qi == 0)
    def _():
        for h, cols in enumerate(heads):
            kf = k_ref[:, cols].astype(jnp.float32).reshape(n_blocks, MOBA_BLOCK, HEAD_DIM)
            km = jnp.sum(kf, axis=1) * (1.0 / MOBA_BLOCK)
            hi = km.astype(jnp.bfloat16)
            kmh_ref[h] = hi
            kml_ref[h] = (km - hi.astype(jnp.float32)).astype(jnp.bfloat16)
            for n in range(n_blocks):
                vt_ref[h * n_blocks + n] = _transposed_values(
                    v_ref[n * ATT_BLOCK:(n + 1) * ATT_BLOCK, cols])

    qs = [q_ref[:, cols] for cols in heads]
    for h, q in enumerate(qs):
        gate = (lax.dot_general(kmh_ref[h], q, NT_DIMS, preferred_element_type=jnp.float32)
                + lax.dot_general(kml_ref[h], q, NT_DIMS, preferred_element_type=jnp.float32))
        blk = lax.broadcasted_iota(jnp.int32, gate.shape, 0)
        past = blk < qi
        g = jnp.where(past, gate, -jnp.inf)
        sel = jnp.zeros(gate.shape, jnp.float32)
        for _ in range(MOBA_TOPK):
            mx = jnp.max(g, axis=0, keepdims=True)
            first = jnp.min(jnp.where(g == mx, blk, n_blocks), axis=0, keepdims=True)
            hit = blk == first
            sel = jnp.where(hit, jnp.where(past, 1.0, 0.0), sel)
            g = jnp.where(hit, -jnp.inf, g)
        bias_ref[h] = jnp.where(sel > 0.0, 0.0, NEG_BIG)

    diag = [_scores(_key_rows(k_ref, qi, 1, cols), q) for q, cols in zip(qs, heads)]
    states = tuple(_diag_state(s, vt_ref[h * n_blocks + qi]) for h, s in enumerate(diag))

    def group(i, states):
        first = i * ATT_GROUP
        scores = [_scores(_key_rows(k_ref, first, ATT_GROUP, cols), q)
                  for q, cols in zip(qs, heads)]
        out = []
        for h, (s, st) in enumerate(zip(scores, states)):
            pieces = [s[t * ATT_BLOCK:(t + 1) * ATT_BLOCK] for t in range(ATT_GROUP)]
            biases = [bias_ref[h, pl.ds(first + t, 1), :] for t in range(ATT_GROUP)]
            vts = [vt_ref[h * n_blocks + first + t] for t in range(ATT_GROUP)]
            out.append(_update_state(pieces, biases, vts, st))
        return tuple(out)

    states = lax.fori_loop(0, lax.div(qi + (ATT_GROUP - 1), ATT_GROUP), group, states)
    for cols, (_, acc) in zip(heads, states):
        o_ref[:, cols] = _rms(_normalised(acc).T, g_ref[:, cols]).astype(o_ref.dtype)


def _moba_attention(qk, vproj, out_g, n_heads, mix_width, heads_per_step=8):
    b, s, _ = qk.shape
    n_blocks = s // MOBA_BLOCK
    hp = heads_per_step
    w = hp * HEAD_DIM
    n_steps = n_heads // hp
    return pl.pallas_call(
        functools.partial(_moba_kernel, n_blocks=n_blocks, n_heads=hp),
        out_shape=jax.ShapeDtypeStruct((b, s, mix_width), jnp.bfloat16),
        grid=(b, n_steps, n_blocks),
        in_specs=[pl.BlockSpec((None, MOBA_BLOCK, w), lambda bi, h, qi: (bi, qi, h)),
                  pl.BlockSpec((None, s, w), lambda bi, h, qi: (bi, 0, n_steps + h)),
                  pl.BlockSpec((None, s, w), lambda bi, h, qi: (bi, 0, h)),
                  pl.BlockSpec((None, 1, w), lambda bi, h, qi: (h, 0, 0))],
        out_specs=pl.BlockSpec((None, MOBA_BLOCK, w), lambda bi, h, qi: (bi, qi, h)),
        scratch_shapes=[pltpu.VMEM((hp, n_blocks, HEAD_DIM), jnp.bfloat16),
                        pltpu.VMEM((hp, n_blocks, HEAD_DIM), jnp.bfloat16),
                        pltpu.VMEM((hp * n_blocks, HEAD_DIM + VT_PAD, MOBA_BLOCK), jnp.bfloat16),
                        pltpu.VMEM((hp, n_blocks, MOBA_BLOCK), jnp.float32)],
        compiler_params=_params("parallel", "parallel", "arbitrary"),
        name="moba_attention",
    )(qk, qk, vproj, out_g.reshape(n_steps, 1, w))


def _diff_kernel(lq1_ref, lk1_ref, lq2_ref, lk2_ref, q_ref, k_ref, v_ref, g_ref, merged_ref,
                 o_ref, vt_ref, *, lambda_init, n_heads):
    del merged_ref
    qi = pl.program_id(2)
    n_blocks = vt_ref.shape[0] // n_heads
    vw = 2 * HEAD_DIM

    @pl.when(qi == 0)
    def _():
        for h in range(n_heads):
            for n in range(n_blocks):
                vt_ref[h * n_blocks + n] = _transposed_values(
                    v_ref[n * ATT_BLOCK:(n + 1) * ATT_BLOCK, h * vw:(h + 1) * vw])

    subs = [slice(j * HEAD_DIM, (j + 1) * HEAD_DIM) for j in range(2 * n_heads)]
    qs = [q_ref[:, cols] for cols in subs]
    diag = [_scores(_key_rows(k_ref, qi, 1, cols), q) for q, cols in zip(qs, subs)]
    states = tuple(_diag_state(s, vt_ref[(j // 2) * n_blocks + qi]) for j, s in enumerate(diag))

    def group(i, states):
        first = i * ATT_GROUP
        scores = [_scores(_key_rows(k_ref, first, ATT_GROUP, cols), q) for q, cols in zip(qs, subs)]
        biases = [jnp.where(first + t < qi, 0.0, NEG_BIG) for t in range(ATT_GROUP)]
        out = []
        for j, (s, st) in enumerate(zip(scores, states)):
            vts = [vt_ref[(j // 2) * n_blocks + first + t] for t in range(ATT_GROUP)]
            pieces = [s[t * ATT_BLOCK:(t + 1) * ATT_BLOCK] for t in range(ATT_GROUP)]
            out.append(_update_state(pieces, biases, vts, st))
        return tuple(out)

    states = lax.fori_loop(0, lax.div(qi + (ATT_GROUP - 1), ATT_GROUP), group, states)
    outs = [_normalised(acc) for _, acc in states]
    lam = (jnp.exp(jnp.sum(lq1_ref[...] * lk1_ref[...], axis=1, keepdims=True))
           - jnp.exp(jnp.sum(lq2_ref[...] * lk2_ref[...], axis=1, keepdims=True))
           + lambda_init)
    for h in range(n_heads):
        o = (outs[2 * h] - lam * outs[2 * h + 1]).T
        o_ref[:, h * vw:(h + 1) * vw] = (_rms(o, g_ref[...]) * (1.0 - lambda_init)).astype(o_ref.dtype)


def _diff_attention(qk, vproj, merged, lq1, lk1, lq2, lk2, subln_g, n_heads, lambda_init,
                    heads_per_step=4):
    b, s, qk_width = qk.shape
    v_col0 = out_col0 = vproj.shape[2] // 2
    col0 = qk_width // 2
    hp = heads_per_step
    w = hp * 2 * HEAD_DIM
    n_steps = n_heads // hp
    tq = ATT_BLOCK
    vec = lambda x: x.reshape(1, HEAD_DIM).astype(jnp.float32)
    vspec = pl.BlockSpec((1, HEAD_DIM), lambda bi, h, qi: (0, 0))
    qb, kb, vb, ob = col0 // w, col0 // w + n_steps, v_col0 // w, out_col0 // w
    return pl.pallas_call(
        functools.partial(_diff_kernel, lambda_init=lambda_init, n_heads=hp),
        out_shape=jax.ShapeDtypeStruct(merged.shape, merged.dtype),
        grid=(b, n_steps, s // tq),
        in_specs=[vspec, vspec, vspec, vspec,
                  pl.BlockSpec((None, tq, w), lambda bi, h, qi: (bi, qi, qb + h)),
                  pl.BlockSpec((None, s, w), lambda bi, h, qi: (bi, 0, kb + h)),
                  pl.BlockSpec((None, s, w), lambda bi, h, qi: (bi, 0, vb + h)),
                  pl.BlockSpec((1, 2 * HEAD_DIM), lambda bi, h, qi: (0, 0)),
                  pl.BlockSpec(memory_space=pl.ANY)],
        out_specs=pl.BlockSpec((None, tq, w), lambda bi, h, qi: (bi, qi, ob + h)),
        scratch_shapes=[pltpu.VMEM((hp * (s // ATT_BLOCK), 2 * HEAD_DIM + VT_PAD, ATT_BLOCK),
                                   jnp.bfloat16)],
        input_output_aliases={8: 0},
        compiler_params=_params("parallel", "parallel", "arbitrary"),
        name="diff_attention",
    )(vec(lq1), vec(lk1), vec(lq2), vec(lk2), qk, qk, vproj,
      subln_g.reshape(1, 2 * HEAD_DIM), merged)


def _cross_kernel(x_ref, gc_ref, wq_ref, kv_ref, wo_ref, gm_ref, x2_ref, hm_ref):
    c = QK_SCALE
    x = x_ref[...]
    hc = _rms(x, gc_ref[...]).astype(jnp.bfloat16)
    q = jnp.dot(hc, wq_ref[...], preferred_element_type=jnp.float32).astype(jnp.bfloat16)
    kv_w = MEM_HEADS * HEAD_DIM
    outs = []
    for h in range(MEM_HEADS):
        k = kv_ref[:, h * HEAD_DIM:(h + 1) * HEAD_DIM]
        v = kv_ref[:, kv_w + h * HEAD_DIM:kv_w + (h + 1) * HEAD_DIM]
        s = lax.dot_general(q[:, h * HEAD_DIM:(h + 1) * HEAD_DIM], k, NT_DIMS,
                            preferred_element_type=jnp.float32)
        m = jnp.max(s, axis=1, keepdims=True)
        p = jnp.exp2((s - m) * c)
        l = jnp.sum(p, axis=1, keepdims=True)
        o = jnp.dot(p.astype(jnp.bfloat16), v, preferred_element_type=jnp.float32) / l
        outs.append(o.astype(jnp.bfloat16))
    o = jnp.concatenate(outs, axis=1)
    x2 = x + jnp.dot(o, wo_ref[...], preferred_element_type=jnp.float32)
    x2_ref[...] = x2
    hm_ref[...] = _rms(x2, gm_ref[...]).astype(hm_ref.dtype)


def _cross_sublayer(x, kv, g_cross, w_cq, w_co, g_mlp, seq_len, mem_len, tq=256):
    n, d = x.shape
    qw = w_cq.shape[1]
    const = lambda i: (0, 0)
    return pl.pallas_call(
        _cross_kernel,
        out_shape=(jax.ShapeDtypeStruct((n, d), jnp.float32),
                   jax.ShapeDtypeStruct((n, d), jnp.bfloat16)),
        grid=(n // tq,),
        in_specs=[pl.BlockSpec((tq, d), lambda i: (i, 0)),
                  pl.BlockSpec((1, d), const),
                  pl.BlockSpec((d, qw), const),
                  pl.BlockSpec((mem_len, 2 * qw), lambda i: (i // (seq_len // tq), 0)),
                  pl.BlockSpec((qw, d), const),
                  pl.BlockSpec((1, d), const)],
        out_specs=(pl.BlockSpec((tq, d), lambda i: (i, 0)),
                   pl.BlockSpec((tq, d), lambda i: (i, 0))),
        compiler_params=_params("parallel"),
        name="cross_sublayer",
    )(x, g_cross.reshape(1, d), w_cq, kv, w_co, g_mlp.reshape(1, d))


def kernel(x, mem, ln_mix_g, w_in, moba_out_g, lambda_q1, lambda_k1, lambda_q2, lambda_k2,
           diff_subln_g, w_out, ln_cross_g, ln_mem_g, w_cq, w_ckv, w_co, ln_mlp_g,
           w_up, w_down, final_g):
    b, s, d = x.shape
    mem_len = mem.shape[1]
    depth = w_in.shape[0]
    mix_width = w_out.shape[1]
    moba_width = mix_width // 2
    moba_heads = moba_width // HEAD_DIM
    diff_heads = (mix_width - moba_width) // (2 * HEAD_DIM)
    bf = jnp.bfloat16

    xf = x.reshape(b * s, d)
    memf = mem.reshape(b * mem_len, d)
    for l in range(depth):
        lambda_init = 0.8 - 0.6 * math.exp(-0.3 * l)

        h = _rmsnorm(xf, ln_mix_g[l], bf)
        w_in_bf = w_in[l].astype(bf)
        qk = _inproj_qk(h, w_in_bf, s, moba_width).reshape(b, s, -1)
        bpg = moba_width // 1024
        vproj, w_up_bf = _matmul(h, w_in_bf, bm=1024, bn=1024, bk=d, out_dtype=bf,
                                 n_out=2 * moba_width, round_also=w_up, layer=l,
                                 w_col_block=lambda j: (2 + 3 * (j // bpg)) * bpg + j % bpg,
                                 name="inproj_v")
        vproj = vproj.reshape(b, s, -1)
        merged = _moba_attention(qk, vproj, moba_out_g[l], moba_heads, mix_width)
        merged = _diff_attention(qk, vproj, merged, lambda_q1[l], lambda_k1[l], lambda_q2[l],
                                 lambda_k2[l], diff_subln_g[l], diff_heads, lambda_init)
        x1 = _matmul(merged.reshape(b * s, mix_width), w_out, layer=l, bm=1024, bn=512,
                     bk=mix_width, out_dtype=jnp.float32, res=xf, name="outproj")

        mn = _rmsnorm(memf, ln_mem_g[l], bf)
        kv = _matmul(mn, w_ckv, layer=l, bm=b * mem_len, bn=512, bk=d, out_dtype=bf,
                     name="mem_kv")
        x2, hm = _cross_sublayer(x1, kv, ln_cross_g[l], w_cq[l].astype(bf), w_co[l].astype(bf),
                                 ln_mlp_g[l], s, mem_len)

        u, w_down_bf = _matmul(hm, w_up_bf, bm=1024, bn=1024, bk=d, out_dtype=bf,
                               act="relu2", round_also=w_down, layer=l, name="mlp_up")
        xf = _matmul(u, w_down_bf, bm=1024, bn=1024, bk=4096, out_dtype=jnp.float32,
                     res=x2, name="mlp_down")
    return _rmsnorm(xf, final_g, jnp.float32).reshape(b, s, d)
```

```python
import functools
import math

import jax
import jax.numpy as jnp
from jax import lax
from jax.experimental import pallas as pl
from jax.experimental.pallas import tpu as pltpu

HEAD_DIM = 128
MOBA_BLOCK = 256
MOBA_TOPK = 3
ROT_DIM = HEAD_DIM // 4
ROPE_THETA = 500000.0
MEM_HEADS = 4
EPS = 1e-5

VMEM_LIMIT_BYTES = 56 * 2**20
NEG_BIG = -1e30
LOG2E = 1.4426950408889634
QK_SCALE = HEAD_DIM ** -0.5 * LOG2E
NT_DIMS = (((1,), (1,)), ((), ()))
ATT_BLOCK = 256
ATT_GROUP = 2
MM_CHUNK = 256


def _params(*sem):
    return pltpu.CompilerParams(dimension_semantics=sem, vmem_limit_bytes=VMEM_LIMIT_BYTES)


def _rms(x, g):
    return x * lax.rsqrt(jnp.mean(x * x, axis=-1, keepdims=True) + EPS) * g


def _rmsnorm_kernel(x_ref, g_ref, o_ref):
    o_ref[...] = _rms(x_ref[...], g_ref[...]).astype(o_ref.dtype)


def _rmsnorm(x, g, out_dtype, bm=256):
    n, d = x.shape
    return pl.pallas_call(
        _rmsnorm_kernel,
        out_shape=jax.ShapeDtypeStruct((n, d), out_dtype),
        grid=(n // bm,),
        in_specs=[pl.BlockSpec((bm, d), lambda i: (i, 0)),
                  pl.BlockSpec((1, d), lambda i: (0, 0))],
        out_specs=pl.BlockSpec((bm, d), lambda i: (i, 0)),
        compiler_params=_params("parallel"),
        name="rmsnorm",
    )(x, g.reshape(1, d))


def _weight_cols(w_ref, cols):
    return w_ref[:, cols].astype(jnp.bfloat16)


def _weight_spec(w, layer, bk, bn, index):
    if w.ndim == 2:
        return pl.BlockSpec((bk, bn), index)
    return pl.BlockSpec((None, bk, bn), lambda *g: (layer,) + tuple(index(*g)))


def _mm_kernel(*refs, nk, act, has_res, n_side):
    a_ref, w_ref = refs[0], refs[1]
    res_ref = refs[2] if has_res else None
    side_in = refs[2 + has_res:2 + has_res + n_side]
    o_ref = refs[2 + has_res + n_side]
    side_out = refs[3 + has_res + n_side:]
    for src, dst in zip(side_in, side_out):
        dst[...] = src[...].astype(jnp.bfloat16)
    if nk > 1:
        assert act is None and o_ref.dtype == jnp.float32

        @pl.when(pl.program_id(2) == 0)
        def _():
            o_ref[...] = res_ref[...] if has_res else jnp.zeros_like(o_ref)

    for c in range(o_ref.shape[1] // MM_CHUNK):
        cols = slice(c * MM_CHUNK, (c + 1) * MM_CHUNK)
        acc = jnp.dot(a_ref[...], _weight_cols(w_ref, cols), preferred_element_type=jnp.float32)
        if nk > 1:
            o_ref[:, cols] = acc + o_ref[:, cols]
        else:
            if act == "relu2":
                acc = jnp.square(jnp.maximum(acc, 0.0))
            if has_res:
                acc = acc + res_ref[:, cols]
            o_ref[:, cols] = acc.astype(o_ref.dtype)


def _matmul(a, w, *, bm, bn, bk, out_dtype, act=None, res=None, layer=0, n_out=None,
            w_col_block=lambda j: j, round_also=(), name):
    m, kdim = a.shape
    n = w.shape[-1] if n_out is None else n_out
    nk = kdim // bk
    nj = n // bn
    in_specs = [pl.BlockSpec((bm, bk), lambda i, j, k: (i, k)),
                _weight_spec(w, layer, bk, bn, lambda i, j, k: (k, w_col_block(j)))]
    args = [a, w]
    if res is not None:
        in_specs.append(pl.BlockSpec((bm, bn), lambda i, j, k: (i, j)))
        args.append(res)
    out_shape = [jax.ShapeDtypeStruct((m, n), out_dtype)]
    out_specs = [pl.BlockSpec((bm, bn), lambda i, j, k: (i, j))]
    step = lambda i, j, k: (i * nj + j) * nk + k
    for side in round_also:
        _, rows, cols = side.shape
        slab = rows // ((m // bm) * nj * nk)
        in_specs.append(pl.BlockSpec((None, slab, cols), lambda i, j, k: (layer, step(i, j, k), 0)))
        args.append(side)
        out_shape.append(jax.ShapeDtypeStruct((rows, cols), jnp.bfloat16))
        out_specs.append(pl.BlockSpec((slab, cols), lambda i, j, k: (step(i, j, k), 0)))
    out = pl.pallas_call(
        functools.partial(_mm_kernel, nk=nk, act=act, has_res=res is not None,
                          n_side=len(round_also)),
        out_shape=out_shape,
        grid=(m // bm, nj, nk),
        in_specs=in_specs,
        out_specs=out_specs,
        compiler_params=_params("parallel", "parallel", "arbitrary"),
        name=name,
    )(*args)
    return out if round_also else out[0]


def _inproj_kernel(h_ref, w_ref, cos_ref, sa_ref, sb_ref, o_ref):
    cos, sa, sb = cos_ref[...], sa_ref[...], sb_ref[...]
    for c in range(o_ref.shape[1] // MM_CHUNK):
        acc = jnp.dot(h_ref[...], _weight_cols(w_ref, slice(c * MM_CHUNK, (c + 1) * MM_CHUNK)),
                      preferred_element_type=jnp.float32)
        for hd in range(MM_CHUNK // HEAD_DIM):
            a = acc[:, hd * HEAD_DIM:(hd + 1) * HEAD_DIM]
            r = (a * cos + pltpu.roll(a, HEAD_DIM - ROT_DIM // 2, 1) * sa
                 + pltpu.roll(a, ROT_DIM // 2, 1) * sb)
            col = c * MM_CHUNK + hd * HEAD_DIM
            o_ref[:, col:col + HEAD_DIM] = r.astype(o_ref.dtype)


def _rope_tables(seq_len):
    half = ROT_DIM // 2
    pos = jnp.arange(seq_len, dtype=jnp.float32)
    inv_freq = ROPE_THETA ** (-jnp.arange(0, ROT_DIM, 2, dtype=jnp.float32) / ROT_DIM)
    ang = pos[:, None] * inv_freq[None, :]
    cos, sin = jnp.cos(ang), jnp.sin(ang)
    z = lambda w: jnp.zeros((seq_len, w), jnp.float32)
    cos_t = jnp.concatenate([cos, cos, jnp.ones((seq_len, HEAD_DIM - ROT_DIM), jnp.float32)], axis=1)
    sin_a = jnp.concatenate([-sin, z(HEAD_DIM - half)], axis=1)
    sin_b = jnp.concatenate([z(half), sin, z(HEAD_DIM - ROT_DIM)], axis=1)
    return cos_t, sin_a, sin_b


def _inproj_qk(h, w, seq_len, group_width, bm=1024, bn=1024):
    m, kdim = h.shape
    bpg = group_width // bn
    cos_t, sin_a, sin_b = (jnp.stack([t * QK_SCALE, t]) for t in _rope_tables(seq_len))
    tspec = pl.BlockSpec((None, bm, HEAD_DIM),
                         lambda i, j: ((j // bpg) % 2, i % (seq_len // bm), 0))

    def w_block(i, j):
        out_group = j // bpg
        return 0, (out_group + out_group // 2) * bpg + j % bpg

    return pl.pallas_call(
        _inproj_kernel,
        out_shape=jax.ShapeDtypeStruct((m, 4 * group_width), jnp.bfloat16),
        grid=(m // bm, 4 * bpg),
        in_specs=[pl.BlockSpec((bm, kdim), lambda i, j: (i, 0)),
                  pl.BlockSpec((kdim, bn), w_block),
                  tspec, tspec, tspec],
        out_specs=pl.BlockSpec((bm, bn), lambda i, j: (i, j)),
        compiler_params=_params("parallel", "parallel"),
        name="inproj_qk_rope",
    )(h, w, cos_t, sin_a, sin_b)


VT_PAD = 16


def _scores(k, q):
    return lax.dot_general(k, q, NT_DIMS, preferred_element_type=jnp.float32)


def _transposed_values(v_blk):
    vt = v_blk.astype(jnp.float32).T.astype(jnp.bfloat16)
    row = lax.broadcasted_iota(jnp.int32, (VT_PAD, v_blk.shape[0]), 0)
    return jnp.concatenate([vt, jnp.where(row == 0, 1.0, 0.0).astype(jnp.bfloat16)], axis=0)


def _weights(x):
    return jnp.exp2(x).astype(jnp.bfloat16)


def _diag_state(s, vt):
    key = lax.broadcasted_iota(jnp.int32, s.shape, 0)
    qry = lax.broadcasted_iota(jnp.int32, s.shape, 1)
    s = jnp.where(key <= qry, s, NEG_BIG)
    m = jnp.max(s, axis=0, keepdims=True)
    return m, jnp.dot(vt, _weights(s - m), preferred_element_type=jnp.float32)


def _normalised(acc):
    v_dim = acc.shape[0] - VT_PAD
    return acc[:v_dim] / acc[v_dim:v_dim + 1]


def _update_state(pieces, biases, vts, state):
    m, acc = state
    m_new = m
    for s, b in zip(pieces, biases):
        m_new = jnp.maximum(m_new, jnp.max(s, axis=0, keepdims=True) + b)
    acc = jnp.exp2(m - m_new) * acc
    for s, b, vt in zip(pieces, biases, vts):
        acc = acc + jnp.dot(vt, _weights(s - (m_new - b)), preferred_element_type=jnp.float32)
    return m_new, acc


def _key_rows(k_ref, first_block, n_blocks, cols=slice(None)):
    rows = n_blocks * ATT_BLOCK
    return k_ref[pl.ds(pl.multiple_of(first_block * ATT_BLOCK, ATT_BLOCK), rows), cols]


def _moba_kernel(q_ref, k_ref, v_ref, g_ref, o_ref, kmh_ref, kml_ref, vt_ref, bias_ref, *,
                 n_blocks, n_heads):
    qi = pl.program_id(2)
    heads = [slice(h * HEAD_DIM, (h + 1) * HEAD_DIM) for h in range(n_heads)]

    @pl.when(qi == 0)
    def _():
        for h, cols in enumerate(heads):
            kf = k_ref[:, cols].astype(jnp.float32).reshape(n_blocks, MOBA_BLOCK, HEAD_DIM)
            km = jnp.sum(kf, axis=1) * (1.0 / MOBA_BLOCK)
            hi = km.astype(jnp.bfloat16)
            kmh_ref[h] = hi
            kml_ref[h] = (km - hi.astype(jnp.float32)).astype(jnp.bfloat16)
            for n in range(n_blocks):
                vt_ref[h * n_blocks + n] = _transposed_values(
                    v_ref[n * ATT_BLOCK:(n + 1) * ATT_BLOCK, cols])

    qs = [q_ref[:, cols] for cols in heads]
    for h, q in enumerate(qs):
        gate = (lax.dot_general(kmh_ref[h], q, NT_DIMS, preferred_element_type=jnp.float32)
                + lax.dot_general(kml_ref[h], q, NT_DIMS, preferred_element_type=jnp.float32))
        blk = lax.broadcasted_iota(jnp.int32, gate.shape, 0)
        past = blk < qi
        g = jnp.where(past, gate, -jnp.inf)
        sel = jnp.zeros(gate.shape, jnp.float32)
        for _ in range(MOBA_TOPK):
            mx = jnp.max(g, axis=0, keepdims=True)
            first = jnp.min(jnp.where(g == mx, blk, n_blocks), axis=0, keepdims=True)
            hit = blk == first
            sel = jnp.where(hit, jnp.where(past, 1.0, 0.0), sel)
            g = jnp.where(hit, -jnp.inf, g)
        bias_ref[h] = jnp.where(sel > 0.0, 0.0, NEG_BIG)

    diag = [_scores(_key_rows(k_ref, qi, 1, cols), q) for q, cols in zip(qs, heads)]
    states = tuple(_diag_state(s, vt_ref[h * n_blocks + qi]) for h, s in enumerate(diag))

    def group(i, states):
        first = i * ATT_GROUP
        scores = [_scores(_key_rows(k_ref, first, ATT_GROUP, cols), q)
                  for q, cols in zip(qs, heads)]
        out = []
        for h, (s, st) in enumerate(zip(scores, states)):
            pieces = [s[t * ATT_BLOCK:(t + 1) * ATT_BLOCK] for t in range(ATT_GROUP)]
            biases = [bias_ref[h, pl.ds(first + t, 1), :] for t in range(ATT_GROUP)]
            vts = [vt_ref[h * n_blocks + first + t] for t in range(ATT_GROUP)]
            out.append(_update_state(pieces, biases, vts, st))
        return tuple(out)

    states = lax.fori_loop(0, lax.div(qi + (ATT_GROUP - 1), ATT_GROUP), group, states)
    for cols, (_, acc) in zip(heads, states):
        o_ref[:, cols] = _rms(_normalised(acc).T, g_ref[:, cols]).astype(o_ref.dtype)


def _moba_attention(qk, vproj, out_g, n_heads, mix_width, heads_per_step=8):
    b, s, _ = qk.shape
    n_blocks = s // MOBA_BLOCK
    hp = heads_per_step
    w = hp * HEAD_DIM
    n_steps = n_heads // hp
    return pl.pallas_call(
        functools.partial(_moba_kernel, n_blocks=n_blocks, n_heads=hp),
        out_shape=jax.ShapeDtypeStruct((b, s, mix_width), jnp.bfloat16),
        grid=(b, n_steps, n_blocks),
        in_specs=[pl.BlockSpec((None, MOBA_BLOCK, w), lambda bi, h, qi: (bi, qi, h)),
                  pl.BlockSpec((None, s, w), lambda bi, h, qi: (bi, 0, n_steps + h)),
                  pl.BlockSpec((None, s, w), lambda bi, h, qi: (bi, 0, h)),
                  pl.BlockSpec((None, 1, w), lambda bi, h, qi: (h, 0, 0))],
        out_specs=pl.BlockSpec((None, MOBA_BLOCK, w), lambda bi, h, qi: (bi, qi, h)),
        scratch_shapes=[pltpu.VMEM((hp, n_blocks, HEAD_DIM), jnp.bfloat16),
                        pltpu.VMEM((hp, n_blocks, HEAD_DIM), jnp.bfloat16),
                        pltpu.VMEM((hp * n_blocks, HEAD_DIM + VT_PAD, MOBA_BLOCK), jnp.bfloat16),
                        pltpu.VMEM((hp, n_blocks, MOBA_BLOCK), jnp.float32)],
        compiler_params=_params("parallel", "parallel", "arbitrary"),
        name="moba_attention",
    )(qk, qk, vproj, out_g.reshape(n_steps, 1, w))


def _diff_kernel(lq1_ref, lk1_ref, lq2_ref, lk2_ref, q_ref, k_ref, v_ref, g_ref, merged_ref,
                 o_ref, vt_ref, *, lambda_init, n_heads):
    del merged_ref
    qi = pl.program_id(2)
    n_blocks = vt_ref.shape[0] // n_heads
    vw = 2 * HEAD_DIM

    @pl.when(qi == 0)
    def _():
        for h in range(n_heads):
            for n in range(n_blocks):
                vt_ref[h * n_blocks + n] = _transposed_values(
                    v_ref[n * ATT_BLOCK:(n + 1) * ATT_BLOCK, h * vw:(h + 1) * vw])

    subs = [slice(j * HEAD_DIM, (j + 1) * HEAD_DIM) for j in range(2 * n_heads)]
    qs = [q_ref[:, cols] for cols in subs]
    diag = [_scores(_key_rows(k_ref, qi, 1, cols), q) for q, cols in zip(qs, subs)]
    states = tuple(_diag_state(s, vt_ref[(j // 2) * n_blocks + qi]) for j, s in enumerate(diag))

    def group(i, states):
        first = i * ATT_GROUP
        scores = [_scores(_key_rows(k_ref, first, ATT_GROUP, cols), q) for q, cols in zip(qs, subs)]
        biases = [jnp.where(first + t < qi, 0.0, NEG_BIG) for t in range(ATT_GROUP)]
        out = []
        for j, (s, st) in enumerate(zip(scores, states)):
            vts = [vt_ref[(j // 2) * n_blocks + first + t] for t in range(ATT_GROUP)]
            pieces = [s[t * ATT_BLOCK:(t + 1) * ATT_BLOCK] for t in range(ATT_GROUP)]
            out.append(_update_state(pieces, biases, vts, st))
        return tuple(out)

    states = lax.fori_loop(0, lax.div(qi + (ATT_GROUP - 1), ATT_GROUP), group, states)
    outs = [_normalised(acc) for _, acc in states]
    lam = (jnp.exp(jnp.sum(lq1_ref[...] * lk1_ref[...], axis=1, keepdims=True))
           - jnp.exp(jnp.sum(lq2_ref[...] * lk2_ref[...], axis=1, keepdims=True))
           + lambda_init)
    for h in range(n_heads):
        o = (outs[2 * h] - lam * outs[2 * h + 1]).T
        o_ref[:, h * vw:(h + 1) * vw] = (_rms(o, g_ref[...]) * (1.0 - lambda_init)).astype(o_ref.dtype)


def _diff_attention(qk, vproj, merged, lq1, lk1, lq2, lk2, subln_g, n_heads, lambda_init,
                    heads_per_step=4):
    b, s, qk_width = qk.shape
    v_col0 = out_col0 = vproj.shape[2] // 2
    col0 = qk_width // 2
    hp = heads_per_step
    w = hp * 2 * HEAD_DIM
    n_steps = n_heads // hp
    tq = ATT_BLOCK
    vec = lambda x: x.reshape(1, HEAD_DIM).astype(jnp.float32)
    vspec = pl.BlockSpec((1, HEAD_DIM), lambda bi, h, qi: (0, 0))
    qb, kb, vb, ob = col0 // w, col0 // w + n_steps, v_col0 // w, out_col0 // w
    return pl.pallas_call(
        functools.partial(_diff_kernel, lambda_init=lambda_init, n_heads=hp),
        out_shape=jax.ShapeDtypeStruct(merged.shape, merged.dtype),
        grid=(b, n_steps, s // tq),
        in_specs=[vspec, vspec, vspec, vspec,
                  pl.BlockSpec((None, tq, w), lambda bi, h, qi: (bi, qi, qb + h)),
                  pl.BlockSpec((None, s, w), lambda bi, h, qi: (bi, 0, kb + h)),
                  pl.BlockSpec((None, s, w), lambda bi, h, qi: (bi, 0, vb + h)),
                  pl.BlockSpec((1, 2 * HEAD_DIM), lambda bi, h, qi: (0, 0)),
                  pl.BlockSpec(memory_space=pl.ANY)],
        out_specs=pl.BlockSpec((None, tq, w), lambda bi, h, qi: (bi, qi, ob + h)),
        scratch_shapes=[pltpu.VMEM((hp * (s // ATT_BLOCK), 2 * HEAD_DIM + VT_PAD, ATT_BLOCK),
                                   jnp.bfloat16)],
        input_output_aliases={8: 0},
        compiler_params=_params("parallel", "parallel", "arbitrary"),
        name="diff_attention",
    )(vec(lq1), vec(lk1), vec(lq2), vec(lk2), qk, qk, vproj,
      subln_g.reshape(1, 2 * HEAD_DIM), merged)


def _cross_kernel(x_ref, gc_ref, wq_ref, kv_ref, wo_ref, gm_ref, x2_ref, hm_ref):
    c = QK_SCALE
    x = x_ref[...]
    hc = _rms(x, gc_ref[...]).astype(jnp.bfloat16)
    q = jnp.dot(hc, wq_ref[...], preferred_element_type=jnp.float32).astype(jnp.bfloat16)
    kv_w = MEM_HEADS * HEAD_DIM
    outs = []
    for h in range(MEM_HEADS):
        k = kv_ref[:, h * HEAD_DIM:(h + 1) * HEAD_DIM]
        v = kv_ref[:, kv_w + h * HEAD_DIM:kv_w + (h + 1) * HEAD_DIM]
        s = lax.dot_general(q[:, h * HEAD_DIM:(h + 1) * HEAD_DIM], k, NT_DIMS,
                            preferred_element_type=jnp.float32)
        m = jnp.max(s, axis=1, keepdims=True)
        p = jnp.exp2((s - m) * c)
        l = jnp.sum(p, axis=1, keepdims=True)
        o = jnp.dot(p.astype(jnp.bfloat16), v, preferred_element_type=jnp.float32) / l
        outs.append(o.astype(jnp.bfloat16))
    o = jnp.concatenate(outs, axis=1)
    x2 = x + jnp.dot(o, wo_ref[...], preferred_element_type=jnp.float32)
    x2_ref[...] = x2
    hm_ref[...] = _rms(x2, gm_ref[...]).astype(hm_ref.dtype)


def _cross_sublayer(x, kv, g_cross, w_cq, w_co, g_mlp, seq_len, mem_len, tq=256):
    n, d = x.shape
    qw = w_cq.shape[1]
    const = lambda i: (0, 0)
    return pl.pallas_call(
        _cross_kernel,
        out_shape=(jax.ShapeDtypeStruct((n, d), jnp.float32),
                   jax.ShapeDtypeStruct((n, d), jnp.bfloat16)),
        grid=(n // tq,),
        in_specs=[pl.BlockSpec((tq, d), lambda i: (i, 0)),
                  pl.BlockSpec((1, d), const),
                  pl.BlockSpec((d, qw), const),
                  pl.BlockSpec((mem_len, 2 * qw), lambda i: (i // (seq_len // tq), 0)),
                  pl.BlockSpec((qw, d), const),
                  pl.BlockSpec((1, d), const)],
        out_specs=(pl.BlockSpec((tq, d), lambda i: (i, 0)),
                   pl.BlockSpec((tq, d), lambda i: (i, 0))),
        compiler_params=_params("parallel"),
        name="cross_sublayer",
    )(x, g_cross.reshape(1, d), w_cq, kv, w_co, g_mlp.reshape(1, d))


def kernel(x, mem, ln_mix_g, w_in, moba_out_g, lambda_q1, lambda_k1, lambda_q2, lambda_k2,
           diff_subln_g, w_out, ln_cross_g, ln_mem_g, w_cq, w_ckv, w_co, ln_mlp_g,
           w_up, w_down, final_g):
    b, s, d = x.shape
    mem_len = mem.shape[1]
    depth = w_in.shape[0]
    mix_width = w_out.shape[1]
    moba_width = mix_width // 2
    moba_heads = moba_width // HEAD_DIM
    diff_heads = (mix_width - moba_width) // (2 * HEAD_DIM)
    bf = jnp.bfloat16

    xf = x.reshape(b * s, d)
    memf = mem.reshape(b * mem_len, d)
    for l in range(depth):
        lambda_init = 0.8 - 0.6 * math.exp(-0.3 * l)

        h = _rmsnorm(xf, ln_mix_g[l], bf)
        w_in_bf = w_in[l].astype(bf)
        qk = _inproj_qk(h, w_in_bf, s, moba_width).reshape(b, s, -1)
        bpg = moba_width // 1024
        vproj, w_out_bf, w_up_bf = _matmul(
            h, w_in_bf, bm=1024, bn=1024, bk=d, out_dtype=bf, n_out=2 * moba_width,
            w_col_block=lambda j: (2 + 3 * (j // bpg)) * bpg + j % bpg,
            round_also=(w_out, w_up), layer=l, name="inproj_v")
        vproj = vproj.reshape(b, s, -1)
        merged = _moba_attention(qk, vproj, moba_out_g[l], moba_heads, mix_width)
        merged = _diff_attention(qk, vproj, merged, lambda_q1[l], lambda_k1[l], lambda_q2[l],
                                 lambda_k2[l], diff_subln_g[l], diff_heads, lambda_init)
        x1 = _matmul(merged.reshape(b * s, mix_width), w_out_bf, bm=1024, bn=512,
                     bk=mix_width, out_dtype=jnp.float32, res=xf, name="outproj")

        mn = _rmsnorm(memf, ln_mem_g[l], bf)
        kv = _matmul(mn, w_ckv, layer=l, bm=b * mem_len, bn=512, bk=d, out_dtype=bf,
                     name="mem_kv")
        x2, hm = _cross_sublayer(x1, kv, ln_cross_g[l], w_cq[l].astype(bf), w_co[l].astype(bf),
                                 ln_mlp_g[l], s, mem_len)

        u, w_down_bf = _matmul(hm, w_up_bf, bm=1024, bn=1024, bk=d, out_dtype=bf,
                               act="relu2", round_also=(w_down,), layer=l, name="mlp_up")
        xf = _matmul(u, w_down_bf, bm=1024, bn=1024, bk=4096, out_dtype=jnp.float32,
                     res=x2, name="mlp_down")
    return _rmsnorm(xf, final_g, jnp.float32).reshape(b, s, d)
```

```python
import functools
import math

import jax
import jax.numpy as jnp
from jax import lax
from jax.experimental import pallas as pl
from jax.experimental.pallas import tpu as pltpu

HEAD_DIM = 128
MOBA_BLOCK = 256
MOBA_TOPK = 3
ROT_DIM = HEAD_DIM // 4
ROPE_THETA = 500000.0
MEM_HEADS = 4
EPS = 1e-5

VMEM_LIMIT_BYTES = 56 * 2**20
NEG_BIG = -1e30
LOG2E = 1.4426950408889634
QK_SCALE = HEAD_DIM ** -0.5 * LOG2E
NT_DIMS = (((1,), (1,)), ((), ()))
ATT_BLOCK = 256
ATT_GROUP = 2
MM_CHUNK = 256


def _params(*sem):
    return pltpu.CompilerParams(dimension_semantics=sem, vmem_limit_bytes=VMEM_LIMIT_BYTES)


def _rms(x, g):
    return x * lax.rsqrt(jnp.mean(x * x, axis=-1, keepdims=True) + EPS) * g


def _rmsnorm_kernel(x_ref, g_ref, o_ref):
    o_ref[...] = _rms(x_ref[...], g_ref[...]).astype(o_ref.dtype)


def _rmsnorm(x, g, out_dtype, bm=256):
    n, d = x.shape
    return pl.pallas_call(
        _rmsnorm_kernel,
        out_shape=jax.ShapeDtypeStruct((n, d), out_dtype),
        grid=(n // bm,),
        in_specs=[pl.BlockSpec((bm, d), lambda i: (i, 0)),
                  pl.BlockSpec((1, d), lambda i: (0, 0))],
        out_specs=pl.BlockSpec((bm, d), lambda i: (i, 0)),
        compiler_params=_params("parallel"),
        name="rmsnorm",
    )(x, g.reshape(1, d))


def _weight_cols(w_ref, cols):
    return w_ref[:, cols].astype(jnp.bfloat16)


def _weight_spec(w, layer, bk, bn, index):
    if w.ndim == 2:
        return pl.BlockSpec((bk, bn), index)
    return pl.BlockSpec((None, bk, bn), lambda *g: (layer,) + tuple(index(*g)))


def _mm_kernel(*refs, nk, act, has_res, n_side):
    a_ref, w_ref = refs[0], refs[1]
    res_ref = refs[2] if has_res else None
    side_in = refs[2 + has_res:2 + has_res + n_side]
    o_ref = refs[2 + has_res + n_side]
    side_out = refs[3 + has_res + n_side:]
    for src, dst in zip(side_in, side_out):
        dst[...] = src[...].astype(jnp.bfloat16)
    if nk > 1:
        assert act is None and o_ref.dtype == jnp.float32

        @pl.when(pl.program_id(2) == 0)
        def _():
            o_ref[...] = res_ref[...] if has_res else jnp.zeros_like(o_ref)

    for c in range(o_ref.shape[1] // MM_CHUNK):
        cols = slice(c * MM_CHUNK, (c + 1) * MM_CHUNK)
        acc = jnp.dot(a_ref[...], _weight_cols(w_ref, cols), preferred_element_type=jnp.float32)
        if nk > 1:
            o_ref[:, cols] = acc + o_ref[:, cols]
        else:
            if act == "relu2":
                acc = jnp.square(jnp.maximum(acc, 0.0))
            if has_res:
                acc = acc + res_ref[:, cols]
            o_ref[:, cols] = acc.astype(o_ref.dtype)


def _matmul(a, w, *, bm, bn, bk, out_dtype, act=None, res=None, layer=0, n_out=None,
            w_col_block=lambda j: j, round_also=(), name):
    m, kdim = a.shape
    n = w.shape[-1] if n_out is None else n_out
    nk = kdim // bk
    nj = n // bn
    in_specs = [pl.BlockSpec((bm, bk), lambda i, j, k: (i, k)),
                _weight_spec(w, layer, bk, bn, lambda i, j, k: (k, w_col_block(j)))]
    args = [a, w]
    if res is not None:
        in_specs.append(pl.BlockSpec((bm, bn), lambda i, j, k: (i, j)))
        args.append(res)
    out_shape = [jax.ShapeDtypeStruct((m, n), out_dtype)]
    out_specs = [pl.BlockSpec((bm, bn), lambda i, j, k: (i, j))]
    step = lambda i, j, k: (i * nj + j) * nk + k
    for side in round_also:
        _, rows, cols = side.shape
        slab = rows // ((m // bm) * nj * nk)
        in_specs.append(pl.BlockSpec((None, slab, cols), lambda i, j, k: (layer, step(i, j, k), 0)))
        args.append(side)
        out_shape.append(jax.ShapeDtypeStruct((rows, cols), jnp.bfloat16))
        out_specs.append(pl.BlockSpec((slab, cols), lambda i, j, k: (step(i, j, k), 0)))
    out = pl.pallas_call(
        functools.partial(_mm_kernel, nk=nk, act=act, has_res=res is not None,
                          n_side=len(round_also)),
        out_shape=out_shape,
        grid=(m // bm, nj, nk),
        in_specs=in_specs,
        out_specs=out_specs,
        compiler_params=_params("parallel", "parallel", "arbitrary"),
        name=name,
    )(*args)
    return out if round_also else out[0]


def _inproj_kernel(h_ref, w_ref, cos_ref, sa_ref, sb_ref, o_ref, acc_ref):
    @pl.when(pl.program_id(0) == 0)
    def _():
        acc_ref[...] = jnp.zeros_like(acc_ref)

    half_rows = acc_ref.shape[0] // 2
    for r0 in (0, half_rows):
        rows = slice(r0, r0 + half_rows)
        cos, sa, sb = cos_ref[rows, :], sa_ref[rows, :], sb_ref[rows, :]
        for c in range(o_ref.shape[1] // HEAD_DIM):
            cols = slice(c * HEAD_DIM, (c + 1) * HEAD_DIM)
            a = acc_ref[rows, cols]
            r = (a * cos + pltpu.roll(a, HEAD_DIM - ROT_DIM // 2, 1) * sa
                 + pltpu.roll(a, ROT_DIM // 2, 1) * sb)
            o_ref[rows, cols] = r.astype(o_ref.dtype)
    acc_ref[...] = jnp.dot(h_ref[...], w_ref[...], preferred_element_type=jnp.float32)


def _rope_tables(seq_len):
    half = ROT_DIM // 2
    pos = jnp.arange(seq_len, dtype=jnp.float32)
    inv_freq = ROPE_THETA ** (-jnp.arange(0, ROT_DIM, 2, dtype=jnp.float32) / ROT_DIM)
    ang = pos[:, None] * inv_freq[None, :]
    cos, sin = jnp.cos(ang), jnp.sin(ang)
    z = lambda w: jnp.zeros((seq_len, w), jnp.float32)
    cos_t = jnp.concatenate([cos, cos, jnp.ones((seq_len, HEAD_DIM - ROT_DIM), jnp.float32)], axis=1)
    sin_a = jnp.concatenate([-sin, z(HEAD_DIM - half)], axis=1)
    sin_b = jnp.concatenate([z(half), sin, z(HEAD_DIM - ROT_DIM)], axis=1)
    return cos_t, sin_a, sin_b


def _inproj_qk(h, w, seq_len, group_width, bm=1024, bn=1024):
    m, kdim = h.shape
    bpg = group_width // bn
    nj = 4 * bpg
    n_blocks = (m // bm) * nj
    cos_t, sin_a, sin_b = (jnp.stack([t * QK_SCALE, t]) for t in _rope_tables(seq_len))

    mul = lambda t: jnp.minimum(t, n_blocks - 1)
    fin = lambda t: jnp.maximum(t - 1, 0)

    def w_block(t):
        out_group = (mul(t) % nj) // bpg
        return 0, (out_group + out_group // 2) * bpg + mul(t) % bpg

    tspec = pl.BlockSpec(
        (None, bm, HEAD_DIM),
        lambda t: (((fin(t) % nj) // bpg) % 2, (fin(t) // nj) % (seq_len // bm), 0))
    return pl.pallas_call(
        _inproj_kernel,
        out_shape=jax.ShapeDtypeStruct((m, 4 * group_width), jnp.bfloat16),
        grid=(n_blocks + 1,),
        in_specs=[pl.BlockSpec((bm, kdim), lambda t: (mul(t) // nj, 0)),
                  pl.BlockSpec((kdim, bn), w_block),
                  tspec, tspec, tspec],
        out_specs=pl.BlockSpec((bm, bn), lambda t: (fin(t) // nj, fin(t) % nj)),
        scratch_shapes=[pltpu.VMEM((bm, bn), jnp.float32)],
        compiler_params=_params("arbitrary"),
        name="inproj_qk_rope",
    )(h, w, cos_t, sin_a, sin_b)


VT_PAD = 16


def _scores(k, q):
    return lax.dot_general(k, q, NT_DIMS, preferred_element_type=jnp.float32)


def _transposed_values(v_blk):
    vt = v_blk.astype(jnp.float32).T.astype(jnp.bfloat16)
    row = lax.broadcasted_iota(jnp.int32, (VT_PAD, v_blk.shape[0]), 0)
    return jnp.concatenate([vt, jnp.where(row == 0, 1.0, 0.0).astype(jnp.bfloat16)], axis=0)


def _weights(x):
    return jnp.exp2(x).astype(jnp.bfloat16)


def _diag_state(s, vt):
    key = lax.broadcasted_iota(jnp.int32, s.shape, 0)
    qry = lax.broadcasted_iota(jnp.int32, s.shape, 1)
    s = jnp.where(key <= qry, s, NEG_BIG)
    m = jnp.max(s, axis=0, keepdims=True)
    return m, jnp.dot(vt, _weights(s - m), preferred_element_type=jnp.float32)


def _normalised(acc):
    v_dim = acc.shape[0] - VT_PAD
    return acc[:v_dim] / acc[v_dim:v_dim + 1]


def _update_state(pieces, biases, vts, state):
    m, acc = state
    m_new = m
    for s, b in zip(pieces, biases):
        m_new = jnp.maximum(m_new, jnp.max(s, axis=0, keepdims=True) + b)
    acc = jnp.exp2(m - m_new) * acc
    for s, b, vt in zip(pieces, biases, vts):
        acc = acc + jnp.dot(vt, _weights(s - (m_new - b)), preferred_element_type=jnp.float32)
    return m_new, acc


def _key_rows(k_ref, first_block, n_blocks, cols=slice(None)):
    rows = n_blocks * ATT_BLOCK
    return k_ref[pl.ds(pl.multiple_of(first_block * ATT_BLOCK, ATT_BLOCK), rows), cols]


def _moba_kernel(q_ref, k_ref, v_ref, g_ref, o_ref, kmh_ref, kml_ref, vt_ref, bias_ref, *,
                 n_blocks, n_heads):
    qi = pl.program_id(2)
    heads = [slice(h * HEAD_DIM, (h + 1) * HEAD_DIM) for h in range(n_heads)]

    @pl.when(qi == 0)
    def _():
        for h, cols in enumerate(heads):
            kf = k_ref[:, cols].astype(jnp.float32).reshape(n_blocks, MOBA_BLOCK, HEAD_DIM)
            km = jnp.sum(kf, axis=1) * (1.0 / MOBA_BLOCK)
            hi = km.astype(jnp.bfloat16)
            kmh_ref[h] = hi
            kml_ref[h] = (km - hi.astype(jnp.float32)).astype(jnp.bfloat16)
            for n in range(n_blocks):
                vt_ref[h * n_blocks + n] = _transposed_values(
                    v_ref[n * ATT_BLOCK:(n + 1) * ATT_BLOCK, cols])

    qs = [q_ref[:, cols] for cols in heads]
    for h, q in enumerate(qs):
        gate = (lax.dot_general(kmh_ref[h], q, NT_DIMS, preferred_element_type=jnp.float32)
                + lax.dot_general(kml_ref[h], q, NT_DIMS, preferred_element_type=jnp.float32))
        blk = lax.broadcasted_iota(jnp.int32, gate.shape, 0)
        past = blk < qi
        g = jnp.where(past, gate, -jnp.inf)
        sel = jnp.zeros(gate.shape, jnp.float32)
        for _ in range(MOBA_TOPK):
            mx = jnp.max(g, axis=0, keepdims=True)
            first = jnp.min(jnp.where(g == mx, blk, n_blocks), axis=0, keepdims=True)
            hit = blk == first
            sel = jnp.where(hit, jnp.where(past, 1.0, 0.0), sel)
            g = jnp.where(hit, -jnp.inf, g)
        bias_ref[h] = jnp.where(sel > 0.0, 0.0, NEG_BIG)

    diag = [_scores(_key_rows(k_ref, qi, 1, cols), q) for q, cols in zip(qs, heads)]
    states = tuple(_diag_state(s, vt_ref[h * n_blocks + qi]) for h, s in enumerate(diag))

    def group(i, states):
        first = i * ATT_GROUP
        scores = [_scores(_key_rows(k_ref, first, ATT_GROUP, cols), q)
                  for q, cols in zip(qs, heads)]
        out = []
        for h, (s, st) in enumerate(zip(scores, states)):
            pieces = [s[t * ATT_BLOCK:(t + 1) * ATT_BLOCK] for t in range(ATT_GROUP)]
            biases = [bias_ref[h, pl.ds(first + t, 1), :] for t in range(ATT_GROUP)]
            vts = [vt_ref[h * n_blocks + first + t] for t in range(ATT_GROUP)]
            out.append(_update_state(pieces, biases, vts, st))
        return tuple(out)

    states = lax.fori_loop(0, lax.div(qi + (ATT_GROUP - 1), ATT_GROUP), group, states)
    for cols, (_, acc) in zip(heads, states):
        o_ref[:, cols] = _rms(_normalised(acc).T, g_ref[:, cols]).astype(o_ref.dtype)


def _moba_attention(qk, vproj, out_g, n_heads, mix_width, heads_per_step=8):
    b, s, _ = qk.shape
    n_blocks = s // MOBA_BLOCK
    hp = heads_per_step
    w = hp * HEAD_DIM
    n_steps = n_heads // hp
    return pl.pallas_call(
        functools.partial(_moba_kernel, n_blocks=n_blocks, n_heads=hp),
        out_shape=jax.ShapeDtypeStruct((b, s, mix_width), jnp.bfloat16),
        grid=(b, n_steps, n_blocks),
        in_specs=[pl.BlockSpec((None, MOBA_BLOCK, w), lambda bi, h, qi: (bi, qi, h)),
                  pl.BlockSpec((None, s, w), lambda bi, h, qi: (bi, 0, n_steps + h)),
                  pl.BlockSpec((None, s, w), lambda bi, h, qi: (bi, 0, h)),
                  pl.BlockSpec((None, 1, w), lambda bi, h, qi: (h, 0, 0))],
        out_specs=pl.BlockSpec((None, MOBA_BLOCK, w), lambda bi, h, qi: (bi, qi, h)),
        scratch_shapes=[pltpu.VMEM((hp, n_blocks, HEAD_DIM), jnp.bfloat16),
                        pltpu.VMEM((hp, n_blocks, HEAD_DIM), jnp.bfloat16),
                        pltpu.VMEM((hp * n_blocks, HEAD_DIM + VT_PAD, MOBA_BLOCK), jnp.bfloat16),
                        pltpu.VMEM((hp, n_blocks, MOBA_BLOCK), jnp.float32)],
        compiler_params=_params("parallel", "parallel", "arbitrary"),
        name="moba_attention",
    )(qk, qk, vproj, out_g.reshape(n_steps, 1, w))


def _diff_kernel(lq1_ref, lk1_ref, lq2_ref, lk2_ref, q_ref, k_ref, v_ref, g_ref, merged_ref,
                 o_ref, vt_ref, *, lambda_init, n_heads):
    del merged_ref
    qi = pl.program_id(2)
    n_blocks = vt_ref.shape[0] // n_heads
    vw = 2 * HEAD_DIM

    @pl.when(qi == 0)
    def _():
        for h in range(n_heads):
            for n in range(n_blocks):
                vt_ref[h * n_blocks + n] = _transposed_values(
                    v_ref[n * ATT_BLOCK:(n + 1) * ATT_BLOCK, h * vw:(h + 1) * vw])

    subs = [slice(j * HEAD_DIM, (j + 1) * HEAD_DIM) for j in range(2 * n_heads)]
    qs = [q_ref[:, cols] for cols in subs]
    diag = [_scores(_key_rows(k_ref, qi, 1, cols), q) for q, cols in zip(qs, subs)]
    states = tuple(_diag_state(s, vt_ref[(j // 2) * n_blocks + qi]) for j, s in enumerate(diag))

    def group(i, states):
        first = i * ATT_GROUP
        scores = [_scores(_key_rows(k_ref, first, ATT_GROUP, cols), q) for q, cols in zip(qs, subs)]
        biases = [jnp.where(first + t < qi, 0.0, NEG_BIG) for t in range(ATT_GROUP)]
        out = []
        for j, (s, st) in enumerate(zip(scores, states)):
            vts = [vt_ref[(j // 2) * n_blocks + first + t] for t in range(ATT_GROUP)]
            pieces = [s[t * ATT_BLOCK:(t + 1) * ATT_BLOCK] for t in range(ATT_GROUP)]
            out.append(_update_state(pieces, biases, vts, st))
        return tuple(out)

    states = lax.fori_loop(0, lax.div(qi + (ATT_GROUP - 1), ATT_GROUP), group, states)
    outs = [_normalised(acc) for _, acc in states]
    lam = (jnp.exp(jnp.sum(lq1_ref[...] * lk1_ref[...], axis=1, keepdims=True))
           - jnp.exp(jnp.sum(lq2_ref[...] * lk2_ref[...], axis=1, keepdims=True))
           + lambda_init)
    for h in range(n_heads):
        o = (outs[2 * h] - lam * outs[2 * h + 1]).T
        o_ref[:, h * vw:(h + 1) * vw] = (_rms(o, g_ref[...]) * (1.0 - lambda_init)).astype(o_ref.dtype)


def _diff_attention(qk, vproj, merged, lq1, lk1, lq2, lk2, subln_g, n_heads, lambda_init,
                    heads_per_step=4):
    b, s, qk_width = qk.shape
    v_col0 = out_col0 = vproj.shape[2] // 2
    col0 = qk_width // 2
    hp = heads_per_step
    w = hp * 2 * HEAD_DIM
    n_steps = n_heads // hp
    tq = ATT_BLOCK
    vec = lambda x: x.reshape(1, HEAD_DIM).astype(jnp.float32)
    vspec = pl.BlockSpec((1, HEAD_DIM), lambda bi, h, qi: (0, 0))
    qb, kb, vb, ob = col0 // w, col0 // w + n_steps, v_col0 // w, out_col0 // w
    return pl.pallas_call(
        functools.partial(_diff_kernel, lambda_init=lambda_init, n_heads=hp),
        out_shape=jax.ShapeDtypeStruct(merged.shape, merged.dtype),
        grid=(b, n_steps, s // tq),
        in_specs=[vspec, vspec, vspec, vspec,
                  pl.BlockSpec((None, tq, w), lambda bi, h, qi: (bi, qi, qb + h)),
                  pl.BlockSpec((None, s, w), lambda bi, h, qi: (bi, 0, kb + h)),
                  pl.BlockSpec((None, s, w), lambda bi, h, qi: (bi, 0, vb + h)),
                  pl.BlockSpec((1, 2 * HEAD_DIM), lambda bi, h, qi: (0, 0)),
                  pl.BlockSpec(memory_space=pl.ANY)],
        out_specs=pl.BlockSpec((None, tq, w), lambda bi, h, qi: (bi, qi, ob + h)),
        scratch_shapes=[pltpu.VMEM((hp * (s // ATT_BLOCK), 2 * HEAD_DIM + VT_PAD, ATT_BLOCK),
                                   jnp.bfloat16)],
        input_output_aliases={8: 0},
        compiler_params=_params("parallel", "parallel", "arbitrary"),
        name="diff_attention",
    )(vec(lq1), vec(lk1), vec(lq2), vec(lk2), qk, qk, vproj,
      subln_g.reshape(1, 2 * HEAD_DIM), merged)


def _cross_kernel(x_ref, gc_ref, wq_ref, kv_ref, wo_ref, gm_ref, x2_ref, hm_ref):
    c = QK_SCALE
    x = x_ref[...]
    hc = _rms(x, gc_ref[...]).astype(jnp.bfloat16)
    q = jnp.dot(hc, wq_ref[...], preferred_element_type=jnp.float32).astype(jnp.bfloat16)
    kv_w = MEM_HEADS * HEAD_DIM
    outs = []
    for h in range(MEM_HEADS):
        k = kv_ref[:, h * HEAD_DIM:(h + 1) * HEAD_DIM]
        v = kv_ref[:, kv_w + h * HEAD_DIM:kv_w + (h + 1) * HEAD_DIM]
        s = lax.dot_general(q[:, h * HEAD_DIM:(h + 1) * HEAD_DIM], k, NT_DIMS,
                            preferred_element_type=jnp.float32)
        m = jnp.max(s, axis=1, keepdims=True)
        p = jnp.exp2((s - m) * c)
        l = jnp.sum(p, axis=1, keepdims=True)
        o = jnp.dot(p.astype(jnp.bfloat16), v, preferred_element_type=jnp.float32) / l
        outs.append(o.astype(jnp.bfloat16))
    o = jnp.concatenate(outs, axis=1)
    x2 = x + jnp.dot(o, wo_ref[...], preferred_element_type=jnp.float32)
    x2_ref[...] = x2
    hm_ref[...] = _rms(x2, gm_ref[...]).astype(hm_ref.dtype)


def _cross_sublayer(x, kv, g_cross, w_cq, w_co, g_mlp, seq_len, mem_len, tq=256):
    n, d = x.shape
    qw = w_cq.shape[1]
    const = lambda i: (0, 0)
    return pl.pallas_call(
        _cross_kernel,
        out_shape=(jax.ShapeDtypeStruct((n, d), jnp.float32),
                   jax.ShapeDtypeStruct((n, d), jnp.bfloat16)),
        grid=(n // tq,),
        in_specs=[pl.BlockSpec((tq, d), lambda i: (i, 0)),
                  pl.BlockSpec((1, d), const),
                  pl.BlockSpec((d, qw), const),
                  pl.BlockSpec((mem_len, 2 * qw), lambda i: (i // (seq_len // tq), 0)),
                  pl.BlockSpec((qw, d), const),
                  pl.BlockSpec((1, d), const)],
        out_specs=(pl.BlockSpec((tq, d), lambda i: (i, 0)),
                   pl.BlockSpec((tq, d), lambda i: (i, 0))),
        compiler_params=_params("parallel"),
        name="cross_sublayer",
    )(x, g_cross.reshape(1, d), w_cq, kv, w_co, g_mlp.reshape(1, d))


def kernel(x, mem, ln_mix_g, w_in, moba_out_g, lambda_q1, lambda_k1, lambda_q2, lambda_k2,
           diff_subln_g, w_out, ln_cross_g, ln_mem_g, w_cq, w_ckv, w_co, ln_mlp_g,
           w_up, w_down, final_g):
    b, s, d = x.shape
    mem_len = mem.shape[1]
    depth = w_in.shape[0]
    mix_width = w_out.shape[1]
    moba_width = mix_width // 2
    moba_heads = moba_width // HEAD_DIM
    diff_heads = (mix_width - moba_width) // (2 * HEAD_DIM)
    bf = jnp.bfloat16

    xf = x.reshape(b * s, d)
    memf = mem.reshape(b * mem_len, d)
    for l in range(depth):
        lambda_init = 0.8 - 0.6 * math.exp(-0.3 * l)

        h = _rmsnorm(xf, ln_mix_g[l], bf)
        w_in_bf = w_in[l].astype(bf)
        qk = _inproj_qk(h, w_in_bf, s, moba_width).reshape(b, s, -1)
        bpg = moba_width // 1024
        vproj, w_out_bf, w_up_bf = _matmul(
            h, w_in_bf, bm=1024, bn=1024, bk=d, out_dtype=bf, n_out=2 * moba_width,
            w_col_block=lambda j: (2 + 3 * (j // bpg)) * bpg + j % bpg,
            round_also=(w_out, w_up), layer=l, name="inproj_v")
        vproj = vproj.reshape(b, s, -1)
        merged = _moba_attention(qk, vproj, moba_out_g[l], moba_heads, mix_width)
        merged = _diff_attention(qk, vproj, merged, lambda_q1[l], lambda_k1[l], lambda_q2[l],
                                 lambda_k2[l], diff_subln_g[l], diff_heads, lambda_init)
        x1 = _matmul(merged.reshape(b * s, mix_width), w_out_bf, bm=1024, bn=512,
                     bk=mix_width, out_dtype=jnp.float32, res=xf, name="outproj")

        mn = _rmsnorm(memf, ln_mem_g[l], bf)
        kv = _matmul(mn, w_ckv, layer=l, bm=b * mem_len, bn=512, bk=d, out_dtype=bf,
                     name="mem_kv")
        x2, hm = _cross_sublayer(x1, kv, ln_cross_g[l], w_cq[l].astype(bf), w_co[l].astype(bf),
                                 ln_mlp_g[l], s, mem_len)

        u, w_down_bf = _matmul(hm, w_up_bf, bm=1024, bn=1024, bk=d, out_dtype=bf,
                               act="relu2", round_also=(w_down,), layer=l, name="mlp_up")
        xf = _matmul(u, w_down_bf, bm=1024, bn=1024, bk=4096, out_dtype=jnp.float32,
                     res=x2, name="mlp_down")
    return _rmsnorm(xf, final_g, jnp.float32).reshape(b, s, d)
```

```python
import functools
import math

import jax
import jax.numpy as jnp
from jax import lax
from jax.experimental import pallas as pl
from jax.experimental.pallas import tpu as pltpu

HEAD_DIM = 128
MOBA_BLOCK = 256
MOBA_TOPK = 3
ROT_DIM = HEAD_DIM // 4
ROPE_THETA = 500000.0
MEM_HEADS = 4
EPS = 1e-5

VMEM_LIMIT_BYTES = 56 * 2**20
NEG_BIG = -1e30
LOG2E = 1.4426950408889634
QK_SCALE = HEAD_DIM ** -0.5 * LOG2E
NT_DIMS = (((1,), (1,)), ((), ()))
ATT_BLOCK = 256
ATT_GROUP = 2
MM_CHUNK = 256


def _params(*sem):
    return pltpu.CompilerParams(dimension_semantics=sem, vmem_limit_bytes=VMEM_LIMIT_BYTES)


def _rms(x, g):
    return x * lax.rsqrt(jnp.mean(x * x, axis=-1, keepdims=True) + EPS) * g


def _rmsnorm_kernel(x_ref, g_ref, o_ref):
    o_ref[...] = _rms(x_ref[...], g_ref[...]).astype(o_ref.dtype)


def _rmsnorm(x, g, out_dtype, bm=256):
    n, d = x.shape
    return pl.pallas_call(
        _rmsnorm_kernel,
        out_shape=jax.ShapeDtypeStruct((n, d), out_dtype),
        grid=(n // bm,),
        in_specs=[pl.BlockSpec((bm, d), lambda i: (i, 0)),
                  pl.BlockSpec((1, d), lambda i: (0, 0))],
        out_specs=pl.BlockSpec((bm, d), lambda i: (i, 0)),
        compiler_params=_params("parallel"),
        name="rmsnorm",
    )(x, g.reshape(1, d))


def _weight_cols(w_ref, cols):
    return w_ref[:, cols].astype(jnp.bfloat16)


def _weight_spec(w, layer, bk, bn, index):
    if w.ndim == 2:
        return pl.BlockSpec((bk, bn), index)
    return pl.BlockSpec((None, bk, bn), lambda *g: (layer,) + tuple(index(*g)))


def _mm_kernel(*refs, nk, act, has_res, n_side):
    a_ref, w_ref = refs[0], refs[1]
    res_ref = refs[2] if has_res else None
    side_in = refs[2 + has_res:2 + has_res + n_side]
    o_ref = refs[2 + has_res + n_side]
    side_out = refs[3 + has_res + n_side:]
    for src, dst in zip(side_in, side_out):
        dst[...] = src[...].astype(jnp.bfloat16)
    if nk > 1:
        assert act is None and o_ref.dtype == jnp.float32

        @pl.when(pl.program_id(2) == 0)
        def _():
            o_ref[...] = res_ref[...] if has_res else jnp.zeros_like(o_ref)

    for c in range(o_ref.shape[1] // MM_CHUNK):
        cols = slice(c * MM_CHUNK, (c + 1) * MM_CHUNK)
        acc = jnp.dot(a_ref[...], _weight_cols(w_ref, cols), preferred_element_type=jnp.float32)
        if nk > 1:
            o_ref[:, cols] = acc + o_ref[:, cols]
        else:
            if act == "relu2":
                acc = jnp.square(jnp.maximum(acc, 0.0))
            if has_res:
                acc = acc + res_ref[:, cols]
            o_ref[:, cols] = acc.astype(o_ref.dtype)


def _matmul(a, w, *, bm, bn, bk, out_dtype, act=None, res=None, layer=0, n_out=None,
            w_col_block=lambda j: j, round_also=(), name):
    m, kdim = a.shape
    n = w.shape[-1] if n_out is None else n_out
    nk = kdim // bk
    nj = n // bn
    in_specs = [pl.BlockSpec((bm, bk), lambda i, j, k: (i, k)),
                _weight_spec(w, layer, bk, bn, lambda i, j, k: (k, w_col_block(j)))]
    args = [a, w]
    if res is not None:
        in_specs.append(pl.BlockSpec((bm, bn), lambda i, j, k: (i, j)))
        args.append(res)
    out_shape = [jax.ShapeDtypeStruct((m, n), out_dtype)]
    out_specs = [pl.BlockSpec((bm, bn), lambda i, j, k: (i, j))]
    step = lambda i, j, k: (i * nj + j) * nk + k
    for side in round_also:
        _, rows, cols = side.shape
        slab = rows // ((m // bm) * nj * nk)
        in_specs.append(pl.BlockSpec((None, slab, cols), lambda i, j, k: (layer, step(i, j, k), 0)))
        args.append(side)
        out_shape.append(jax.ShapeDtypeStruct((rows, cols), jnp.bfloat16))
        out_specs.append(pl.BlockSpec((slab, cols), lambda i, j, k: (step(i, j, k), 0)))
    out = pl.pallas_call(
        functools.partial(_mm_kernel, nk=nk, act=act, has_res=res is not None,
                          n_side=len(round_also)),
        out_shape=out_shape,
        grid=(m // bm, nj, nk),
        in_specs=in_specs,
        out_specs=out_specs,
        compiler_params=_params("parallel", "parallel", "arbitrary"),
        name=name,
    )(*args)
    return out if round_also else out[0]


def _inproj_kernel(h_ref, w_ref, cos_ref, sa_ref, sb_ref, o_ref, acc_ref):
    @pl.when(pl.program_id(0) == 0)
    def _():
        acc_ref[...] = jnp.zeros_like(acc_ref)

    half_rows = acc_ref.shape[0] // 2
    for r0 in (0, half_rows):
        rows = slice(r0, r0 + half_rows)
        cos, sa, sb = cos_ref[rows, :], sa_ref[rows, :], sb_ref[rows, :]
        for c in range(o_ref.shape[1] // HEAD_DIM):
            cols = slice(c * HEAD_DIM, (c + 1) * HEAD_DIM)
            a = acc_ref[rows, cols]
            r = (a * cos + pltpu.roll(a, HEAD_DIM - ROT_DIM // 2, 1) * sa
                 + pltpu.roll(a, ROT_DIM // 2, 1) * sb)
            o_ref[rows, cols] = r.astype(o_ref.dtype)
    acc_ref[...] = jnp.dot(h_ref[...], w_ref[...], preferred_element_type=jnp.float32)


def _rope_tables(seq_len):
    half = ROT_DIM // 2
    pos = jnp.arange(seq_len, dtype=jnp.float32)
    inv_freq = ROPE_THETA ** (-jnp.arange(0, ROT_DIM, 2, dtype=jnp.float32) / ROT_DIM)
    ang = pos[:, None] * inv_freq[None, :]
    cos, sin = jnp.cos(ang), jnp.sin(ang)
    z = lambda w: jnp.zeros((seq_len, w), jnp.float32)
    cos_t = jnp.concatenate([cos, cos, jnp.ones((seq_len, HEAD_DIM - ROT_DIM), jnp.float32)], axis=1)
    sin_a = jnp.concatenate([-sin, z(HEAD_DIM - half)], axis=1)
    sin_b = jnp.concatenate([z(half), sin, z(HEAD_DIM - ROT_DIM)], axis=1)
    return cos_t, sin_a, sin_b


def _inproj_qk(h, w, seq_len, group_width, bm=1024, bn=1024):
    m, kdim = h.shape
    bpg = group_width // bn
    nj = 4 * bpg
    n_blocks = (m // bm) * nj
    cos_t, sin_a, sin_b = (jnp.stack([t * QK_SCALE, t]) for t in _rope_tables(seq_len))

    mul = lambda t: jnp.minimum(t, n_blocks - 1)
    fin = lambda t: jnp.maximum(t - 1, 0)

    def w_block(t):
        out_group = (mul(t) % nj) // bpg
        return 0, (out_group + out_group // 2) * bpg + mul(t) % bpg

    tspec = pl.BlockSpec(
        (None, bm, HEAD_DIM),
        lambda t: (((fin(t) % nj) // bpg) % 2, (fin(t) // nj) % (seq_len // bm), 0))
    return pl.pallas_call(
        _inproj_kernel,
        out_shape=jax.ShapeDtypeStruct((m, 4 * group_width), jnp.bfloat16),
        grid=(n_blocks + 1,),
        in_specs=[pl.BlockSpec((bm, kdim), lambda t: (mul(t) // nj, 0)),
                  pl.BlockSpec((kdim, bn), w_block),
                  tspec, tspec, tspec],
        out_specs=pl.BlockSpec((bm, bn), lambda t: (fin(t) // nj, fin(t) % nj)),
        scratch_shapes=[pltpu.VMEM((bm, bn), jnp.float32)],
        compiler_params=_params("arbitrary"),
        name="inproj_qk_rope",
    )(h, w, cos_t, sin_a, sin_b)


VT_PAD = 16


def _scores(k, q):
    return lax.dot_general(k, q, NT_DIMS, preferred_element_type=jnp.float32)


def _transposed_values(v_blk):
    vt = v_blk.astype(jnp.float32).T.astype(jnp.bfloat16)
    row = lax.broadcasted_iota(jnp.int32, (VT_PAD, v_blk.shape[0]), 0)
    return jnp.concatenate([vt, jnp.where(row == 0, 1.0, 0.0).astype(jnp.bfloat16)], axis=0)


def _weights(x):
    return jnp.exp2(x).astype(jnp.bfloat16)


def _diag_state(s, vt):
    key = lax.broadcasted_iota(jnp.int32, s.shape, 0)
    qry = lax.broadcasted_iota(jnp.int32, s.shape, 1)
    s = jnp.where(key <= qry, s, NEG_BIG)
    m = jnp.max(s, axis=0, keepdims=True)
    return m, jnp.dot(vt, _weights(s - m), preferred_element_type=jnp.float32)


def _normalised(acc):
    v_dim = acc.shape[0] - VT_PAD
    return acc[:v_dim] / acc[v_dim:v_dim + 1]


def _update_state(pieces, biases, vts, state):
    m, acc = state
    m_new = m
    for s, b in zip(pieces, biases):
        m_new = jnp.maximum(m_new, jnp.max(s, axis=0, keepdims=True) + b)
    acc = jnp.exp2(m - m_new) * acc
    for s, b, vt in zip(pieces, biases, vts):
        acc = acc + jnp.dot(vt, _weights(s - (m_new - b)), preferred_element_type=jnp.float32)
    return m_new, acc


class _LaggedGrid:
    def __init__(self, n_steps, n_groups, n_blocks):
        self.n_steps, self.n_groups, self.n_blocks = n_steps, n_groups, n_blocks

    def _decode(self, item, index):
        per_batch = self.n_groups * self.n_blocks
        return index(item // per_batch, (item // self.n_blocks) % self.n_groups,
                     item % self.n_blocks)

    def cur(self, index):
        return lambda t: self._decode(jnp.minimum(t, self.n_steps - 1), index)

    def prev(self, index):
        return lambda t: self._decode(jnp.maximum(t - 1, 0), index)


def _key_rows(k_ref, first_block, n_blocks, cols=slice(None)):
    rows = n_blocks * ATT_BLOCK
    return k_ref[pl.ds(pl.multiple_of(first_block * ATT_BLOCK, ATT_BLOCK), rows), cols]


def _moba_kernel(q_ref, k_ref, v_ref, g_ref, o_ref, kmh_ref, kml_ref, vt_ref, bias_ref, fin_ref,
                 *, n_blocks, n_heads, n_steps):
    t = pl.program_id(0)
    qi = lax.rem(jnp.minimum(t, n_steps - 1), n_blocks)
    heads = [slice(h * HEAD_DIM, (h + 1) * HEAD_DIM) for h in range(n_heads)]

    @pl.when(t == 0)
    def _():
        fin_ref[...] = jnp.ones_like(fin_ref)

    @pl.when(qi == 0)
    def _():
        for h, cols in enumerate(heads):
            kf = k_ref[:, cols].astype(jnp.float32).reshape(n_blocks, MOBA_BLOCK, HEAD_DIM)
            km = jnp.sum(kf, axis=1) * (1.0 / MOBA_BLOCK)
            hi = km.astype(jnp.bfloat16)
            kmh_ref[h] = hi
            kml_ref[h] = (km - hi.astype(jnp.float32)).astype(jnp.bfloat16)
            for n in range(n_blocks):
                vt_ref[h * n_blocks + n] = _transposed_values(
                    v_ref[n * ATT_BLOCK:(n + 1) * ATT_BLOCK, cols])

    for h, cols in enumerate(heads):
        o_ref[:, cols] = _rms(_normalised(fin_ref[h]).T, g_ref[:, cols]).astype(o_ref.dtype)

    qs = [q_ref[:, cols] for cols in heads]
    for h, q in enumerate(qs):
        gate = (lax.dot_general(kmh_ref[h], q, NT_DIMS, preferred_element_type=jnp.float32)
                + lax.dot_general(kml_ref[h], q, NT_DIMS, preferred_element_type=jnp.float32))
        blk = lax.broadcasted_iota(jnp.int32, gate.shape, 0)
        past = blk < qi
        g = jnp.where(past, gate, -jnp.inf)
        sel = jnp.zeros(gate.shape, jnp.float32)
        for _ in range(MOBA_TOPK):
            mx = jnp.max(g, axis=0, keepdims=True)
            first = jnp.min(jnp.where(g == mx, blk, n_blocks), axis=0, keepdims=True)
            hit = blk == first
            sel = jnp.where(hit, jnp.where(past, 1.0, 0.0), sel)
            g = jnp.where(hit, -jnp.inf, g)
        bias_ref[h] = jnp.where(sel > 0.0, 0.0, NEG_BIG)

    diag = [_scores(_key_rows(k_ref, qi, 1, cols), q) for q, cols in zip(qs, heads)]
    states = tuple(_diag_state(s, vt_ref[h * n_blocks + qi]) for h, s in enumerate(diag))

    def group(i, states):
        first = i * ATT_GROUP
        scores = [_scores(_key_rows(k_ref, first, ATT_GROUP, cols), q)
                  for q, cols in zip(qs, heads)]
        out = []
        for h, (s, st) in enumerate(zip(scores, states)):
            pieces = [s[t * ATT_BLOCK:(t + 1) * ATT_BLOCK] for t in range(ATT_GROUP)]
            biases = [bias_ref[h, pl.ds(first + t, 1), :] for t in range(ATT_GROUP)]
            vts = [vt_ref[h * n_blocks + first + t] for t in range(ATT_GROUP)]
            out.append(_update_state(pieces, biases, vts, st))
        return tuple(out)

    states = lax.fori_loop(0, lax.div(qi + (ATT_GROUP - 1), ATT_GROUP), group, states)
    for h, (_, acc) in enumerate(states):
        fin_ref[h] = acc


def _moba_attention(qk, vproj, out_g, n_heads, mix_width, heads_per_step=8):
    b, s, _ = qk.shape
    n_blocks = s // MOBA_BLOCK
    hp = heads_per_step
    w = hp * HEAD_DIM
    n_groups = n_heads // hp
    n_steps = b * n_groups * n_blocks
    grid = _LaggedGrid(n_steps, n_groups, n_blocks)
    return pl.pallas_call(
        functools.partial(_moba_kernel, n_blocks=n_blocks, n_heads=hp, n_steps=n_steps),
        out_shape=jax.ShapeDtypeStruct((b, s, mix_width), jnp.bfloat16),
        grid=(n_steps + 1,),
        in_specs=[pl.BlockSpec((None, MOBA_BLOCK, w), grid.cur(lambda bi, h, qi: (bi, qi, h))),
                  pl.BlockSpec((None, s, w), grid.cur(lambda bi, h, qi: (bi, 0, n_groups + h))),
                  pl.BlockSpec((None, s, w), grid.cur(lambda bi, h, qi: (bi, 0, h))),
                  pl.BlockSpec((None, 1, w), grid.prev(lambda bi, h, qi: (h, 0, 0)))],
        out_specs=pl.BlockSpec((None, MOBA_BLOCK, w), grid.prev(lambda bi, h, qi: (bi, qi, h))),
        scratch_shapes=[pltpu.VMEM((hp, n_blocks, HEAD_DIM), jnp.bfloat16),
                        pltpu.VMEM((hp, n_blocks, HEAD_DIM), jnp.bfloat16),
                        pltpu.VMEM((hp * n_blocks, HEAD_DIM + VT_PAD, MOBA_BLOCK), jnp.bfloat16),
                        pltpu.VMEM((hp, n_blocks, MOBA_BLOCK), jnp.float32),
                        pltpu.VMEM((hp, HEAD_DIM + VT_PAD, MOBA_BLOCK), jnp.float32)],
        compiler_params=_params("arbitrary"),
        name="moba_attention",
    )(qk, qk, vproj, out_g.reshape(n_groups, 1, w))


def _diff_kernel(lq1_ref, lk1_ref, lq2_ref, lk2_ref, q_ref, k_ref, v_ref, g_ref, merged_ref,
                 o_ref, vt_ref, fin_ref, *, lambda_init, n_heads, n_steps):
    del merged_ref
    t = pl.program_id(0)
    n_blocks = vt_ref.shape[0] // n_heads
    qi = lax.rem(jnp.minimum(t, n_steps - 1), n_blocks)
    vw = 2 * HEAD_DIM

    @pl.when(t == 0)
    def _():
        fin_ref[...] = jnp.ones_like(fin_ref)

    @pl.when(qi == 0)
    def _():
        for h in range(n_heads):
            for n in range(n_blocks):
                vt_ref[h * n_blocks + n] = _transposed_values(
                    v_ref[n * ATT_BLOCK:(n + 1) * ATT_BLOCK, h * vw:(h + 1) * vw])

    lam = (jnp.exp(jnp.sum(lq1_ref[...] * lk1_ref[...], axis=1, keepdims=True))
           - jnp.exp(jnp.sum(lq2_ref[...] * lk2_ref[...], axis=1, keepdims=True))
           + lambda_init)
    for h in range(n_heads):
        o = (_normalised(fin_ref[2 * h]) - lam * _normalised(fin_ref[2 * h + 1])).T
        o_ref[:, h * vw:(h + 1) * vw] = (_rms(o, g_ref[...]) * (1.0 - lambda_init)).astype(o_ref.dtype)

    subs = [slice(j * HEAD_DIM, (j + 1) * HEAD_DIM) for j in range(2 * n_heads)]
    qs = [q_ref[:, cols] for cols in subs]
    diag = [_scores(_key_rows(k_ref, qi, 1, cols), q) for q, cols in zip(qs, subs)]
    states = tuple(_diag_state(s, vt_ref[(j // 2) * n_blocks + qi]) for j, s in enumerate(diag))

    def group(i, states):
        first = i * ATT_GROUP
        scores = [_scores(_key_rows(k_ref, first, ATT_GROUP, cols), q) for q, cols in zip(qs, subs)]
        biases = [jnp.where(first + t < qi, 0.0, NEG_BIG) for t in range(ATT_GROUP)]
        out = []
        for j, (s, st) in enumerate(zip(scores, states)):
            vts = [vt_ref[(j // 2) * n_blocks + first + t] for t in range(ATT_GROUP)]
            pieces = [s[t * ATT_BLOCK:(t + 1) * ATT_BLOCK] for t in range(ATT_GROUP)]
            out.append(_update_state(pieces, biases, vts, st))
        return tuple(out)

    states = lax.fori_loop(0, lax.div(qi + (ATT_GROUP - 1), ATT_GROUP), group, states)
    for j, (_, acc) in enumerate(states):
        fin_ref[j] = acc


def _diff_attention(qk, vproj, merged, lq1, lk1, lq2, lk2, subln_g, n_heads, lambda_init,
                    heads_per_step=4):
    b, s, qk_width = qk.shape
    v_col0 = out_col0 = vproj.shape[2] // 2
    col0 = qk_width // 2
    hp = heads_per_step
    w = hp * 2 * HEAD_DIM
    n_groups = n_heads // hp
    tq = ATT_BLOCK
    n_steps = b * n_groups * (s // tq)
    grid = _LaggedGrid(n_steps, n_groups, s // tq)
    vec = lambda x: x.reshape(1, HEAD_DIM).astype(jnp.float32)
    vspec = pl.BlockSpec((1, HEAD_DIM), lambda t: (0, 0))
    qb, kb, vb, ob = col0 // w, col0 // w + n_groups, v_col0 // w, out_col0 // w
    return pl.pallas_call(
        functools.partial(_diff_kernel, lambda_init=lambda_init, n_heads=hp, n_steps=n_steps),
        out_shape=jax.ShapeDtypeStruct(merged.shape, merged.dtype),
        grid=(n_steps + 1,),
        in_specs=[vspec, vspec, vspec, vspec,
                  pl.BlockSpec((None, tq, w), grid.cur(lambda bi, h, qi: (bi, qi, qb + h))),
                  pl.BlockSpec((None, s, w), grid.cur(lambda bi, h, qi: (bi, 0, kb + h))),
                  pl.BlockSpec((None, s, w), grid.cur(lambda bi, h, qi: (bi, 0, vb + h))),
                  pl.BlockSpec((1, 2 * HEAD_DIM), lambda t: (0, 0)),
                  pl.BlockSpec(memory_space=pl.ANY)],
        out_specs=pl.BlockSpec((None, tq, w), grid.prev(lambda bi, h, qi: (bi, qi, ob + h))),
        scratch_shapes=[pltpu.VMEM((hp * (s // ATT_BLOCK), 2 * HEAD_DIM + VT_PAD, ATT_BLOCK),
                                   jnp.bfloat16),
                        pltpu.VMEM((2 * hp, 2 * HEAD_DIM + VT_PAD, ATT_BLOCK), jnp.float32)],
        input_output_aliases={8: 0},
        compiler_params=_params("arbitrary"),
        name="diff_attention",
    )(vec(lq1), vec(lk1), vec(lq2), vec(lk2), qk, qk, vproj,
      subln_g.reshape(1, 2 * HEAD_DIM), merged)


def _cross_kernel(x_ref, gc_ref, wq_ref, kv_ref, wo_ref, gm_ref, x2_ref, hm_ref):
    c = QK_SCALE
    x = x_ref[...]
    hc = _rms(x, gc_ref[...]).astype(jnp.bfloat16)
    q = jnp.dot(hc, wq_ref[...], preferred_element_type=jnp.float32).astype(jnp.bfloat16)
    kv_w = MEM_HEADS * HEAD_DIM
    outs = []
    for h in range(MEM_HEADS):
        k = kv_ref[:, h * HEAD_DIM:(h + 1) * HEAD_DIM]
        v = kv_ref[:, kv_w + h * HEAD_DIM:kv_w + (h + 1) * HEAD_DIM]
        s = lax.dot_general(q[:, h * HEAD_DIM:(h + 1) * HEAD_DIM], k, NT_DIMS,
                            preferred_element_type=jnp.float32)
        m = jnp.max(s, axis=1, keepdims=True)
        p = jnp.exp2((s - m) * c)
        l = jnp.sum(p, axis=1, keepdims=True)
        o = jnp.dot(p.astype(jnp.bfloat16), v, preferred_element_type=jnp.float32) / l
        outs.append(o.astype(jnp.bfloat16))
    o = jnp.concatenate(outs, axis=1)
    x2 = x + jnp.dot(o, wo_ref[...], preferred_element_type=jnp.float32)
    x2_ref[...] = x2
    hm_ref[...] = _rms(x2, gm_ref[...]).astype(hm_ref.dtype)


def _cross_sublayer(x, kv, g_cross, w_cq, w_co, g_mlp, seq_len, mem_len, tq=256):
    n, d = x.shape
    qw = w_cq.shape[1]
    const = lambda i: (0, 0)
    return pl.pallas_call(
        _cross_kernel,
        out_shape=(jax.ShapeDtypeStruct((n, d), jnp.float32),
                   jax.ShapeDtypeStruct((n, d), jnp.bfloat16)),
        grid=(n // tq,),
        in_specs=[pl.BlockSpec((tq, d), lambda i: (i, 0)),
                  pl.BlockSpec((1, d), const),
                  pl.BlockSpec((d, qw), const),
                  pl.BlockSpec((mem_len, 2 * qw), lambda i: (i // (seq_len // tq), 0)),
                  pl.BlockSpec((qw, d), const),
                  pl.BlockSpec((1, d), const)],
        out_specs=(pl.BlockSpec((tq, d), lambda i: (i, 0)),
                   pl.BlockSpec((tq, d), lambda i: (i, 0))),
        compiler_params=_params("parallel"),
        name="cross_sublayer",
    )(x, g_cross.reshape(1, d), w_cq, kv, w_co, g_mlp.reshape(1, d))


def kernel(x, mem, ln_mix_g, w_in, moba_out_g, lambda_q1, lambda_k1, lambda_q2, lambda_k2,
           diff_subln_g, w_out, ln_cross_g, ln_mem_g, w_cq, w_ckv, w_co, ln_mlp_g,
           w_up, w_down, final_g):
    b, s, d = x.shape
    mem_len = mem.shape[1]
    depth = w_in.shape[0]
    mix_width = w_out.shape[1]
    moba_width = mix_width // 2
    moba_heads = moba_width // HEAD_DIM
    diff_heads = (mix_width - moba_width) // (2 * HEAD_DIM)
    bf = jnp.bfloat16

    xf = x.reshape(b * s, d)
    memf = mem.reshape(b * mem_len, d)
    for l in range(depth):
        lambda_init = 0.8 - 0.6 * math.exp(-0.3 * l)

        h = _rmsnorm(xf, ln_mix_g[l], bf)
        w_in_bf = w_in[l].astype(bf)
        qk = _inproj_qk(h, w_in_bf, s, moba_width).reshape(b, s, -1)
        bpg = moba_width // 1024
        vproj, w_out_bf, w_up_bf = _matmul(
            h, w_in_bf, bm=1024, bn=1024, bk=d, out_dtype=bf, n_out=2 * moba_width,
            w_col_block=lambda j: (2 + 3 * (j // bpg)) * bpg + j % bpg,
            round_also=(w_out, w_up), layer=l, name="inproj_v")
        vproj = vproj.reshape(b, s, -1)
        merged = _moba_attention(qk, vproj, moba_out_g[l], moba_heads, mix_width)
        merged = _diff_attention(qk, vproj, merged, lambda_q1[l], lambda_k1[l], lambda_q2[l],
                                 lambda_k2[l], diff_subln_g[l], diff_heads, lambda_init)
        x1 = _matmul(merged.reshape(b * s, mix_width), w_out_bf, bm=1024, bn=512,
                     bk=mix_width, out_dtype=jnp.float32, res=xf, name="outproj")

        mn = _rmsnorm(memf, ln_mem_g[l], bf)
        kv = _matmul(mn, w_ckv, layer=l, bm=b * mem_len, bn=512, bk=d, out_dtype=bf,
                     name="mem_kv")
        x2, hm = _cross_sublayer(x1, kv, ln_cross_g[l], w_cq[l].astype(bf), w_co[l].astype(bf),
                                 ln_mlp_g[l], s, mem_len)

        u, w_down_bf = _matmul(hm, w_up_bf, bm=1024, bn=1024, bk=d, out_dtype=bf,
                               act="relu2", round_also=(w_down,), layer=l, name="mlp_up")
        xf = _matmul(u, w_down_bf, bm=1024, bn=1024, bk=4096, out_dtype=jnp.float32,
                     res=x2, name="mlp_down")
    return _rmsnorm(xf, final_g, jnp.float32).reshape(b, s, d)
```

```python
import functools
import math

import jax
import jax.numpy as jnp
from jax import lax
from jax.experimental import pallas as pl
from jax.experimental.pallas import tpu as pltpu

HEAD_DIM = 128
MOBA_BLOCK = 256
MOBA_TOPK = 3
ROT_DIM = HEAD_DIM // 4
ROPE_THETA = 500000.0
MEM_HEADS = 4
EPS = 1e-5

VMEM_LIMIT_BYTES = 60 * 2**20
NEG_BIG = -1e30
LOG2E = 1.4426950408889634
QK_SCALE = HEAD_DIM ** -0.5 * LOG2E
NT_DIMS = (((1,), (1,)), ((), ()))
ATT_BLOCK = 256
ATT_GROUP = 2
MM_CHUNK = 256


def _params(*sem):
    return pltpu.CompilerParams(dimension_semantics=sem, vmem_limit_bytes=VMEM_LIMIT_BYTES)


def _rms(x, g):
    return x * lax.rsqrt(jnp.mean(x * x, axis=-1, keepdims=True) + EPS) * g


def _rmsnorm_kernel(x_ref, g_ref, o_ref):
    o_ref[...] = _rms(x_ref[...], g_ref[...]).astype(o_ref.dtype)


def _rmsnorm(x, g, out_dtype, bm=256):
    n, d = x.shape
    return pl.pallas_call(
        _rmsnorm_kernel,
        out_shape=jax.ShapeDtypeStruct((n, d), out_dtype),
        grid=(n // bm,),
        in_specs=[pl.BlockSpec((bm, d), lambda i: (i, 0)),
                  pl.BlockSpec((1, d), lambda i: (0, 0))],
        out_specs=pl.BlockSpec((bm, d), lambda i: (i, 0)),
        compiler_params=_params("parallel"),
        name="rmsnorm",
    )(x, g.reshape(1, d))


def _weight_spec(w, layer, bk, bn, index):
    if w.ndim == 2:
        return pl.BlockSpec((bk, bn), index)
    return pl.BlockSpec((None, bk, bn), lambda *g: (layer,) + tuple(index(*g)))


def _mm_kernel(*refs, act, has_res, n_side):
    a_ref, w_ref = refs[0], refs[1]
    res_ref = refs[2] if has_res else None
    side_in = refs[2 + has_res:2 + has_res + n_side]
    o_ref = refs[2 + has_res + n_side]
    side_out = refs[3 + has_res + n_side:]
    for src, dst in zip(side_in, side_out):
        dst[...] = src[...].astype(jnp.bfloat16)
    for c in range(o_ref.shape[1] // MM_CHUNK):
        cols = slice(c * MM_CHUNK, (c + 1) * MM_CHUNK)
        acc = jnp.dot(a_ref[...], w_ref[:, cols].astype(jnp.bfloat16),
                      preferred_element_type=jnp.float32)
        if act == "relu2":
            acc = jnp.square(jnp.maximum(acc, 0.0))
        if has_res:
            acc = acc + res_ref[:, cols]
        o_ref[:, cols] = acc.astype(o_ref.dtype)


def _matmul(a, w, *, bm, bn, out_dtype, act=None, res=None, layer=0, n_out=None,
            w_col_block=lambda j: j, round_also=(), name):
    m, kdim = a.shape
    n = w.shape[-1] if n_out is None else n_out
    nj = n // bn
    in_specs = [pl.BlockSpec((bm, kdim), lambda i, j: (i, 0)),
                _weight_spec(w, layer, kdim, bn, lambda i, j: (0, w_col_block(j)))]
    args = [a, w]
    if res is not None:
        in_specs.append(pl.BlockSpec((bm, bn), lambda i, j: (i, j)))
        args.append(res)
    out_shape = [jax.ShapeDtypeStruct((m, n), out_dtype)]
    out_specs = [pl.BlockSpec((bm, bn), lambda i, j: (i, j))]
    for side in round_also:
        _, rows, cols = side.shape
        slab = rows // ((m // bm) * nj)
        in_specs.append(pl.BlockSpec((None, slab, cols), lambda i, j: (layer, i * nj + j, 0)))
        args.append(side)
        out_shape.append(jax.ShapeDtypeStruct((rows, cols), jnp.bfloat16))
        out_specs.append(pl.BlockSpec((slab, cols), lambda i, j: (i * nj + j, 0)))
    out = pl.pallas_call(
        functools.partial(_mm_kernel, act=act, has_res=res is not None, n_side=len(round_also)),
        out_shape=out_shape,
        grid=(m // bm, nj),
        in_specs=in_specs,
        out_specs=out_specs,
        compiler_params=_params("parallel", "parallel"),
        name=name,
    )(*args)
    return out if round_also else out[0]


def _mm_ksplit_kernel(a_ref, w_ref, res_ref, o_ref):
    @pl.when(pl.program_id(2) == 0)
    def _():
        o_ref[...] = res_ref[...]

    for c in range(o_ref.shape[1] // MM_CHUNK):
        cols = slice(c * MM_CHUNK, (c + 1) * MM_CHUNK)
        o_ref[:, cols] = (jnp.dot(a_ref[...], w_ref[:, cols], preferred_element_type=jnp.float32)
                          + o_ref[:, cols])


def _matmul_ksplit(a, w, res, *, bm, bn, bk, name):
    m, kdim = a.shape
    n = w.shape[1]
    return pl.pallas_call(
        _mm_ksplit_kernel,
        out_shape=jax.ShapeDtypeStruct((m, n), jnp.float32),
        grid=(m // bm, n // bn, kdim // bk),
        in_specs=[pl.BlockSpec((bm, bk), lambda i, j, k: (i, k)),
                  pl.BlockSpec((bk, bn), lambda i, j, k: (k, j)),
                  pl.BlockSpec((bm, bn), lambda i, j, k: (i, j))],
        out_specs=pl.BlockSpec((bm, bn), lambda i, j, k: (i, j)),
        compiler_params=_params("parallel", "parallel", "arbitrary"),
        name=name,
    )(a, w, res)


def _inproj_kernel(h_ref, w_ref, cos_ref, sa_ref, sb_ref, o_ref, acc_ref):
    @pl.when(pl.program_id(0) == 0)
    def _():
        acc_ref[...] = jnp.zeros_like(acc_ref)

    half_rows = acc_ref.shape[0] // 2
    for r0 in (0, half_rows):
        rows = slice(r0, r0 + half_rows)
        cos, sa, sb = cos_ref[rows, :], sa_ref[rows, :], sb_ref[rows, :]
        for c in range(o_ref.shape[1] // HEAD_DIM):
            cols = slice(c * HEAD_DIM, (c + 1) * HEAD_DIM)
            a = acc_ref[rows, cols]
            r = (a * cos + pltpu.roll(a, HEAD_DIM - ROT_DIM // 2, 1) * sa
                 + pltpu.roll(a, ROT_DIM // 2, 1) * sb)
            o_ref[rows, cols] = r.astype(o_ref.dtype)
    acc_ref[...] = jnp.dot(h_ref[...], w_ref[...], preferred_element_type=jnp.float32)


def _rope_tables(seq_len):
    half = ROT_DIM // 2
    pos = jnp.arange(seq_len, dtype=jnp.float32)
    inv_freq = ROPE_THETA ** (-jnp.arange(0, ROT_DIM, 2, dtype=jnp.float32) / ROT_DIM)
    ang = pos[:, None] * inv_freq[None, :]
    cos, sin = jnp.cos(ang), jnp.sin(ang)
    z = lambda w: jnp.zeros((seq_len, w), jnp.float32)
    cos_t = jnp.concatenate([cos, cos, jnp.ones((seq_len, HEAD_DIM - ROT_DIM), jnp.float32)], axis=1)
    sin_a = jnp.concatenate([-sin, z(HEAD_DIM - half)], axis=1)
    sin_b = jnp.concatenate([z(half), sin, z(HEAD_DIM - ROT_DIM)], axis=1)
    return cos_t, sin_a, sin_b


def _inproj_qk(h, w, seq_len, group_width, bm=1024, bn=1024):
    m, kdim = h.shape
    bpg = group_width // bn
    nj = 4 * bpg
    n_blocks = (m // bm) * nj
    cos_t, sin_a, sin_b = (jnp.stack([t * QK_SCALE, t]) for t in _rope_tables(seq_len))

    mul = lambda t: jnp.minimum(t, n_blocks - 1)
    fin = lambda t: jnp.maximum(t - 1, 0)

    def w_block(t):
        out_group = (mul(t) % nj) // bpg
        return 0, (out_group + out_group // 2) * bpg + mul(t) % bpg

    tspec = pl.BlockSpec(
        (None, bm, HEAD_DIM),
        lambda t: (((fin(t) % nj) // bpg) % 2, (fin(t) // nj) % (seq_len // bm), 0))
    return pl.pallas_call(
        _inproj_kernel,
        out_shape=jax.ShapeDtypeStruct((m, 4 * group_width), jnp.bfloat16),
        grid=(n_blocks + 1,),
        in_specs=[pl.BlockSpec((bm, kdim), lambda t: (mul(t) // nj, 0)),
                  pl.BlockSpec((kdim, bn), w_block),
                  tspec, tspec, tspec],
        out_specs=pl.BlockSpec((bm, bn), lambda t: (fin(t) // nj, fin(t) % nj)),
        scratch_shapes=[pltpu.VMEM((bm, bn), jnp.float32)],
        compiler_params=_params("arbitrary"),
        name="inproj_qk_rope",
    )(h, w, cos_t, sin_a, sin_b)


VT_PAD = 16


def _scores(k, q):
    return lax.dot_general(k, q, NT_DIMS, preferred_element_type=jnp.float32)


def _transposed_values(v_blk):
    vt = v_blk.astype(jnp.float32).T.astype(jnp.bfloat16)
    row = lax.broadcasted_iota(jnp.int32, (VT_PAD, v_blk.shape[0]), 0)
    return jnp.concatenate([vt, jnp.where(row == 0, 1.0, 0.0).astype(jnp.bfloat16)], axis=0)


def _weights(x):
    return jnp.exp2(x).astype(jnp.bfloat16)


def _diag_state(s, vt):
    key = lax.broadcasted_iota(jnp.int32, s.shape, 0)
    qry = lax.broadcasted_iota(jnp.int32, s.shape, 1)
    s = jnp.where(key <= qry, s, NEG_BIG)
    m = jnp.max(s, axis=0, keepdims=True)
    return m, jnp.dot(vt, _weights(s - m), preferred_element_type=jnp.float32)


def _normalised(acc):
    v_dim = acc.shape[0] - VT_PAD
    return acc[:v_dim] / acc[v_dim:v_dim + 1]


def _update_state(pieces, biases, vts, state):
    m, acc = state
    m_new = m
    for s, b in zip(pieces, biases):
        m_new = jnp.maximum(m_new, jnp.max(s, axis=0, keepdims=True) + b)
    acc = jnp.exp2(m - m_new) * acc
    for s, b, vt in zip(pieces, biases, vts):
        acc = acc + jnp.dot(vt, _weights(s - (m_new - b)), preferred_element_type=jnp.float32)
    return m_new, acc


STALE_MAX_CAP = 40.0


def _update_state_stale(pieces, biases, vts, state):
    m, acc = state
    m_new = m
    for s, b, vt in zip(pieces, biases, vts):
        m_new = jnp.maximum(m_new, jnp.max(s, axis=0, keepdims=True) + b)
        acc = acc + jnp.dot(vt, _weights(s - (m - b)), preferred_element_type=jnp.float32)
    rise = m_new - m
    return (m_new, jnp.exp2(-rise) * acc), rise


def _attend_groups(n_groups, group_inputs, states, fin_ref):
    def fast(i, carry):
        states, worst = carry
        new, rises = zip(*(_update_state_stale(*inp, st)
                           for inp, st in zip(group_inputs(i), states)))
        return tuple(new), functools.reduce(jnp.maximum, rises + (worst,))

    def exact(i, states):
        return tuple(_update_state(*inp, st) for inp, st in zip(group_inputs(i), states))

    def store(states):
        for c, (_, acc) in enumerate(states):
            fin_ref[c] = acc

    done, worst = lax.fori_loop(0, n_groups, fast, (states, jnp.zeros_like(states[0][0])))
    store(done)

    @pl.when(jnp.max(worst) > STALE_MAX_CAP)
    def _():
        store(lax.fori_loop(0, n_groups, exact, states))


class _LaggedGrid:
    def __init__(self, n_steps, n_groups, n_blocks):
        self.n_steps, self.n_groups, self.n_blocks = n_steps, n_groups, n_blocks

    def _decode(self, item, index):
        per_batch = self.n_groups * self.n_blocks
        return index(item // per_batch, (item // self.n_blocks) % self.n_groups,
                     item % self.n_blocks)

    def cur(self, index):
        return lambda t: self._decode(jnp.minimum(t, self.n_steps - 1), index)

    def prev(self, index):
        return lambda t: self._decode(jnp.maximum(t - 1, 0), index)


def _key_rows(k_ref, first_block, n_blocks, cols=slice(None)):
    rows = n_blocks * ATT_BLOCK
    return k_ref[pl.ds(pl.multiple_of(first_block * ATT_BLOCK, ATT_BLOCK), rows), cols]


def _moba_kernel(q_ref, k_ref, v_ref, g_ref, o_ref, kmh_ref, kml_ref, vt_ref, bias_ref, fin_ref,
                 *, n_blocks, n_heads, n_steps):
    t = pl.program_id(0)
    qi = lax.rem(jnp.minimum(t, n_steps - 1), n_blocks)
    heads = [slice(h * HEAD_DIM, (h + 1) * HEAD_DIM) for h in range(n_heads)]

    @pl.when(t == 0)
    def _():
        fin_ref[...] = jnp.ones_like(fin_ref)

    @pl.when(qi == 0)
    def _():
        for h, cols in enumerate(heads):
            kf = k_ref[:, cols].astype(jnp.float32).reshape(n_blocks, MOBA_BLOCK, HEAD_DIM)
            km = jnp.sum(kf, axis=1) * (1.0 / MOBA_BLOCK)
            hi = km.astype(jnp.bfloat16)
            kmh_ref[h] = hi
            kml_ref[h] = (km - hi.astype(jnp.float32)).astype(jnp.bfloat16)
            for n in range(n_blocks):
                vt_ref[h * n_blocks + n] = _transposed_values(
                    v_ref[n * ATT_BLOCK:(n + 1) * ATT_BLOCK, cols])

    for h, cols in enumerate(heads):
        o_ref[:, cols] = _rms(_normalised(fin_ref[h]).T, g_ref[:, cols]).astype(o_ref.dtype)

    qs = [q_ref[:, cols] for cols in heads]
    for h, q in enumerate(qs):
        gate = (lax.dot_general(kmh_ref[h], q, NT_DIMS, preferred_element_type=jnp.float32)
                + lax.dot_general(kml_ref[h], q, NT_DIMS, preferred_element_type=jnp.float32))
        blk = lax.broadcasted_iota(jnp.int32, gate.shape, 0)
        past = blk < qi
        g = jnp.where(past, gate, -jnp.inf)
        sel = jnp.zeros(gate.shape, jnp.float32)
        for _ in range(MOBA_TOPK):
            mx = jnp.max(g, axis=0, keepdims=True)
            first = jnp.min(jnp.where(g == mx, blk, n_blocks), axis=0, keepdims=True)
            hit = blk == first
            sel = jnp.where(hit, jnp.where(past, 1.0, 0.0), sel)
            g = jnp.where(hit, -jnp.inf, g)
        bias_ref[h] = jnp.where(sel > 0.0, 0.0, NEG_BIG)

    diag = [_scores(_key_rows(k_ref, qi, 1, cols), q) for q, cols in zip(qs, heads)]
    states = tuple(_diag_state(s, vt_ref[h * n_blocks + qi]) for h, s in enumerate(diag))

    def group_inputs(i):
        first = i * ATT_GROUP
        scores = [_scores(_key_rows(k_ref, first, ATT_GROUP, cols), q)
                  for q, cols in zip(qs, heads)]
        inputs = []
        for h, s in enumerate(scores):
            pieces = [s[t * ATT_BLOCK:(t + 1) * ATT_BLOCK] for t in range(ATT_GROUP)]
            biases = [bias_ref[h, pl.ds(first + t, 1), :] for t in range(ATT_GROUP)]
            vts = [vt_ref[h * n_blocks + first + t] for t in range(ATT_GROUP)]
            inputs.append((pieces, biases, vts))
        return inputs

    _attend_groups(lax.div(qi + (ATT_GROUP - 1), ATT_GROUP), group_inputs, states, fin_ref)


def _moba_attention(qk, vproj, out_g, n_heads, mix_width, heads_per_step=8):
    b, s, _ = qk.shape
    n_blocks = s // MOBA_BLOCK
    hp = heads_per_step
    w = hp * HEAD_DIM
    n_groups = n_heads // hp
    n_steps = b * n_groups * n_blocks
    grid = _LaggedGrid(n_steps, n_groups, n_blocks)
    return pl.pallas_call(
        functools.partial(_moba_kernel, n_blocks=n_blocks, n_heads=hp, n_steps=n_steps),
        out_shape=jax.ShapeDtypeStruct((b, s, mix_width), jnp.bfloat16),
        grid=(n_steps + 1,),
        in_specs=[pl.BlockSpec((None, MOBA_BLOCK, w), grid.cur(lambda bi, h, qi: (bi, qi, h))),
                  pl.BlockSpec((None, s, w), grid.cur(lambda bi, h, qi: (bi, 0, n_groups + h))),
                  pl.BlockSpec((None, s, w), grid.cur(lambda bi, h, qi: (bi, 0, h))),
                  pl.BlockSpec((None, 1, w), grid.prev(lambda bi, h, qi: (h, 0, 0)))],
        out_specs=pl.BlockSpec((None, MOBA_BLOCK, w), grid.prev(lambda bi, h, qi: (bi, qi, h))),
        scratch_shapes=[pltpu.VMEM((hp, n_blocks, HEAD_DIM), jnp.bfloat16),
                        pltpu.VMEM((hp, n_blocks, HEAD_DIM), jnp.bfloat16),
                        pltpu.VMEM((hp * n_blocks, HEAD_DIM + VT_PAD, MOBA_BLOCK), jnp.bfloat16),
                        pltpu.VMEM((hp, n_blocks, MOBA_BLOCK), jnp.float32),
                        pltpu.VMEM((hp, HEAD_DIM + VT_PAD, MOBA_BLOCK), jnp.float32)],
        compiler_params=_params("arbitrary"),
        name="moba_attention",
    )(qk, qk, vproj, out_g.reshape(n_groups, 1, w))


def _diff_kernel(lq1_ref, lk1_ref, lq2_ref, lk2_ref, q_ref, k_ref, v_ref, g_ref, merged_ref,
                 o_ref, vt_ref, fin_ref, *, lambda_init, n_heads, n_steps):
    del merged_ref
    t = pl.program_id(0)
    n_blocks = vt_ref.shape[0] // n_heads
    qi = lax.rem(jnp.minimum(t, n_steps - 1), n_blocks)
    vw = 2 * HEAD_DIM

    @pl.when(t == 0)
    def _():
        fin_ref[...] = jnp.ones_like(fin_ref)

    @pl.when(qi == 0)
    def _():
        for h in range(n_heads):
            for n in range(n_blocks):
                vt_ref[h * n_blocks + n] = _transposed_values(
                    v_ref[n * ATT_BLOCK:(n + 1) * ATT_BLOCK, h * vw:(h + 1) * vw])

    lam = (jnp.exp(jnp.sum(lq1_ref[...] * lk1_ref[...], axis=1, keepdims=True))
           - jnp.exp(jnp.sum(lq2_ref[...] * lk2_ref[...], axis=1, keepdims=True))
           + lambda_init)
    for h in range(n_heads):
        o = (_normalised(fin_ref[2 * h]) - lam * _normalised(fin_ref[2 * h + 1])).T
        o_ref[:, h * vw:(h + 1) * vw] = (_rms(o, g_ref[...]) * (1.0 - lambda_init)).astype(o_ref.dtype)

    subs = [slice(j * HEAD_DIM, (j + 1) * HEAD_DIM) for j in range(2 * n_heads)]
    qs = [q_ref[:, cols] for cols in subs]
    diag = [_scores(_key_rows(k_ref, qi, 1, cols), q) for q, cols in zip(qs, subs)]
    states = tuple(_diag_state(s, vt_ref[(j // 2) * n_blocks + qi]) for j, s in enumerate(diag))

    def group_inputs(i):
        first = i * ATT_GROUP
        scores = [_scores(_key_rows(k_ref, first, ATT_GROUP, cols), q) for q, cols in zip(qs, subs)]
        biases = [jnp.where(first + t < qi, 0.0, NEG_BIG) for t in range(ATT_GROUP)]
        inputs = []
        for j, s in enumerate(scores):
            vts = [vt_ref[(j // 2) * n_blocks + first + t] for t in range(ATT_GROUP)]
            pieces = [s[t * ATT_BLOCK:(t + 1) * ATT_BLOCK] for t in range(ATT_GROUP)]
            inputs.append((pieces, biases, vts))
        return inputs

    _attend_groups(lax.div(qi + (ATT_GROUP - 1), ATT_GROUP), group_inputs, states, fin_ref)


def _diff_attention(qk, vproj, merged, lq1, lk1, lq2, lk2, subln_g, n_heads, lambda_init,
                    heads_per_step=4):
    b, s, qk_width = qk.shape
    v_col0 = out_col0 = vproj.shape[2] // 2
    col0 = qk_width // 2
    hp = heads_per_step
    w = hp * 2 * HEAD_DIM
    n_groups = n_heads // hp
    tq = ATT_BLOCK
    n_steps = b * n_groups * (s // tq)
    grid = _LaggedGrid(n_steps, n_groups, s // tq)
    vec = lambda x: x.reshape(1, HEAD_DIM).astype(jnp.float32)
    vspec = pl.BlockSpec((1, HEAD_DIM), lambda t: (0, 0))
    qb, kb, vb, ob = col0 // w, col0 // w + n_groups, v_col0 // w, out_col0 // w
    return pl.pallas_call(
        functools.partial(_diff_kernel, lambda_init=lambda_init, n_heads=hp, n_steps=n_steps),
        out_shape=jax.ShapeDtypeStruct(merged.shape, merged.dtype),
        grid=(n_steps + 1,),
        in_specs=[vspec, vspec, vspec, vspec,
                  pl.BlockSpec((None, tq, w), grid.cur(lambda bi, h, qi: (bi, qi, qb + h))),
                  pl.BlockSpec((None, s, w), grid.cur(lambda bi, h, qi: (bi, 0, kb + h))),
                  pl.BlockSpec((None, s, w), grid.cur(lambda bi, h, qi: (bi, 0, vb + h))),
                  pl.BlockSpec((1, 2 * HEAD_DIM), lambda t: (0, 0)),
                  pl.BlockSpec(memory_space=pl.ANY)],
        out_specs=pl.BlockSpec((None, tq, w), grid.prev(lambda bi, h, qi: (bi, qi, ob + h))),
        scratch_shapes=[pltpu.VMEM((hp * (s // ATT_BLOCK), 2 * HEAD_DIM + VT_PAD, ATT_BLOCK),
                                   jnp.bfloat16),
                        pltpu.VMEM((2 * hp, 2 * HEAD_DIM + VT_PAD, ATT_BLOCK), jnp.float32)],
        input_output_aliases={8: 0},
        compiler_params=_params("arbitrary"),
        name="diff_attention",
    )(vec(lq1), vec(lk1), vec(lq2), vec(lk2), qk, qk, vproj,
      subln_g.reshape(1, 2 * HEAD_DIM), merged)


def _cross_kernel(x_ref, gc_ref, wq_ref, kv_ref, wo_ref, gm_ref, x2_ref, hm_ref):
    c = QK_SCALE
    x = x_ref[...]
    hc = _rms(x, gc_ref[...]).astype(jnp.bfloat16)
    q = jnp.dot(hc, wq_ref[...], preferred_element_type=jnp.float32).astype(jnp.bfloat16)
    kv_w = MEM_HEADS * HEAD_DIM
    outs = []
    for h in range(MEM_HEADS):
        k = kv_ref[:, h * HEAD_DIM:(h + 1) * HEAD_DIM]
        v = kv_ref[:, kv_w + h * HEAD_DIM:kv_w + (h + 1) * HEAD_DIM]
        s = lax.dot_general(q[:, h * HEAD_DIM:(h + 1) * HEAD_DIM], k, NT_DIMS,
                            preferred_element_type=jnp.float32)
        m = jnp.max(s, axis=1, keepdims=True)
        p = jnp.exp2((s - m) * c)
        l = jnp.sum(p, axis=1, keepdims=True)
        o = jnp.dot(p.astype(jnp.bfloat16), v, preferred_element_type=jnp.float32) / l
        outs.append(o.astype(jnp.bfloat16))
    o = jnp.concatenate(outs, axis=1)
    x2 = x + jnp.dot(o, wo_ref[...], preferred_element_type=jnp.float32)
    x2_ref[...] = x2
    hm_ref[...] = _rms(x2, gm_ref[...]).astype(hm_ref.dtype)


def _cross_sublayer(x, kv, g_cross, w_cq, w_co, g_mlp, seq_len, mem_len, tq=256):
    n, d = x.shape
    qw = w_cq.shape[1]
    const = lambda i: (0, 0)
    return pl.pallas_call(
        _cross_kernel,
        out_shape=(jax.ShapeDtypeStruct((n, d), jnp.float32),
                   jax.ShapeDtypeStruct((n, d), jnp.bfloat16)),
        grid=(n // tq,),
        in_specs=[pl.BlockSpec((tq, d), lambda i: (i, 0)),
                  pl.BlockSpec((1, d), const),
                  pl.BlockSpec((d, qw), const),
                  pl.BlockSpec((mem_len, 2 * qw), lambda i: (i // (seq_len // tq), 0)),
                  pl.BlockSpec((qw, d), const),
                  pl.BlockSpec((1, d), const)],
        out_specs=(pl.BlockSpec((tq, d), lambda i: (i, 0)),
                   pl.BlockSpec((tq, d), lambda i: (i, 0))),
        compiler_params=_params("parallel"),
        name="cross_sublayer",
    )(x, g_cross.reshape(1, d), w_cq, kv, w_co, g_mlp.reshape(1, d))


def kernel(x, mem, ln_mix_g, w_in, moba_out_g, lambda_q1, lambda_k1, lambda_q2, lambda_k2,
           diff_subln_g, w_out, ln_cross_g, ln_mem_g, w_cq, w_ckv, w_co, ln_mlp_g,
           w_up, w_down, final_g):
    b, s, d = x.shape
    mem_len = mem.shape[1]
    depth = w_in.shape[0]
    mix_width = w_out.shape[1]
    moba_width = mix_width // 2
    moba_heads = moba_width // HEAD_DIM
    diff_heads = (mix_width - moba_width) // (2 * HEAD_DIM)
    bf = jnp.bfloat16

    xf = x.reshape(b * s, d)
    memf = mem.reshape(b * mem_len, d)
    for l in range(depth):
        lambda_init = 0.8 - 0.6 * math.exp(-0.3 * l)

        h = _rmsnorm(xf, ln_mix_g[l], bf)
        w_in_bf = w_in[l].astype(bf)
        qk = _inproj_qk(h, w_in_bf, s, moba_width).reshape(b, s, -1)
        bpg = moba_width // 1024
        vproj, w_out_bf, w_up_bf = _matmul(
            h, w_in_bf, bm=1024, bn=1024, out_dtype=bf, n_out=2 * moba_width,
            w_col_block=lambda j: (2 + 3 * (j // bpg)) * bpg + j % bpg,
            round_also=(w_out, w_up), layer=l, name="inproj_v")
        vproj = vproj.reshape(b, s, -1)
        merged = _moba_attention(qk, vproj, moba_out_g[l], moba_heads, mix_width)
        merged = _diff_attention(qk, vproj, merged, lambda_q1[l], lambda_k1[l], lambda_q2[l],
                                 lambda_k2[l], diff_subln_g[l], diff_heads, lambda_init)
        x1 = _matmul(merged.reshape(b * s, mix_width), w_out_bf, bm=1024, bn=512,
                     out_dtype=jnp.float32, res=xf, name="outproj")

        mn = _rmsnorm(memf, ln_mem_g[l], bf)
        kv = _matmul(mn, w_ckv, layer=l, bm=b * mem_len, bn=512, out_dtype=bf, name="mem_kv")
        x2, hm = _cross_sublayer(x1, kv, ln_cross_g[l], w_cq[l].astype(bf), w_co[l].astype(bf),
                                 ln_mlp_g[l], s, mem_len)

        u, w_down_bf = _matmul(hm, w_up_bf, bm=1024, bn=1024, out_dtype=bf, act="relu2",
                               round_also=(w_down,), layer=l, name="mlp_up")
        xf = _matmul_ksplit(u, w_down_bf, x2, bm=1024, bn=1024, bk=4096, name="mlp_down")
    return _rmsnorm(xf, final_g, jnp.float32).reshape(b, s, d)
```

```python
import functools
import math

import jax
import jax.numpy as jnp
from jax import lax
from jax.experimental import pallas as pl
from jax.experimental.pallas import tpu as pltpu

HEAD_DIM = 128
MOBA_BLOCK = 256
MOBA_TOPK = 3
ROT_DIM = HEAD_DIM // 4
ROPE_THETA = 500000.0
MEM_HEADS = 4
EPS = 1e-5

VMEM_LIMIT_BYTES = 60 * 2**20
NEG_BIG = -1e30
LOG2E = 1.4426950408889634
QK_SCALE = HEAD_DIM ** -0.5 * LOG2E
NT_DIMS = (((1,), (1,)), ((), ()))
ATT_BLOCK = 256
MOBA_GROUPS = (4, 2)
DIFF_GROUPS = (2,)
MM_CHUNK = 256


def _params(*sem):
    return pltpu.CompilerParams(dimension_semantics=sem, vmem_limit_bytes=VMEM_LIMIT_BYTES)


def _rms(x, g):
    return x * lax.rsqrt(jnp.mean(x * x, axis=-1, keepdims=True) + EPS) * g


def _rmsnorm_kernel(x_ref, g_ref, o_ref):
    o_ref[...] = _rms(x_ref[...], g_ref[...]).astype(o_ref.dtype)


def _rmsnorm(x, g, out_dtype, bm=256):
    n, d = x.shape
    return pl.pallas_call(
        _rmsnorm_kernel,
        out_shape=jax.ShapeDtypeStruct((n, d), out_dtype),
        grid=(n // bm,),
        in_specs=[pl.BlockSpec((bm, d), lambda i: (i, 0)),
                  pl.BlockSpec((1, d), lambda i: (0, 0))],
        out_specs=pl.BlockSpec((bm, d), lambda i: (i, 0)),
        compiler_params=_params("parallel"),
        name="rmsnorm",
    )(x, g.reshape(1, d))


def _weight_spec(w, layer, bk, bn, index):
    if w.ndim == 2:
        return pl.BlockSpec((bk, bn), index)
    return pl.BlockSpec((None, bk, bn), lambda *g: (layer,) + tuple(index(*g)))


def _mm_kernel(*refs, act, has_res, n_side):
    a_ref, w_ref = refs[0], refs[1]
    res_ref = refs[2] if has_res else None
    side_in = refs[2 + has_res:2 + has_res + n_side]
    o_ref = refs[2 + has_res + n_side]
    side_out = refs[3 + has_res + n_side:]
    for src, dst in zip(side_in, side_out):
        dst[...] = src[...].astype(jnp.bfloat16)
    for c in range(o_ref.shape[1] // MM_CHUNK):
        cols = slice(c * MM_CHUNK, (c + 1) * MM_CHUNK)
        acc = jnp.dot(a_ref[...], w_ref[:, cols].astype(jnp.bfloat16),
                      preferred_element_type=jnp.float32)
        if act == "relu2":
            acc = jnp.square(jnp.maximum(acc, 0.0))
        if has_res:
            acc = acc + res_ref[:, cols]
        o_ref[:, cols] = acc.astype(o_ref.dtype)


def _matmul(a, w, *, bm, bn, out_dtype, act=None, res=None, layer=0, n_out=None,
            w_col_block=lambda j: j, round_also=(), name):
    m, kdim = a.shape
    n = w.shape[-1] if n_out is None else n_out
    nj = n // bn
    in_specs = [pl.BlockSpec((bm, kdim), lambda i, j: (i, 0)),
                _weight_spec(w, layer, kdim, bn, lambda i, j: (0, w_col_block(j)))]
    args = [a, w]
    if res is not None:
        in_specs.append(pl.BlockSpec((bm, bn), lambda i, j: (i, j)))
        args.append(res)
    out_shape = [jax.ShapeDtypeStruct((m, n), out_dtype)]
    out_specs = [pl.BlockSpec((bm, bn), lambda i, j: (i, j))]
    for side in round_also:
        _, rows, cols = side.shape
        slab = rows // ((m // bm) * nj)
        in_specs.append(pl.BlockSpec((None, slab, cols), lambda i, j: (layer, i * nj + j, 0)))
        args.append(side)
        out_shape.append(jax.ShapeDtypeStruct((rows, cols), jnp.bfloat16))
        out_specs.append(pl.BlockSpec((slab, cols), lambda i, j: (i * nj + j, 0)))
    out = pl.pallas_call(
        functools.partial(_mm_kernel, act=act, has_res=res is not None, n_side=len(round_also)),
        out_shape=out_shape,
        grid=(m // bm, nj),
        in_specs=in_specs,
        out_specs=out_specs,
        compiler_params=_params("parallel", "parallel"),
        name=name,
    )(*args)
    return out if round_also else out[0]


def _mm_ksplit_kernel(a_ref, w_ref, res_ref, o_ref):
    @pl.when(pl.program_id(2) == 0)
    def _():
        o_ref[...] = res_ref[...]

    for c in range(o_ref.shape[1] // MM_CHUNK):
        cols = slice(c * MM_CHUNK, (c + 1) * MM_CHUNK)
        o_ref[:, cols] = (jnp.dot(a_ref[...], w_ref[:, cols], preferred_element_type=jnp.float32)
                          + o_ref[:, cols])


def _matmul_ksplit(a, w, res, *, bm, bn, bk, name):
    m, kdim = a.shape
    n = w.shape[1]
    return pl.pallas_call(
        _mm_ksplit_kernel,
        out_shape=jax.ShapeDtypeStruct((m, n), jnp.float32),
        grid=(m // bm, n // bn, kdim // bk),
        in_specs=[pl.BlockSpec((bm, bk), lambda i, j, k: (i, k)),
                  pl.BlockSpec((bk, bn), lambda i, j, k: (k, j)),
                  pl.BlockSpec((bm, bn), lambda i, j, k: (i, j))],
        out_specs=pl.BlockSpec((bm, bn), lambda i, j, k: (i, j)),
        compiler_params=_params("parallel", "parallel", "arbitrary"),
        name=name,
    )(a, w, res)


def _inproj_kernel(h_ref, w_ref, cos_ref, sa_ref, sb_ref, o_ref, acc_ref):
    @pl.when(pl.program_id(0) == 0)
    def _():
        acc_ref[...] = jnp.zeros_like(acc_ref)

    half_rows = acc_ref.shape[0] // 2
    for r0 in (0, half_rows):
        rows = slice(r0, r0 + half_rows)
        cos, sa, sb = cos_ref[rows, :], sa_ref[rows, :], sb_ref[rows, :]
        for c in range(o_ref.shape[1] // HEAD_DIM):
            cols = slice(c * HEAD_DIM, (c + 1) * HEAD_DIM)
            a = acc_ref[rows, cols]
            r = (a * cos + pltpu.roll(a, HEAD_DIM - ROT_DIM // 2, 1) * sa
                 + pltpu.roll(a, ROT_DIM // 2, 1) * sb)
            o_ref[rows, cols] = r.astype(o_ref.dtype)
    acc_ref[...] = jnp.dot(h_ref[...], w_ref[...], preferred_element_type=jnp.float32)


def _rope_tables(seq_len):
    half = ROT_DIM // 2
    pos = jnp.arange(seq_len, dtype=jnp.float32)
    inv_freq = ROPE_THETA ** (-jnp.arange(0, ROT_DIM, 2, dtype=jnp.float32) / ROT_DIM)
    ang = pos[:, None] * inv_freq[None, :]
    cos, sin = jnp.cos(ang), jnp.sin(ang)
    z = lambda w: jnp.zeros((seq_len, w), jnp.float32)
    cos_t = jnp.concatenate([cos, cos, jnp.ones((seq_len, HEAD_DIM - ROT_DIM), jnp.float32)], axis=1)
    sin_a = jnp.concatenate([-sin, z(HEAD_DIM - half)], axis=1)
    sin_b = jnp.concatenate([z(half), sin, z(HEAD_DIM - ROT_DIM)], axis=1)
    return cos_t, sin_a, sin_b


def _inproj_qk(h, w, seq_len, group_width, bm=1024, bn=1024):
    m, kdim = h.shape
    bpg = group_width // bn
    nj = 4 * bpg
    n_blocks = (m // bm) * nj
    cos_t, sin_a, sin_b = (jnp.stack([t * QK_SCALE, t]) for t in _rope_tables(seq_len))

    mul = lambda t: jnp.minimum(t, n_blocks - 1)
    fin = lambda t: jnp.maximum(t - 1, 0)

    def w_block(t):
        out_group = (mul(t) % nj) // bpg
        return 0, (out_group + out_group // 2) * bpg + mul(t) % bpg

    tspec = pl.BlockSpec(
        (None, bm, HEAD_DIM),
        lambda t: (((fin(t) % nj) // bpg) % 2, (fin(t) // nj) % (seq_len // bm), 0))
    return pl.pallas_call(
        _inproj_kernel,
        out_shape=jax.ShapeDtypeStruct((m, 4 * group_width), jnp.bfloat16),
        grid=(n_blocks + 1,),
        in_specs=[pl.BlockSpec((bm, kdim), lambda t: (mul(t) // nj, 0)),
                  pl.BlockSpec((kdim, bn), w_block),
                  tspec, tspec, tspec],
        out_specs=pl.BlockSpec((bm, bn), lambda t: (fin(t) // nj, fin(t) % nj)),
        scratch_shapes=[pltpu.VMEM((bm, bn), jnp.float32)],
        compiler_params=_params("arbitrary"),
        name="inproj_qk_rope",
    )(h, w, cos_t, sin_a, sin_b)


VT_PAD = 16


def _scores(k, q):
    return lax.dot_general(k, q, NT_DIMS, preferred_element_type=jnp.float32)


def _transposed_values(v_blk):
    vt = v_blk.astype(jnp.float32).T.astype(jnp.bfloat16)
    row = lax.broadcasted_iota(jnp.int32, (VT_PAD, v_blk.shape[0]), 0)
    return jnp.concatenate([vt, jnp.where(row == 0, 1.0, 0.0).astype(jnp.bfloat16)], axis=0)


def _weights(x):
    return jnp.exp2(x).astype(jnp.bfloat16)


def _diag_state(s, vt):
    key = lax.broadcasted_iota(jnp.int32, s.shape, 0)
    qry = lax.broadcasted_iota(jnp.int32, s.shape, 1)
    s = jnp.where(key <= qry, s, NEG_BIG)
    m = jnp.max(s, axis=0, keepdims=True)
    return m, jnp.dot(vt, _weights(s - m), preferred_element_type=jnp.float32)


def _normalised(acc):
    v_dim = acc.shape[0] - VT_PAD
    return acc[:v_dim] / acc[v_dim:v_dim + 1]


def _update_state(pieces, biases, vts, state):
    m, acc = state
    m_new = m
    for s, b in zip(pieces, biases):
        m_new = jnp.maximum(m_new, jnp.max(s, axis=0, keepdims=True) + b)
    acc = jnp.exp2(m - m_new) * acc
    for s, b, vt in zip(pieces, biases, vts):
        acc = acc + jnp.dot(vt, _weights(s - (m_new - b)), preferred_element_type=jnp.float32)
    return m_new, acc


STALE_MAX_CAP = 40.0


def _update_state_stale(pieces, biases, vts, state):
    m, acc = state
    m_new = m
    for s, b, vt in zip(pieces, biases, vts):
        m_new = jnp.maximum(m_new, jnp.max(s, axis=0, keepdims=True) + b)
        acc = acc + jnp.dot(vt, _weights(s - (m - b)), preferred_element_type=jnp.float32)
    rise = m_new - m
    return (m_new, jnp.exp2(-rise) * acc), rise


def _attend_groups(n_blocks, group_sizes, group_inputs, states, fin_ref):
    def fast(inputs, carry):
        states, worst = carry
        new, rises = zip(*(_update_state_stale(*inp, st) for inp, st in zip(inputs, states)))
        return tuple(new), functools.reduce(jnp.maximum, rises + (worst,))

    def exact(inputs, states):
        return tuple(_update_state(*inp, st) for inp, st in zip(inputs, states))

    def fold(update, carry):
        first = 0
        for idx, size in enumerate(group_sizes):
            slack = size - 1 if idx == len(group_sizes) - 1 else 0
            trips = lax.div(n_blocks - first + slack, size)
            carry = lax.fori_loop(
                0, trips,
                lambda i, c, size=size, first=first: update(group_inputs(first + i * size, size), c),
                carry)
            first = first + trips * size
        return carry

    def store(states):
        for c, (_, acc) in enumerate(states):
            fin_ref[c] = acc

    done, worst = fold(fast, (states, jnp.zeros_like(states[0][0])))
    store(done)

    @pl.when(jnp.max(worst) > STALE_MAX_CAP)
    def _():
        store(fold(exact, states))


class _LaggedGrid:
    def __init__(self, n_steps, n_groups, n_blocks):
        self.n_steps, self.n_groups, self.n_blocks = n_steps, n_groups, n_blocks

    def _decode(self, item, index):
        per_batch = self.n_groups * self.n_blocks
        return index(item // per_batch, (item // self.n_blocks) % self.n_groups,
                     item % self.n_blocks)

    def cur(self, index):
        return lambda t: self._decode(jnp.minimum(t, self.n_steps - 1), index)

    def prev(self, index):
        return lambda t: self._decode(jnp.maximum(t - 1, 0), index)


def _key_rows(k_ref, first_block, n_blocks, cols=slice(None)):
    rows = n_blocks * ATT_BLOCK
    return k_ref[pl.ds(pl.multiple_of(first_block * ATT_BLOCK, ATT_BLOCK), rows), cols]


def _moba_kernel(q_ref, k_ref, v_ref, g_ref, o_ref, kmh_ref, kml_ref, vt_ref, bias_ref, fin_ref,
                 *, n_blocks, n_heads, n_steps):
    t = pl.program_id(0)
    qi = lax.rem(jnp.minimum(t, n_steps - 1), n_blocks)
    heads = [slice(h * HEAD_DIM, (h + 1) * HEAD_DIM) for h in range(n_heads)]

    @pl.when(t == 0)
    def _():
        fin_ref[...] = jnp.ones_like(fin_ref)

    @pl.when(qi == 0)
    def _():
        for h, cols in enumerate(heads):
            kf = k_ref[:, cols].astype(jnp.float32).reshape(n_blocks, MOBA_BLOCK, HEAD_DIM)
            km = jnp.sum(kf, axis=1) * (1.0 / MOBA_BLOCK)
            hi = km.astype(jnp.bfloat16)
            kmh_ref[h] = hi
            kml_ref[h] = (km - hi.astype(jnp.float32)).astype(jnp.bfloat16)
            for n in range(n_blocks):
                vt_ref[h * n_blocks + n] = _transposed_values(
                    v_ref[n * ATT_BLOCK:(n + 1) * ATT_BLOCK, cols])

    for h, cols in enumerate(heads):
        o_ref[:, cols] = _rms(_normalised(fin_ref[h]).T, g_ref[:, cols]).astype(o_ref.dtype)

    qs = [q_ref[:, cols] for cols in heads]
    for h, q in enumerate(qs):
        gate = (lax.dot_general(kmh_ref[h], q, NT_DIMS, preferred_element_type=jnp.float32)
                + lax.dot_general(kml_ref[h], q, NT_DIMS, preferred_element_type=jnp.float32))
        blk = lax.broadcasted_iota(jnp.int32, gate.shape, 0)
        past = blk < qi
        g = jnp.where(past, gate, -jnp.inf)
        sel = jnp.zeros(gate.shape, jnp.float32)
        for _ in range(MOBA_TOPK):
            mx = jnp.max(g, axis=0, keepdims=True)
            first = jnp.min(jnp.where(g == mx, blk, n_blocks), axis=0, keepdims=True)
            hit = blk == first
            sel = jnp.where(hit, jnp.where(past, 1.0, 0.0), sel)
            g = jnp.where(hit, -jnp.inf, g)
        bias_ref[h] = jnp.where(sel > 0.0, 0.0, NEG_BIG)

    diag = [_scores(_key_rows(k_ref, qi, 1, cols), q) for q, cols in zip(qs, heads)]
    states = tuple(_diag_state(s, vt_ref[h * n_blocks + qi]) for h, s in enumerate(diag))

    def group_inputs(first, size):
        scores = [_scores(_key_rows(k_ref, first, size, cols), q) for q, cols in zip(qs, heads)]
        inputs = []
        for h, s in enumerate(scores):
            pieces = [s[t * ATT_BLOCK:(t + 1) * ATT_BLOCK] for t in range(size)]
            biases = [bias_ref[h, pl.ds(first + t, 1), :] for t in range(size)]
            vts = [vt_ref[h * n_blocks + first + t] for t in range(size)]
            inputs.append((pieces, biases, vts))
        return inputs

    _attend_groups(qi, MOBA_GROUPS, group_inputs, states, fin_ref)


def _moba_attention(qk, vproj, out_g, n_heads, mix_width, heads_per_step=8):
    b, s, _ = qk.shape
    n_blocks = s // MOBA_BLOCK
    hp = heads_per_step
    w = hp * HEAD_DIM
    n_groups = n_heads // hp
    n_steps = b * n_groups * n_blocks
    grid = _LaggedGrid(n_steps, n_groups, n_blocks)
    return pl.pallas_call(
        functools.partial(_moba_kernel, n_blocks=n_blocks, n_heads=hp, n_steps=n_steps),
        out_shape=jax.ShapeDtypeStruct((b, s, mix_width), jnp.bfloat16),
        grid=(n_steps + 1,),
        in_specs=[pl.BlockSpec((None, MOBA_BLOCK, w), grid.cur(lambda bi, h, qi: (bi, qi, h))),
                  pl.BlockSpec((None, s, w), grid.cur(lambda bi, h, qi: (bi, 0, n_groups + h))),
                  pl.BlockSpec((None, s, w), grid.cur(lambda bi, h, qi: (bi, 0, h))),
                  pl.BlockSpec((None, 1, w), grid.prev(lambda bi, h, qi: (h, 0, 0)))],
        out_specs=pl.BlockSpec((None, MOBA_BLOCK, w), grid.prev(lambda bi, h, qi: (bi, qi, h))),
        scratch_shapes=[pltpu.VMEM((hp, n_blocks, HEAD_DIM), jnp.bfloat16),
                        pltpu.VMEM((hp, n_blocks, HEAD_DIM), jnp.bfloat16),
                        pltpu.VMEM((hp * n_blocks, HEAD_DIM + VT_PAD, MOBA_BLOCK), jnp.bfloat16),
                        pltpu.VMEM((hp, n_blocks, MOBA_BLOCK), jnp.float32),
                        pltpu.VMEM((hp, HEAD_DIM + VT_PAD, MOBA_BLOCK), jnp.float32)],
        compiler_params=_params("arbitrary"),
        name="moba_attention",
    )(qk, qk, vproj, out_g.reshape(n_groups, 1, w))


def _diff_kernel(lq1_ref, lk1_ref, lq2_ref, lk2_ref, q_ref, k_ref, v_ref, g_ref, merged_ref,
                 o_ref, vt_ref, fin_ref, *, lambda_init, n_heads, n_steps):
    del merged_ref
    t = pl.program_id(0)
    n_blocks = vt_ref.shape[0] // n_heads
    qi = lax.rem(jnp.minimum(t, n_steps - 1), n_blocks)
    vw = 2 * HEAD_DIM

    @pl.when(t == 0)
    def _():
        fin_ref[...] = jnp.ones_like(fin_ref)

    @pl.when(qi == 0)
    def _():
        for h in range(n_heads):
            for n in range(n_blocks):
                vt_ref[h * n_blocks + n] = _transposed_values(
                    v_ref[n * ATT_BLOCK:(n + 1) * ATT_BLOCK, h * vw:(h + 1) * vw])

    lam = (jnp.exp(jnp.sum(lq1_ref[...] * lk1_ref[...], axis=1, keepdims=True))
           - jnp.exp(jnp.sum(lq2_ref[...] * lk2_ref[...], axis=1, keepdims=True))
           + lambda_init)
    for h in range(n_heads):
        o = (_normalised(fin_ref[2 * h]) - lam * _normalised(fin_ref[2 * h + 1])).T
        o_ref[:, h * vw:(h + 1) * vw] = (_rms(o, g_ref[...]) * (1.0 - lambda_init)).astype(o_ref.dtype)

    subs = [slice(j * HEAD_DIM, (j + 1) * HEAD_DIM) for j in range(2 * n_heads)]
    qs = [q_ref[:, cols] for cols in subs]
    diag = [_scores(_key_rows(k_ref, qi, 1, cols), q) for q, cols in zip(qs, subs)]
    states = tuple(_diag_state(s, vt_ref[(j // 2) * n_blocks + qi]) for j, s in enumerate(diag))

    def group_inputs(first, size):
        scores = [_scores(_key_rows(k_ref, first, size, cols), q) for q, cols in zip(qs, subs)]
        biases = [jnp.where(first + t < qi, 0.0, NEG_BIG) for t in range(size)]
        inputs = []
        for j, s in enumerate(scores):
            vts = [vt_ref[(j // 2) * n_blocks + first + t] for t in range(size)]
            pieces = [s[t * ATT_BLOCK:(t + 1) * ATT_BLOCK] for t in range(size)]
            inputs.append((pieces, biases, vts))
        return inputs

    _attend_groups(qi, DIFF_GROUPS, group_inputs, states, fin_ref)


def _diff_attention(qk, vproj, merged, lq1, lk1, lq2, lk2, subln_g, n_heads, lambda_init,
                    heads_per_step=4):
    b, s, qk_width = qk.shape
    v_col0 = out_col0 = vproj.shape[2] // 2
    col0 = qk_width // 2
    hp = heads_per_step
    w = hp * 2 * HEAD_DIM
    n_groups = n_heads // hp
    tq = ATT_BLOCK
    n_steps = b * n_groups * (s // tq)
    grid = _LaggedGrid(n_steps, n_groups, s // tq)
    vec = lambda x: x.reshape(1, HEAD_DIM).astype(jnp.float32)
    vspec = pl.BlockSpec((1, HEAD_DIM), lambda t: (0, 0))
    qb, kb, vb, ob = col0 // w, col0 // w + n_groups, v_col0 // w, out_col0 // w
    return pl.pallas_call(
        functools.partial(_diff_kernel, lambda_init=lambda_init, n_heads=hp, n_steps=n_steps),
        out_shape=jax.ShapeDtypeStruct(merged.shape, merged.dtype),
        grid=(n_steps + 1,),
        in_specs=[vspec, vspec, vspec, vspec,
                  pl.BlockSpec((None, tq, w), grid.cur(lambda bi, h, qi: (bi, qi, qb + h))),
                  pl.BlockSpec((None, s, w), grid.cur(lambda bi, h, qi: (bi, 0, kb + h))),
                  pl.BlockSpec((None, s, w), grid.cur(lambda bi, h, qi: (bi, 0, vb + h))),
                  pl.BlockSpec((1, 2 * HEAD_DIM), lambda t: (0, 0)),
                  pl.BlockSpec(memory_space=pl.ANY)],
        out_specs=pl.BlockSpec((None, tq, w), grid.prev(lambda bi, h, qi: (bi, qi, ob + h))),
        scratch_shapes=[pltpu.VMEM((hp * (s // ATT_BLOCK), 2 * HEAD_DIM + VT_PAD, ATT_BLOCK),
                                   jnp.bfloat16),
                        pltpu.VMEM((2 * hp, 2 * HEAD_DIM + VT_PAD, ATT_BLOCK), jnp.float32)],
        input_output_aliases={8: 0},
        compiler_params=_params("arbitrary"),
        name="diff_attention",
    )(vec(lq1), vec(lk1), vec(lq2), vec(lk2), qk, qk, vproj,
      subln_g.reshape(1, 2 * HEAD_DIM), merged)


def _cross_kernel(x_ref, gc_ref, wq_ref, kv_ref, wo_ref, gm_ref, x2_ref, hm_ref):
    c = QK_SCALE
    x = x_ref[...]
    hc = _rms(x, gc_ref[...]).astype(jnp.bfloat16)
    q = jnp.dot(hc, wq_ref[...], preferred_element_type=jnp.float32).astype(jnp.bfloat16)
    kv_w = MEM_HEADS * HEAD_DIM
    outs = []
    for h in range(MEM_HEADS):
        k = kv_ref[:, h * HEAD_DIM:(h + 1) * HEAD_DIM]
        v = kv_ref[:, kv_w + h * HEAD_DIM:kv_w + (h + 1) * HEAD_DIM]
        s = lax.dot_general(q[:, h * HEAD_DIM:(h + 1) * HEAD_DIM], k, NT_DIMS,
                            preferred_element_type=jnp.float32)
        m = jnp.max(s, axis=1, keepdims=True)
        p = jnp.exp2((s - m) * c)
        l = jnp.sum(p, axis=1, keepdims=True)
        o = jnp.dot(p.astype(jnp.bfloat16), v, preferred_element_type=jnp.float32) / l
        outs.append(o.astype(jnp.bfloat16))
    o = jnp.concatenate(outs, axis=1)
    x2 = x + jnp.dot(o, wo_ref[...], preferred_element_type=jnp.float32)
    x2_ref[...] = x2
    hm_ref[...] = _rms(x2, gm_ref[...]).astype(hm_ref.dtype)


def _cross_sublayer(x, kv, g_cross, w_cq, w_co, g_mlp, seq_len, mem_len, tq=256):
    n, d = x.shape
    qw = w_cq.shape[1]
    const = lambda i: (0, 0)
    return pl.pallas_call(
        _cross_kernel,
        out_shape=(jax.ShapeDtypeStruct((n, d), jnp.float32),
                   jax.ShapeDtypeStruct((n, d), jnp.bfloat16)),
        grid=(n // tq,),
        in_specs=[pl.BlockSpec((tq, d), lambda i: (i, 0)),
                  pl.BlockSpec((1, d), const),
                  pl.BlockSpec((d, qw), const),
                  pl.BlockSpec((mem_len, 2 * qw), lambda i: (i // (seq_len // tq), 0)),
                  pl.BlockSpec((qw, d), const),
                  pl.BlockSpec((1, d), const)],
        out_specs=(pl.BlockSpec((tq, d), lambda i: (i, 0)),
                   pl.BlockSpec((tq, d), lambda i: (i, 0))),
        compiler_params=_params("parallel"),
        name="cross_sublayer",
    )(x, g_cross.reshape(1, d), w_cq, kv, w_co, g_mlp.reshape(1, d))


def kernel(x, mem, ln_mix_g, w_in, moba_out_g, lambda_q1, lambda_k1, lambda_q2, lambda_k2,
           diff_subln_g, w_out, ln_cross_g, ln_mem_g, w_cq, w_ckv, w_co, ln_mlp_g,
           w_up, w_down, final_g):
    b, s, d = x.shape
    mem_len = mem.shape[1]
    depth = w_in.shape[0]
    mix_width = w_out.shape[1]
    moba_width = mix_width // 2
    moba_heads = moba_width // HEAD_DIM
    diff_heads = (mix_width - moba_width) // (2 * HEAD_DIM)
    bf = jnp.bfloat16

    xf = x.reshape(b * s, d)
    memf = mem.reshape(b * mem_len, d)
    for l in range(depth):
        lambda_init = 0.8 - 0.6 * math.exp(-0.3 * l)

        h = _rmsnorm(xf, ln_mix_g[l], bf)
        w_in_bf = w_in[l].astype(bf)
        qk = _inproj_qk(h, w_in_bf, s, moba_width).reshape(b, s, -1)
        bpg = moba_width // 1024
        vproj, w_out_bf, w_up_bf = _matmul(
            h, w_in_bf, bm=1024, bn=1024, out_dtype=bf, n_out=2 * moba_width,
            w_col_block=lambda j: (2 + 3 * (j // bpg)) * bpg + j % bpg,
            round_also=(w_out, w_up), layer=l, name="inproj_v")
        vproj = vproj.reshape(b, s, -1)
        merged = _moba_attention(qk, vproj, moba_out_g[l], moba_heads, mix_width)
        merged = _diff_attention(qk, vproj, merged, lambda_q1[l], lambda_k1[l], lambda_q2[l],
                                 lambda_k2[l], diff_subln_g[l], diff_heads, lambda_init)
        x1 = _matmul(merged.reshape(b * s, mix_width), w_out_bf, bm=1024, bn=512,
                     out_dtype=jnp.float32, res=xf, name="outproj")

        mn = _rmsnorm(memf, ln_mem_g[l], bf)
        kv = _matmul(mn, w_ckv, layer=l, bm=b * mem_len, bn=512, out_dtype=bf, name="mem_kv")
        x2, hm = _cross_sublayer(x1, kv, ln_cross_g[l], w_cq[l].astype(bf), w_co[l].astype(bf),
                                 ln_mlp_g[l], s, mem_len)

        u, w_down_bf = _matmul(hm, w_up_bf, bm=1024, bn=1024, out_dtype=bf, act="relu2",
                               round_also=(w_down,), layer=l, name="mlp_up")
        xf = _matmul_ksplit(u, w_down_bf, x2, bm=1024, bn=1024, bk=4096, name="mlp_down")
    return _rmsnorm(xf, final_g, jnp.float32).reshape(b, s, d)
```

```python
import functools
import math

import jax
import jax.numpy as jnp
from jax import lax
from jax.experimental import pallas as pl
from jax.experimental.pallas import tpu as pltpu

HEAD_DIM = 128
MOBA_BLOCK = 256
MOBA_TOPK = 3
ROT_DIM = HEAD_DIM // 4
ROPE_THETA = 500000.0
MEM_HEADS = 4
EPS = 1e-5

VMEM_LIMIT_BYTES = 60 * 2**20
NEG_BIG = -1e30
LOG2E = 1.4426950408889634
QK_SCALE = HEAD_DIM ** -0.5 * LOG2E
NT_DIMS = (((1,), (1,)), ((), ()))
ATT_BLOCK = 256
MOBA_GROUPS = (4, 2)
DIFF_GROUPS = (4, 2)
MM_CHUNK = 256


def _params(*sem):
    return pltpu.CompilerParams(dimension_semantics=sem, vmem_limit_bytes=VMEM_LIMIT_BYTES)


def _rms(x, g):
    return x * lax.rsqrt(jnp.mean(x * x, axis=-1, keepdims=True) + EPS) * g


def _rmsnorm_kernel(x_ref, g_ref, o_ref):
    o_ref[...] = _rms(x_ref[...], g_ref[...]).astype(o_ref.dtype)


def _rmsnorm(x, g, out_dtype, bm=256):
    n, d = x.shape
    return pl.pallas_call(
        _rmsnorm_kernel,
        out_shape=jax.ShapeDtypeStruct((n, d), out_dtype),
        grid=(n // bm,),
        in_specs=[pl.BlockSpec((bm, d), lambda i: (i, 0)),
                  pl.BlockSpec((1, d), lambda i: (0, 0))],
        out_specs=pl.BlockSpec((bm, d), lambda i: (i, 0)),
        compiler_params=_params("parallel"),
        name="rmsnorm",
    )(x, g.reshape(1, d))


def _weight_spec(w, layer, bk, bn, index):
    if w.ndim == 2:
        return pl.BlockSpec((bk, bn), index)
    return pl.BlockSpec((None, bk, bn), lambda *g: (layer,) + tuple(index(*g)))


def _mm_kernel(*refs, act, has_res, n_side):
    a_ref, w_ref = refs[0], refs[1]
    res_ref = refs[2] if has_res else None
    side_in = refs[2 + has_res:2 + has_res + n_side]
    o_ref = refs[2 + has_res + n_side]
    side_out = refs[3 + has_res + n_side:]
    for src, dst in zip(side_in, side_out):
        dst[...] = src[...].astype(jnp.bfloat16)
    for c in range(o_ref.shape[1] // MM_CHUNK):
        cols = slice(c * MM_CHUNK, (c + 1) * MM_CHUNK)
        acc = jnp.dot(a_ref[...], w_ref[:, cols].astype(jnp.bfloat16),
                      preferred_element_type=jnp.float32)
        if act == "relu2":
            acc = jnp.square(jnp.maximum(acc, 0.0))
        if has_res:
            acc = acc + res_ref[:, cols]
        o_ref[:, cols] = acc.astype(o_ref.dtype)


def _matmul(a, w, *, bm, bn, out_dtype, act=None, res=None, layer=0, n_out=None,
            w_col_block=lambda j: j, round_also=(), name):
    m, kdim = a.shape
    n = w.shape[-1] if n_out is None else n_out
    nj = n // bn
    in_specs = [pl.BlockSpec((bm, kdim), lambda i, j: (i, 0)),
                _weight_spec(w, layer, kdim, bn, lambda i, j: (0, w_col_block(j)))]
    args = [a, w]
    if res is not None:
        in_specs.append(pl.BlockSpec((bm, bn), lambda i, j: (i, j)))
        args.append(res)
    out_shape = [jax.ShapeDtypeStruct((m, n), out_dtype)]
    out_specs = [pl.BlockSpec((bm, bn), lambda i, j: (i, j))]
    for side in round_also:
        _, rows, cols = side.shape
        slab = rows // ((m // bm) * nj)
        in_specs.append(pl.BlockSpec((None, slab, cols), lambda i, j: (layer, i * nj + j, 0)))
        args.append(side)
        out_shape.append(jax.ShapeDtypeStruct((rows, cols), jnp.bfloat16))
        out_specs.append(pl.BlockSpec((slab, cols), lambda i, j: (i * nj + j, 0)))
    out = pl.pallas_call(
        functools.partial(_mm_kernel, act=act, has_res=res is not None, n_side=len(round_also)),
        out_shape=out_shape,
        grid=(m // bm, nj),
        in_specs=in_specs,
        out_specs=out_specs,
        compiler_params=_params("parallel", "parallel"),
        name=name,
    )(*args)
    return out if round_also else out[0]


def _mm_ksplit_kernel(a_ref, w_ref, res_ref, o_ref):
    @pl.when(pl.program_id(2) == 0)
    def _():
        o_ref[...] = res_ref[...]

    for c in range(o_ref.shape[1] // MM_CHUNK):
        cols = slice(c * MM_CHUNK, (c + 1) * MM_CHUNK)
        o_ref[:, cols] = (jnp.dot(a_ref[...], w_ref[:, cols], preferred_element_type=jnp.float32)
                          + o_ref[:, cols])


def _matmul_ksplit(a, w, res, *, bm, bn, bk, name):
    m, kdim = a.shape
    n = w.shape[1]
    return pl.pallas_call(
        _mm_ksplit_kernel,
        out_shape=jax.ShapeDtypeStruct((m, n), jnp.float32),
        grid=(m // bm, n // bn, kdim // bk),
        in_specs=[pl.BlockSpec((bm, bk), lambda i, j, k: (i, k)),
                  pl.BlockSpec((bk, bn), lambda i, j, k: (k, j)),
                  pl.BlockSpec((bm, bn), lambda i, j, k: (i, j))],
        out_specs=pl.BlockSpec((bm, bn), lambda i, j, k: (i, j)),
        compiler_params=_params("parallel", "parallel", "arbitrary"),
        name=name,
    )(a, w, res)


def _inproj_kernel(h_ref, w_ref, cos_ref, sa_ref, sb_ref, o_ref, acc_ref):
    @pl.when(pl.program_id(0) == 0)
    def _():
        acc_ref[...] = jnp.zeros_like(acc_ref)

    half_rows = acc_ref.shape[0] // 2
    for r0 in (0, half_rows):
        rows = slice(r0, r0 + half_rows)
        cos, sa, sb = cos_ref[rows, :], sa_ref[rows, :], sb_ref[rows, :]
        for c in range(o_ref.shape[1] // HEAD_DIM):
            cols = slice(c * HEAD_DIM, (c + 1) * HEAD_DIM)
            a = acc_ref[rows, cols]
            r = (a * cos + pltpu.roll(a, HEAD_DIM - ROT_DIM // 2, 1) * sa
                 + pltpu.roll(a, ROT_DIM // 2, 1) * sb)
            o_ref[rows, cols] = r.astype(o_ref.dtype)
    acc_ref[...] = jnp.dot(h_ref[...], w_ref[...], preferred_element_type=jnp.float32)


def _rope_tables(seq_len):
    half = ROT_DIM // 2
    pos = jnp.arange(seq_len, dtype=jnp.float32)
    inv_freq = ROPE_THETA ** (-jnp.arange(0, ROT_DIM, 2, dtype=jnp.float32) / ROT_DIM)
    ang = pos[:, None] * inv_freq[None, :]
    cos, sin = jnp.cos(ang), jnp.sin(ang)
    z = lambda w: jnp.zeros((seq_len, w), jnp.float32)
    cos_t = jnp.concatenate([cos, cos, jnp.ones((seq_len, HEAD_DIM - ROT_DIM), jnp.float32)], axis=1)
    sin_a = jnp.concatenate([-sin, z(HEAD_DIM - half)], axis=1)
    sin_b = jnp.concatenate([z(half), sin, z(HEAD_DIM - ROT_DIM)], axis=1)
    return cos_t, sin_a, sin_b


def _inproj_qk(h, w, seq_len, group_width, bm=1024, bn=1024):
    m, kdim = h.shape
    bpg = group_width // bn
    nj = 4 * bpg
    n_blocks = (m // bm) * nj
    cos_t, sin_a, sin_b = (jnp.stack([t * QK_SCALE, t]) for t in _rope_tables(seq_len))

    mul = lambda t: jnp.minimum(t, n_blocks - 1)
    fin = lambda t: jnp.maximum(t - 1, 0)

    def w_block(t):
        out_group = (mul(t) % nj) // bpg
        return 0, (out_group + out_group // 2) * bpg + mul(t) % bpg

    tspec = pl.BlockSpec(
        (None, bm, HEAD_DIM),
        lambda t: (((fin(t) % nj) // bpg) % 2, (fin(t) // nj) % (seq_len // bm), 0))
    return pl.pallas_call(
        _inproj_kernel,
        out_shape=jax.ShapeDtypeStruct((m, 4 * group_width), jnp.bfloat16),
        grid=(n_blocks + 1,),
        in_specs=[pl.BlockSpec((bm, kdim), lambda t: (mul(t) // nj, 0)),
                  pl.BlockSpec((kdim, bn), w_block),
                  tspec, tspec, tspec],
        out_specs=pl.BlockSpec((bm, bn), lambda t: (fin(t) // nj, fin(t) % nj)),
        scratch_shapes=[pltpu.VMEM((bm, bn), jnp.float32)],
        compiler_params=_params("arbitrary"),
        name="inproj_qk_rope",
    )(h, w, cos_t, sin_a, sin_b)


VT_PAD = 16


def _scores(k, q):
    return lax.dot_general(k, q, NT_DIMS, preferred_element_type=jnp.float32)


def _transposed_values(v_blk):
    vt = v_blk.astype(jnp.float32).T.astype(jnp.bfloat16)
    row = lax.broadcasted_iota(jnp.int32, (VT_PAD, v_blk.shape[0]), 0)
    return jnp.concatenate([vt, jnp.where(row == 0, 1.0, 0.0).astype(jnp.bfloat16)], axis=0)


def _weights(x):
    return jnp.exp2(x).astype(jnp.bfloat16)


def _diag_state(s, vt):
    key = lax.broadcasted_iota(jnp.int32, s.shape, 0)
    qry = lax.broadcasted_iota(jnp.int32, s.shape, 1)
    s = jnp.where(key <= qry, s, NEG_BIG)
    m = jnp.max(s, axis=0, keepdims=True)
    return m, jnp.dot(vt, _weights(s - m), preferred_element_type=jnp.float32)


def _normalised(acc):
    v_dim = acc.shape[0] - VT_PAD
    return acc[:v_dim] / acc[v_dim:v_dim + 1]


def _update_state(pieces, biases, vts, state):
    m, acc = state
    m_new = m
    for s, b in zip(pieces, biases):
        m_new = jnp.maximum(m_new, jnp.max(s, axis=0, keepdims=True) + b)
    acc = jnp.exp2(m - m_new) * acc
    for s, b, vt in zip(pieces, biases, vts):
        acc = acc + jnp.dot(vt, _weights(s - (m_new - b)), preferred_element_type=jnp.float32)
    return m_new, acc


STALE_MAX_CAP = 40.0


def _update_state_stale(pieces, biases, vts, state):
    m, acc = state
    m_new = m
    for s, b, vt in zip(pieces, biases, vts):
        m_new = jnp.maximum(m_new, jnp.max(s, axis=0, keepdims=True) + b)
        acc = acc + jnp.dot(vt, _weights(s - (m - b)), preferred_element_type=jnp.float32)
    rise = m_new - m
    return (m_new, jnp.exp2(-rise) * acc), rise


def _attend_groups(n_blocks, group_sizes, group_inputs, states, fin_ref):
    def fast(inputs, carry):
        states, worst = carry
        new, rises = zip(*(_update_state_stale(*inp, st) for inp, st in zip(inputs, states)))
        return tuple(new), functools.reduce(jnp.maximum, rises + (worst,))

    def exact(inputs, states):
        return tuple(_update_state(*inp, st) for inp, st in zip(inputs, states))

    def fold(update, carry):
        first = 0
        for idx, size in enumerate(group_sizes):
            slack = size - 1 if idx == len(group_sizes) - 1 else 0
            trips = lax.div(n_blocks - first + slack, size)
            carry = lax.fori_loop(
                0, trips,
                lambda i, c, size=size, first=first: update(group_inputs(first + i * size, size), c),
                carry)
            first = first + trips * size
        return carry

    def store(states):
        for c, (_, acc) in enumerate(states):
            fin_ref[c] = acc

    done, worst = fold(fast, (states, jnp.zeros_like(states[0][0])))
    store(done)

    @pl.when(jnp.max(worst) > STALE_MAX_CAP)
    def _():
        store(fold(exact, states))


class _LaggedGrid:
    def __init__(self, n_steps, n_groups, n_blocks):
        self.n_steps, self.n_groups, self.n_blocks = n_steps, n_groups, n_blocks

    def _decode(self, item, index):
        per_batch = self.n_groups * self.n_blocks
        return index(item // per_batch, (item // self.n_blocks) % self.n_groups,
                     item % self.n_blocks)

    def cur(self, index):
        return lambda t: self._decode(jnp.minimum(t, self.n_steps - 1), index)

    def prev(self, index):
        return lambda t: self._decode(jnp.maximum(t - 1, 0), index)


def _key_rows(k_ref, first_block, n_blocks, cols=slice(None)):
    rows = n_blocks * ATT_BLOCK
    return k_ref[pl.ds(pl.multiple_of(first_block * ATT_BLOCK, ATT_BLOCK), rows), cols]


def _moba_kernel(q_ref, k_ref, v_ref, g_ref, o_ref, kmh_ref, kml_ref, vt_ref, bias_ref, fin_ref,
                 *, n_blocks, n_heads, n_steps):
    t = pl.program_id(0)
    qi = lax.rem(jnp.minimum(t, n_steps - 1), n_blocks)
    heads = [slice(h * HEAD_DIM, (h + 1) * HEAD_DIM) for h in range(n_heads)]

    @pl.when(t == 0)
    def _():
        fin_ref[...] = jnp.ones_like(fin_ref)

    @pl.when(qi == 0)
    def _():
        for h, cols in enumerate(heads):
            kf = k_ref[:, cols].astype(jnp.float32).reshape(n_blocks, MOBA_BLOCK, HEAD_DIM)
            km = jnp.sum(kf, axis=1) * (1.0 / MOBA_BLOCK)
            hi = km.astype(jnp.bfloat16)
            kmh_ref[h] = hi
            kml_ref[h] = (km - hi.astype(jnp.float32)).astype(jnp.bfloat16)
            for n in range(n_blocks):
                vt_ref[h * n_blocks + n] = _transposed_values(
                    v_ref[n * ATT_BLOCK:(n + 1) * ATT_BLOCK, cols])

    for h, cols in enumerate(heads):
        o_ref[:, cols] = _rms(_normalised(fin_ref[h]).T, g_ref[:, cols]).astype(o_ref.dtype)

    qs = [q_ref[:, cols] for cols in heads]
    for h, q in enumerate(qs):
        gate = (lax.dot_general(kmh_ref[h], q, NT_DIMS, preferred_element_type=jnp.float32)
                + lax.dot_general(kml_ref[h], q, NT_DIMS, preferred_element_type=jnp.float32))
        blk = lax.broadcasted_iota(jnp.int32, gate.shape, 0)
        past = blk < qi
        g = jnp.where(past, gate, -jnp.inf)
        sel = jnp.zeros(gate.shape, jnp.float32)
        for _ in range(MOBA_TOPK):
            mx = jnp.max(g, axis=0, keepdims=True)
            first = jnp.min(jnp.where(g == mx, blk, n_blocks), axis=0, keepdims=True)
            hit = blk == first
            sel = jnp.where(hit, jnp.where(past, 1.0, 0.0), sel)
            g = jnp.where(hit, -jnp.inf, g)
        bias_ref[h] = jnp.where(sel > 0.0, 0.0, NEG_BIG)

    diag = [_scores(_key_rows(k_ref, qi, 1, cols), q) for q, cols in zip(qs, heads)]
    states = tuple(_diag_state(s, vt_ref[h * n_blocks + qi]) for h, s in enumerate(diag))

    def group_inputs(first, size):
        scores = [_scores(_key_rows(k_ref, first, size, cols), q) for q, cols in zip(qs, heads)]
        inputs = []
        for h, s in enumerate(scores):
            pieces = [s[t * ATT_BLOCK:(t + 1) * ATT_BLOCK] for t in range(size)]
            biases = [bias_ref[h, pl.ds(first + t, 1), :] for t in range(size)]
            vts = [vt_ref[h * n_blocks + first + t] for t in range(size)]
            inputs.append((pieces, biases, vts))
        return inputs

    _attend_groups(qi, MOBA_GROUPS, group_inputs, states, fin_ref)


def _moba_attention(qk, vproj, out_g, n_heads, mix_width, heads_per_step=8):
    b, s, _ = qk.shape
    n_blocks = s // MOBA_BLOCK
    hp = heads_per_step
    w = hp * HEAD_DIM
    n_groups = n_heads // hp
    n_steps = b * n_groups * n_blocks
    grid = _LaggedGrid(n_steps, n_groups, n_blocks)
    return pl.pallas_call(
        functools.partial(_moba_kernel, n_blocks=n_blocks, n_heads=hp, n_steps=n_steps),
        out_shape=jax.ShapeDtypeStruct((b, s, mix_width), jnp.bfloat16),
        grid=(n_steps + 1,),
        in_specs=[pl.BlockSpec((None, MOBA_BLOCK, w), grid.cur(lambda bi, h, qi: (bi, qi, h))),
                  pl.BlockSpec((None, s, w), grid.cur(lambda bi, h, qi: (bi, 0, n_groups + h))),
                  pl.BlockSpec((None, s, w), grid.cur(lambda bi, h, qi: (bi, 0, h))),
                  pl.BlockSpec((None, 1, w), grid.prev(lambda bi, h, qi: (h, 0, 0)))],
        out_specs=pl.BlockSpec((None, MOBA_BLOCK, w), grid.prev(lambda bi, h, qi: (bi, qi, h))),
        scratch_shapes=[pltpu.VMEM((hp, n_blocks, HEAD_DIM), jnp.bfloat16),
                        pltpu.VMEM((hp, n_blocks, HEAD_DIM), jnp.bfloat16),
                        pltpu.VMEM((hp * n_blocks, HEAD_DIM + VT_PAD, MOBA_BLOCK), jnp.bfloat16),
                        pltpu.VMEM((hp, n_blocks, MOBA_BLOCK), jnp.float32),
                        pltpu.VMEM((hp, HEAD_DIM + VT_PAD, MOBA_BLOCK), jnp.float32)],
        compiler_params=_params("arbitrary"),
        name="moba_attention",
    )(qk, qk, vproj, out_g.reshape(n_groups, 1, w))


def _diff_kernel(lq1_ref, lk1_ref, lq2_ref, lk2_ref, q_ref, k_ref, v_ref, g_ref, merged_ref,
                 o_ref, vt_ref, fin_ref, *, lambda_init, n_heads, n_steps):
    del merged_ref
    t = pl.program_id(0)
    n_blocks = vt_ref.shape[0] // n_heads
    qi = lax.rem(jnp.minimum(t, n_steps - 1), n_blocks)
    vw = 2 * HEAD_DIM

    @pl.when(t == 0)
    def _():
        fin_ref[...] = jnp.ones_like(fin_ref)

    @pl.when(qi == 0)
    def _():
        for h in range(n_heads):
            for n in range(n_blocks):
                vt_ref[h * n_blocks + n] = _transposed_values(
                    v_ref[n * ATT_BLOCK:(n + 1) * ATT_BLOCK, h * vw:(h + 1) * vw])

    lam = (jnp.exp(jnp.sum(lq1_ref[...] * lk1_ref[...], axis=1, keepdims=True))
           - jnp.exp(jnp.sum(lq2_ref[...] * lk2_ref[...], axis=1, keepdims=True))
           + lambda_init)
    for h in range(n_heads):
        o = (_normalised(fin_ref[2 * h]) - lam * _normalised(fin_ref[2 * h + 1])).T
        o_ref[:, h * vw:(h + 1) * vw] = (_rms(o, g_ref[...]) * (1.0 - lambda_init)).astype(o_ref.dtype)

    subs = [slice(j * HEAD_DIM, (j + 1) * HEAD_DIM) for j in range(2 * n_heads)]
    qs = [q_ref[:, cols] for cols in subs]
    diag = [_scores(_key_rows(k_ref, qi, 1, cols), q) for q, cols in zip(qs, subs)]
    states = tuple(_diag_state(s, vt_ref[(j // 2) * n_blocks + qi]) for j, s in enumerate(diag))

    def group_inputs(first, size):
        scores = [_scores(_key_rows(k_ref, first, size, cols), q) for q, cols in zip(qs, subs)]
        biases = [jnp.where(first + t < qi, 0.0, NEG_BIG) for t in range(size)]
        inputs = []
        for j, s in enumerate(scores):
            vts = [vt_ref[(j // 2) * n_blocks + first + t] for t in range(size)]
            pieces = [s[t * ATT_BLOCK:(t + 1) * ATT_BLOCK] for t in range(size)]
            inputs.append((pieces, biases, vts))
        return inputs

    _attend_groups(qi, DIFF_GROUPS, group_inputs, states, fin_ref)


def _diff_attention(qk, vproj, merged, lq1, lk1, lq2, lk2, subln_g, n_heads, lambda_init,
                    heads_per_step=4):
    b, s, qk_width = qk.shape
    v_col0 = out_col0 = vproj.shape[2] // 2
    col0 = qk_width // 2
    hp = heads_per_step
    w = hp * 2 * HEAD_DIM
    n_groups = n_heads // hp
    tq = ATT_BLOCK
    n_steps = b * n_groups * (s // tq)
    grid = _LaggedGrid(n_steps, n_groups, s // tq)
    vec = lambda x: x.reshape(1, HEAD_DIM).astype(jnp.float32)
    vspec = pl.BlockSpec((1, HEAD_DIM), lambda t: (0, 0))
    qb, kb, vb, ob = col0 // w, col0 // w + n_groups, v_col0 // w, out_col0 // w
    return pl.pallas_call(
        functools.partial(_diff_kernel, lambda_init=lambda_init, n_heads=hp, n_steps=n_steps),
        out_shape=jax.ShapeDtypeStruct(merged.shape, merged.dtype),
        grid=(n_steps + 1,),
        in_specs=[vspec, vspec, vspec, vspec,
                  pl.BlockSpec((None, tq, w), grid.cur(lambda bi, h, qi: (bi, qi, qb + h))),
                  pl.BlockSpec((None, s, w), grid.cur(lambda bi, h, qi: (bi, 0, kb + h))),
                  pl.BlockSpec((None, s, w), grid.cur(lambda bi, h, qi: (bi, 0, vb + h)),
                               pipeline_mode=pl.Buffered(1)),
                  pl.BlockSpec((1, 2 * HEAD_DIM), lambda t: (0, 0)),
                  pl.BlockSpec(memory_space=pl.ANY)],
        out_specs=pl.BlockSpec((None, tq, w), grid.prev(lambda bi, h, qi: (bi, qi, ob + h))),
        scratch_shapes=[pltpu.VMEM((hp * (s // ATT_BLOCK), 2 * HEAD_DIM + VT_PAD, ATT_BLOCK),
                                   jnp.bfloat16),
                        pltpu.VMEM((2 * hp, 2 * HEAD_DIM + VT_PAD, ATT_BLOCK), jnp.float32)],
        input_output_aliases={8: 0},
        compiler_params=_params("arbitrary"),
        name="diff_attention",
    )(vec(lq1), vec(lk1), vec(lq2), vec(lk2), qk, qk, vproj,
      subln_g.reshape(1, 2 * HEAD_DIM), merged)


def _cross_kernel(x_ref, gc_ref, wq_ref, kv_ref, wo_ref, gm_ref, x2_ref, hm_ref):
    c = QK_SCALE
    x = x_ref[...]
    hc = _rms(x, gc_ref[...]).astype(jnp.bfloat16)
    q = jnp.dot(hc, wq_ref[...], preferred_element_type=jnp.float32).astype(jnp.bfloat16)
    kv_w = MEM_HEADS * HEAD_DIM
    outs = []
    for h in range(MEM_HEADS):
        k = kv_ref[:, h * HEAD_DIM:(h + 1) * HEAD_DIM]
        v = kv_ref[:, kv_w + h * HEAD_DIM:kv_w + (h + 1) * HEAD_DIM]
        s = lax.dot_general(q[:, h * HEAD_DIM:(h + 1) * HEAD_DIM], k, NT_DIMS,
                            preferred_element_type=jnp.float32)
        m = jnp.max(s, axis=1, keepdims=True)
        p = jnp.exp2((s - m) * c)
        l = jnp.sum(p, axis=1, keepdims=True)
        o = jnp.dot(p.astype(jnp.bfloat16), v, preferred_element_type=jnp.float32) / l
        outs.append(o.astype(jnp.bfloat16))
    o = jnp.concatenate(outs, axis=1)
    x2 = x + jnp.dot(o, wo_ref[...], preferred_element_type=jnp.float32)
    x2_ref[...] = x2
    hm_ref[...] = _rms(x2, gm_ref[...]).astype(hm_ref.dtype)


def _cross_sublayer(x, kv, g_cross, w_cq, w_co, g_mlp, seq_len, mem_len, tq=256):
    n, d = x.shape
    qw = w_cq.shape[1]
    const = lambda i: (0, 0)
    return pl.pallas_call(
        _cross_kernel,
        out_shape=(jax.ShapeDtypeStruct((n, d), jnp.float32),
                   jax.ShapeDtypeStruct((n, d), jnp.bfloat16)),
        grid=(n // tq,),
        in_specs=[pl.BlockSpec((tq, d), lambda i: (i, 0)),
                  pl.BlockSpec((1, d), const),
                  pl.BlockSpec((d, qw), const),
                  pl.BlockSpec((mem_len, 2 * qw), lambda i: (i // (seq_len // tq), 0)),
                  pl.BlockSpec((qw, d), const),
                  pl.BlockSpec((1, d), const)],
        out_specs=(pl.BlockSpec((tq, d), lambda i: (i, 0)),
                   pl.BlockSpec((tq, d), lambda i: (i, 0))),
        compiler_params=_params("parallel"),
        name="cross_sublayer",
    )(x, g_cross.reshape(1, d), w_cq, kv, w_co, g_mlp.reshape(1, d))


def kernel(x, mem, ln_mix_g, w_in, moba_out_g, lambda_q1, lambda_k1, lambda_q2, lambda_k2,
           diff_subln_g, w_out, ln_cross_g, ln_mem_g, w_cq, w_ckv, w_co, ln_mlp_g,
           w_up, w_down, final_g):
    b, s, d = x.shape
    mem_len = mem.shape[1]
    depth = w_in.shape[0]
    mix_width = w_out.shape[1]
    moba_width = mix_width // 2
    moba_heads = moba_width // HEAD_DIM
    diff_heads = (mix_width - moba_width) // (2 * HEAD_DIM)
    bf = jnp.bfloat16

    xf = x.reshape(b * s, d)
    memf = mem.reshape(b * mem_len, d)
    for l in range(depth):
        lambda_init = 0.8 - 0.6 * math.exp(-0.3 * l)

        h = _rmsnorm(xf, ln_mix_g[l], bf)
        w_in_bf = w_in[l].astype(bf)
        qk = _inproj_qk(h, w_in_bf, s, moba_width).reshape(b, s, -1)
        bpg = moba_width // 1024
        vproj, w_out_bf, w_up_bf = _matmul(
            h, w_in_bf, bm=1024, bn=1024, out_dtype=bf, n_out=2 * moba_width,
            w_col_block=lambda j: (2 + 3 * (j // bpg)) * bpg + j % bpg,
            round_also=(w_out, w_up), layer=l, name="inproj_v")
        vproj = vproj.reshape(b, s, -1)
        merged = _moba_attention(qk, vproj, moba_out_g[l], moba_heads, mix_width)
        merged = _diff_attention(qk, vproj, merged, lambda_q1[l], lambda_k1[l], lambda_q2[l],
                                 lambda_k2[l], diff_subln_g[l], diff_heads, lambda_init)
        x1 = _matmul(merged.reshape(b * s, mix_width), w_out_bf, bm=1024, bn=512,
                     out_dtype=jnp.float32, res=xf, name="outproj")

        mn = _rmsnorm(memf, ln_mem_g[l], bf)
        kv = _matmul(mn, w_ckv, layer=l, bm=b * mem_len, bn=512, out_dtype=bf, name="mem_kv")
        x2, hm = _cross_sublayer(x1, kv, ln_cross_g[l], w_cq[l].astype(bf), w_co[l].astype(bf),
                                 ln_mlp_g[l], s, mem_len)

        u, w_down_bf = _matmul(hm, w_up_bf, bm=1024, bn=1024, out_dtype=bf, act="relu2",
                               round_also=(w_down,), layer=l, name="mlp_up")
        xf = _matmul_ksplit(u, w_down_bf, x2, bm=1024, bn=1024, bk=4096, name="mlp_down")
    return _rmsnorm(xf, final_g, jnp.float32).reshape(b, s, d)
```

```python
import functools
import math

import jax
import jax.numpy as jnp
from jax import lax
from jax.experimental import pallas as pl
from jax.experimental.pallas import tpu as pltpu

HEAD_DIM = 128
MOBA_BLOCK = 256
MOBA_TOPK = 3
ROT_DIM = HEAD_DIM // 4
ROPE_THETA = 500000.0
MEM_HEADS = 4
EPS = 1e-5

VMEM_LIMIT_BYTES = 60 * 2**20
NEG_BIG = -1e30
LOG2E = 1.4426950408889634
QK_SCALE = HEAD_DIM ** -0.5 * LOG2E
NT_DIMS = (((1,), (1,)), ((), ()))
ATT_BLOCK = 256
MOBA_GROUPS = (4, 2)
DIFF_GROUPS = (2,)
MM_CHUNK = 256


def _params(*sem):
    return pltpu.CompilerParams(dimension_semantics=sem, vmem_limit_bytes=VMEM_LIMIT_BYTES)


def _rms(x, g):
    return x * lax.rsqrt(jnp.mean(x * x, axis=-1, keepdims=True) + EPS) * g


def _rmsnorm_kernel(x_ref, g_ref, o_ref):
    o_ref[...] = _rms(x_ref[...], g_ref[...]).astype(o_ref.dtype)


def _rmsnorm(x, g, out_dtype, bm=512):
    n, d = x.shape
    return pl.pallas_call(
        _rmsnorm_kernel,
        out_shape=jax.ShapeDtypeStruct((n, d), out_dtype),
        grid=(n // bm,),
        in_specs=[pl.BlockSpec((bm, d), lambda i: (i, 0)),
                  pl.BlockSpec((1, d), lambda i: (0, 0))],
        out_specs=pl.BlockSpec((bm, d), lambda i: (i, 0)),
        compiler_params=_params("parallel"),
        name="rmsnorm",
    )(x, g.reshape(1, d))


def _weight_spec(w, layer, bk, bn, index):
    if w.ndim == 2:
        return pl.BlockSpec((bk, bn), index)
    return pl.BlockSpec((None, bk, bn), lambda *g: (layer,) + tuple(index(*g)))


def _mm_kernel(*refs, act, has_res, n_side):
    a_ref, w_ref = refs[0], refs[1]
    res_ref = refs[2] if has_res else None
    side_in = refs[2 + has_res:2 + has_res + n_side]
    o_ref = refs[2 + has_res + n_side]
    side_out = refs[3 + has_res + n_side:]
    for src, dst in zip(side_in, side_out):
        dst[...] = src[...].astype(jnp.bfloat16)
    for c in range(o_ref.shape[1] // MM_CHUNK):
        cols = slice(c * MM_CHUNK, (c + 1) * MM_CHUNK)
        acc = jnp.dot(a_ref[...], w_ref[:, cols].astype(jnp.bfloat16),
                      preferred_element_type=jnp.float32)
        if act == "relu2":
            acc = jnp.square(jnp.maximum(acc, 0.0))
        if has_res:
            acc = acc + res_ref[:, cols]
        o_ref[:, cols] = acc.astype(o_ref.dtype)


def _matmul(a, w, *, bm, bn, out_dtype, act=None, res=None, layer=0, n_out=None,
            w_col_block=lambda j: j, round_also=(), name):
    m, kdim = a.shape
    n = w.shape[-1] if n_out is None else n_out
    nj = n // bn
    in_specs = [pl.BlockSpec((bm, kdim), lambda i, j: (i, 0)),
                _weight_spec(w, layer, kdim, bn, lambda i, j: (0, w_col_block(j)))]
    args = [a, w]
    if res is not None:
        in_specs.append(pl.BlockSpec((bm, bn), lambda i, j: (i, j)))
        args.append(res)
    out_shape = [jax.ShapeDtypeStruct((m, n), out_dtype)]
    out_specs = [pl.BlockSpec((bm, bn), lambda i, j: (i, j))]
    for side in round_also:
        _, rows, cols = side.shape
        slab = rows // ((m // bm) * nj)
        in_specs.append(pl.BlockSpec((None, slab, cols), lambda i, j: (layer, i * nj + j, 0)))
        args.append(side)
        out_shape.append(jax.ShapeDtypeStruct((rows, cols), jnp.bfloat16))
        out_specs.append(pl.BlockSpec((slab, cols), lambda i, j: (i * nj + j, 0)))
    out = pl.pallas_call(
        functools.partial(_mm_kernel, act=act, has_res=res is not None, n_side=len(round_also)),
        out_shape=out_shape,
        grid=(m // bm, nj),
        in_specs=in_specs,
        out_specs=out_specs,
        compiler_params=_params("parallel", "parallel"),
        name=name,
    )(*args)
    return out if round_also else out[0]


def _mm_ksplit_kernel(a_ref, w_ref, res_ref, o_ref):
    @pl.when(pl.program_id(2) == 0)
    def _():
        o_ref[...] = res_ref[...]

    for c in range(o_ref.shape[1] // MM_CHUNK):
        cols = slice(c * MM_CHUNK, (c + 1) * MM_CHUNK)
        o_ref[:, cols] = (jnp.dot(a_ref[...], w_ref[:, cols], preferred_element_type=jnp.float32)
                          + o_ref[:, cols])


def _matmul_ksplit(a, w, res, *, bm, bn, bk, name):
    m, kdim = a.shape
    n = w.shape[1]
    return pl.pallas_call(
        _mm_ksplit_kernel,
        out_shape=jax.ShapeDtypeStruct((m, n), jnp.float32),
        grid=(m // bm, n // bn, kdim // bk),
        in_specs=[pl.BlockSpec((bm, bk), lambda i, j, k: (i, k)),
                  pl.BlockSpec((bk, bn), lambda i, j, k: (k, j)),
                  pl.BlockSpec((bm, bn), lambda i, j, k: (i, j))],
        out_specs=pl.BlockSpec((bm, bn), lambda i, j, k: (i, j)),
        compiler_params=_params("parallel", "parallel", "arbitrary"),
        name=name,
    )(a, w, res)


def _inproj_kernel(h_ref, w_ref, cos_ref, sa_ref, sb_ref, o_ref, acc_ref):
    @pl.when(pl.program_id(0) == 0)
    def _():
        acc_ref[...] = jnp.zeros_like(acc_ref)

    half_rows = acc_ref.shape[0] // 2
    for r0 in (0, half_rows):
        rows = slice(r0, r0 + half_rows)
        cos, sa, sb = cos_ref[rows, :], sa_ref[rows, :], sb_ref[rows, :]
        for c in range(o_ref.shape[1] // HEAD_DIM):
            cols = slice(c * HEAD_DIM, (c + 1) * HEAD_DIM)
            a = acc_ref[rows, cols]
            r = (a * cos + pltpu.roll(a, HEAD_DIM - ROT_DIM // 2, 1) * sa
                 + pltpu.roll(a, ROT_DIM // 2, 1) * sb)
            o_ref[rows, cols] = r.astype(o_ref.dtype)
    acc_ref[...] = jnp.dot(h_ref[...], w_ref[...], preferred_element_type=jnp.float32)


def _rope_tables(seq_len):
    half = ROT_DIM // 2
    pos = jnp.arange(seq_len, dtype=jnp.float32)
    inv_freq = ROPE_THETA ** (-jnp.arange(0, ROT_DIM, 2, dtype=jnp.float32) / ROT_DIM)
    ang = pos[:, None] * inv_freq[None, :]
    cos, sin = jnp.cos(ang), jnp.sin(ang)
    z = lambda w: jnp.zeros((seq_len, w), jnp.float32)
    cos_t = jnp.concatenate([cos, cos, jnp.ones((seq_len, HEAD_DIM - ROT_DIM), jnp.float32)], axis=1)
    sin_a = jnp.concatenate([-sin, z(HEAD_DIM - half)], axis=1)
    sin_b = jnp.concatenate([z(half), sin, z(HEAD_DIM - ROT_DIM)], axis=1)
    return cos_t, sin_a, sin_b


def _inproj_qk(h, w, seq_len, group_width, bm=1024, bn=1024):
    m, kdim = h.shape
    bpg = group_width // bn
    nj = 4 * bpg
    n_blocks = (m // bm) * nj
    cos_t, sin_a, sin_b = (jnp.stack([t * QK_SCALE, t]) for t in _rope_tables(seq_len))

    mul = lambda t: jnp.minimum(t, n_blocks - 1)
    fin = lambda t: jnp.maximum(t - 1, 0)

    def w_block(t):
        out_group = (mul(t) % nj) // bpg
        return 0, (out_group + out_group // 2) * bpg + mul(t) % bpg

    tspec = pl.BlockSpec(
        (None, bm, HEAD_DIM),
        lambda t: (((fin(t) % nj) // bpg) % 2, (fin(t) // nj) % (seq_len // bm), 0))
    return pl.pallas_call(
        _inproj_kernel,
        out_shape=jax.ShapeDtypeStruct((m, 4 * group_width), jnp.bfloat16),
        grid=(n_blocks + 1,),
        in_specs=[pl.BlockSpec((bm, kdim), lambda t: (mul(t) // nj, 0)),
                  pl.BlockSpec((kdim, bn), w_block),
                  tspec, tspec, tspec],
        out_specs=pl.BlockSpec((bm, bn), lambda t: (fin(t) // nj, fin(t) % nj)),
        scratch_shapes=[pltpu.VMEM((bm, bn), jnp.float32)],
        compiler_params=_params("arbitrary"),
        name="inproj_qk_rope",
    )(h, w, cos_t, sin_a, sin_b)


VT_PAD = 16


def _scores(k, q):
    return lax.dot_general(k, q, NT_DIMS, preferred_element_type=jnp.float32)


def _transposed_values(v_blk):
    vt = v_blk.astype(jnp.float32).T.astype(jnp.bfloat16)
    row = lax.broadcasted_iota(jnp.int32, (VT_PAD, v_blk.shape[0]), 0)
    return jnp.concatenate([vt, jnp.where(row == 0, 1.0, 0.0).astype(jnp.bfloat16)], axis=0)


def _weights(x):
    return jnp.exp2(x).astype(jnp.bfloat16)


def _diag_state(s, vt):
    key = lax.broadcasted_iota(jnp.int32, s.shape, 0)
    qry = lax.broadcasted_iota(jnp.int32, s.shape, 1)
    s = jnp.where(key <= qry, s, NEG_BIG)
    m = jnp.max(s, axis=0, keepdims=True)
    return m, jnp.dot(vt, _weights(s - m), preferred_element_type=jnp.float32)


def _normalised(acc):
    v_dim = acc.shape[0] - VT_PAD
    return acc[:v_dim] / acc[v_dim:v_dim + 1]


def _update_state(pieces, biases, vts, state):
    m, acc = state
    m_new = m
    for s, b in zip(pieces, biases):
        m_new = jnp.maximum(m_new, jnp.max(s, axis=0, keepdims=True) + b)
    acc = jnp.exp2(m - m_new) * acc
    for s, b, vt in zip(pieces, biases, vts):
        acc = acc + jnp.dot(vt, _weights(s - (m_new - b)), preferred_element_type=jnp.float32)
    return m_new, acc


STALE_MAX_CAP = 40.0


def _update_state_stale(pieces, biases, vts, state):
    m, acc = state
    m_new = m
    for s, b, vt in zip(pieces, biases, vts):
        m_new = jnp.maximum(m_new, jnp.max(s, axis=0, keepdims=True) + b)
        acc = acc + jnp.dot(vt, _weights(s - (m - b)), preferred_element_type=jnp.float32)
    rise = m_new - m
    return (m_new, jnp.exp2(-rise) * acc), rise


def _attend_groups(n_blocks, group_sizes, group_inputs, states, fin_ref):
    def fast(inputs, carry):
        states, worst = carry
        new, rises = zip(*(_update_state_stale(*inp, st) for inp, st in zip(inputs, states)))
        return tuple(new), functools.reduce(jnp.maximum, rises + (worst,))

    def exact(inputs, states):
        return tuple(_update_state(*inp, st) for inp, st in zip(inputs, states))

    def fold(update, carry):
        first = 0
        for idx, size in enumerate(group_sizes):
            slack = size - 1 if idx == len(group_sizes) - 1 else 0
            trips = lax.div(n_blocks - first + slack, size)
            carry = lax.fori_loop(
                0, trips,
                lambda i, c, size=size, first=first: update(group_inputs(first + i * size, size), c),
                carry)
            first = first + trips * size
        return carry

    def store(states):
        for c, (_, acc) in enumerate(states):
            fin_ref[c] = acc

    done, worst = fold(fast, (states, jnp.zeros_like(states[0][0])))
    store(done)

    @pl.when(jnp.max(worst) > STALE_MAX_CAP)
    def _():
        store(fold(exact, states))


class _LaggedGrid:
    def __init__(self, n_steps, n_groups, n_blocks):
        self.n_steps, self.n_groups, self.n_blocks = n_steps, n_groups, n_blocks

    def _decode(self, item, index):
        per_batch = self.n_groups * self.n_blocks
        return index(item // per_batch, (item // self.n_blocks) % self.n_groups,
                     item % self.n_blocks)

    def cur(self, index):
        return lambda t: self._decode(jnp.minimum(t, self.n_steps - 1), index)

    def prev(self, index):
        return lambda t: self._decode(jnp.maximum(t - 1, 0), index)


def _key_rows(k_ref, first_block, n_blocks, cols=slice(None)):
    rows = n_blocks * ATT_BLOCK
    return k_ref[pl.ds(pl.multiple_of(first_block * ATT_BLOCK, ATT_BLOCK), rows), cols]


def _moba_kernel(q_ref, k_ref, v_ref, g_ref, o_ref, kmh_ref, kml_ref, vt_ref, bias_ref, fin_ref,
                 *, n_blocks, n_heads, n_steps):
    t = pl.program_id(0)
    qi = lax.rem(jnp.minimum(t, n_steps - 1), n_blocks)
    heads = [slice(h * HEAD_DIM, (h + 1) * HEAD_DIM) for h in range(n_heads)]

    @pl.when(t == 0)
    def _():
        fin_ref[...] = jnp.ones_like(fin_ref)

    @pl.when(qi == 0)
    def _():
        for h, cols in enumerate(heads):
            kf = k_ref[:, cols].astype(jnp.float32).reshape(n_blocks, MOBA_BLOCK, HEAD_DIM)
            km = jnp.sum(kf, axis=1) * (1.0 / MOBA_BLOCK)
            hi = km.astype(jnp.bfloat16)
            kmh_ref[h] = hi
            kml_ref[h] = (km - hi.astype(jnp.float32)).astype(jnp.bfloat16)
            for n in range(n_blocks):
                vt_ref[h * n_blocks + n] = _transposed_values(
                    v_ref[n * ATT_BLOCK:(n + 1) * ATT_BLOCK, cols])

    for h, cols in enumerate(heads):
        o_ref[:, cols] = _rms(_normalised(fin_ref[h]).T, g_ref[:, cols]).astype(o_ref.dtype)

    qs = [q_ref[:, cols] for cols in heads]
    for h, q in enumerate(qs):
        gate = (lax.dot_general(kmh_ref[h], q, NT_DIMS, preferred_element_type=jnp.float32)
                + lax.dot_general(kml_ref[h], q, NT_DIMS, preferred_element_type=jnp.float32))
        blk = lax.broadcasted_iota(jnp.int32, gate.shape, 0)
        past = blk < qi
        g = jnp.where(past, gate, -jnp.inf)
        sel = jnp.zeros(gate.shape, jnp.float32)
        for _ in range(MOBA_TOPK):
            mx = jnp.max(g, axis=0, keepdims=True)
            first = jnp.min(jnp.where(g == mx, blk, n_blocks), axis=0, keepdims=True)
            hit = blk == first
            sel = jnp.where(hit, jnp.where(past, 1.0, 0.0), sel)
            g = jnp.where(hit, -jnp.inf, g)
        bias_ref[h] = jnp.where(sel > 0.0, 0.0, NEG_BIG)

    diag = [_scores(_key_rows(k_ref, qi, 1, cols), q) for q, cols in zip(qs, heads)]
    states = tuple(_diag_state(s, vt_ref[h * n_blocks + qi]) for h, s in enumerate(diag))

    def group_inputs(first, size):
        scores = [_scores(_key_rows(k_ref, first, size, cols), q) for q, cols in zip(qs, heads)]
        inputs = []
        for h, s in enumerate(scores):
            pieces = [s[t * ATT_BLOCK:(t + 1) * ATT_BLOCK] for t in range(size)]
            biases = [bias_ref[h, pl.ds(first + t, 1), :] for t in range(size)]
            vts = [vt_ref[h * n_blocks + first + t] for t in range(size)]
            inputs.append((pieces, biases, vts))
        return inputs

    _attend_groups(qi, MOBA_GROUPS, group_inputs, states, fin_ref)


def _moba_attention(qk, vproj, out_g, n_heads, mix_width, heads_per_step=8):
    b, s, _ = qk.shape
    n_blocks = s // MOBA_BLOCK
    hp = heads_per_step
    w = hp * HEAD_DIM
    n_groups = n_heads // hp
    n_steps = b * n_groups * n_blocks
    grid = _LaggedGrid(n_steps, n_groups, n_blocks)
    return pl.pallas_call(
        functools.partial(_moba_kernel, n_blocks=n_blocks, n_heads=hp, n_steps=n_steps),
        out_shape=jax.ShapeDtypeStruct((b, s, mix_width), jnp.bfloat16),
        grid=(n_steps + 1,),
        in_specs=[pl.BlockSpec((None, MOBA_BLOCK, w), grid.cur(lambda bi, h, qi: (bi, qi, h))),
                  pl.BlockSpec((None, s, w), grid.cur(lambda bi, h, qi: (bi, 0, n_groups + h))),
                  pl.BlockSpec((None, s, w), grid.cur(lambda bi, h, qi: (bi, 0, h))),
                  pl.BlockSpec((None, 1, w), grid.prev(lambda bi, h, qi: (h, 0, 0)))],
        out_specs=pl.BlockSpec((None, MOBA_BLOCK, w), grid.prev(lambda bi, h, qi: (bi, qi, h))),
        scratch_shapes=[pltpu.VMEM((hp, n_blocks, HEAD_DIM), jnp.bfloat16),
                        pltpu.VMEM((hp, n_blocks, HEAD_DIM), jnp.bfloat16),
                        pltpu.VMEM((hp * n_blocks, HEAD_DIM + VT_PAD, MOBA_BLOCK), jnp.bfloat16),
                        pltpu.VMEM((hp, n_blocks, MOBA_BLOCK), jnp.float32),
                        pltpu.VMEM((hp, HEAD_DIM + VT_PAD, MOBA_BLOCK), jnp.float32)],
        compiler_params=_params("arbitrary"),
        name="moba_attention",
    )(qk, qk, vproj, out_g.reshape(n_groups, 1, w))


def _diff_kernel(lq1_ref, lk1_ref, lq2_ref, lk2_ref, q_ref, k_ref, v_ref, g_ref, merged_ref,
                 o_ref, vt_ref, fin_ref, *, lambda_init, n_heads, n_steps):
    del merged_ref
    t = pl.program_id(0)
    n_blocks = vt_ref.shape[0] // n_heads
    qi = lax.rem(jnp.minimum(t, n_steps - 1), n_blocks)
    vw = 2 * HEAD_DIM

    @pl.when(t == 0)
    def _():
        fin_ref[...] = jnp.ones_like(fin_ref)

    @pl.when(qi == 0)
    def _():
        for h in range(n_heads):
            for n in range(n_blocks):
                vt_ref[h * n_blocks + n] = _transposed_values(
                    v_ref[n * ATT_BLOCK:(n + 1) * ATT_BLOCK, h * vw:(h + 1) * vw])

    lam = (jnp.exp(jnp.sum(lq1_ref[...] * lk1_ref[...], axis=1, keepdims=True))
           - jnp.exp(jnp.sum(lq2_ref[...] * lk2_ref[...], axis=1, keepdims=True))
           + lambda_init)
    for h in range(n_heads):
        o = (_normalised(fin_ref[2 * h]) - lam * _normalised(fin_ref[2 * h + 1])).T
        o_ref[:, h * vw:(h + 1) * vw] = (_rms(o, g_ref[...]) * (1.0 - lambda_init)).astype(o_ref.dtype)

    subs = [slice(j * HEAD_DIM, (j + 1) * HEAD_DIM) for j in range(2 * n_heads)]
    qs = [q_ref[:, cols] for cols in subs]
    diag = [_scores(_key_rows(k_ref, qi, 1, cols), q) for q, cols in zip(qs, subs)]
    states = tuple(_diag_state(s, vt_ref[(j // 2) * n_blocks + qi]) for j, s in enumerate(diag))

    def group_inputs(first, size):
        scores = [_scores(_key_rows(k_ref, first, size, cols), q) for q, cols in zip(qs, subs)]
        biases = [jnp.where(first + t < qi, 0.0, NEG_BIG) for t in range(size)]
        inputs = []
        for j, s in enumerate(scores):
            vts = [vt_ref[(j // 2) * n_blocks + first + t] for t in range(size)]
            pieces = [s[t * ATT_BLOCK:(t + 1) * ATT_BLOCK] for t in range(size)]
            inputs.append((pieces, biases, vts))
        return inputs

    _attend_groups(qi, DIFF_GROUPS, group_inputs, states, fin_ref)


def _diff_attention(qk, vproj, merged, lq1, lk1, lq2, lk2, subln_g, n_heads, lambda_init,
                    heads_per_step=4):
    b, s, qk_width = qk.shape
    v_col0 = out_col0 = vproj.shape[2] // 2
    col0 = qk_width // 2
    hp = heads_per_step
    w = hp * 2 * HEAD_DIM
    n_groups = n_heads // hp
    tq = ATT_BLOCK
    n_steps = b * n_groups * (s // tq)
    grid = _LaggedGrid(n_steps, n_groups, s // tq)
    vec = lambda x: x.reshape(1, HEAD_DIM).astype(jnp.float32)
    vspec = pl.BlockSpec((1, HEAD_DIM), lambda t: (0, 0))
    qb, kb, vb, ob = col0 // w, col0 // w + n_groups, v_col0 // w, out_col0 // w
    return pl.pallas_call(
        functools.partial(_diff_kernel, lambda_init=lambda_init, n_heads=hp, n_steps=n_steps),
        out_shape=jax.ShapeDtypeStruct(merged.shape, merged.dtype),
        grid=(n_steps + 1,),
        in_specs=[vspec, vspec, vspec, vspec,
                  pl.BlockSpec((None, tq, w), grid.cur(lambda bi, h, qi: (bi, qi, qb + h))),
                  pl.BlockSpec((None, s, w), grid.cur(lambda bi, h, qi: (bi, 0, kb + h))),
                  pl.BlockSpec((None, s, w), grid.cur(lambda bi, h, qi: (bi, 0, vb + h))),
                  pl.BlockSpec((1, 2 * HEAD_DIM), lambda t: (0, 0)),
                  pl.BlockSpec(memory_space=pl.ANY)],
        out_specs=pl.BlockSpec((None, tq, w), grid.prev(lambda bi, h, qi: (bi, qi, ob + h))),
        scratch_shapes=[pltpu.VMEM((hp * (s // ATT_BLOCK), 2 * HEAD_DIM + VT_PAD, ATT_BLOCK),
                                   jnp.bfloat16),
                        pltpu.VMEM((2 * hp, 2 * HEAD_DIM + VT_PAD, ATT_BLOCK), jnp.float32)],
        input_output_aliases={8: 0},
        compiler_params=_params("arbitrary"),
        name="diff_attention",
    )(vec(lq1), vec(lk1), vec(lq2), vec(lk2), qk, qk, vproj,
      subln_g.reshape(1, 2 * HEAD_DIM), merged)


def _cross_kernel(x_ref, gc_ref, wq_ref, kv_ref, wo_ref, gm_ref, x2_ref, hm_ref):
    c = QK_SCALE
    x = x_ref[...]
    hc = _rms(x, gc_ref[...]).astype(jnp.bfloat16)
    q = jnp.dot(hc, wq_ref[...], preferred_element_type=jnp.float32).astype(jnp.bfloat16)
    kv_w = MEM_HEADS * HEAD_DIM
    outs = []
    for h in range(MEM_HEADS):
        k = kv_ref[:, h * HEAD_DIM:(h + 1) * HEAD_DIM]
        v = kv_ref[:, kv_w + h * HEAD_DIM:kv_w + (h + 1) * HEAD_DIM]
        s = lax.dot_general(q[:, h * HEAD_DIM:(h + 1) * HEAD_DIM], k, NT_DIMS,
                            preferred_element_type=jnp.float32)
        m = jnp.max(s, axis=1, keepdims=True)
        p = jnp.exp2((s - m) * c)
        l = jnp.sum(p, axis=1, keepdims=True)
        o = jnp.dot(p.astype(jnp.bfloat16), v, preferred_element_type=jnp.float32) / l
        outs.append(o.astype(jnp.bfloat16))
    o = jnp.concatenate(outs, axis=1)
    x2 = x + jnp.dot(o, wo_ref[...], preferred_element_type=jnp.float32)
    x2_ref[...] = x2
    hm_ref[...] = _rms(x2, gm_ref[...]).astype(hm_ref.dtype)


def _cross_sublayer(x, kv, g_cross, w_cq, w_co, g_mlp, seq_len, mem_len, tq=256):
    n, d = x.shape
    qw = w_cq.shape[1]
    const = lambda i: (0, 0)
    return pl.pallas_call(
        _cross_kernel,
        out_shape=(jax.ShapeDtypeStruct((n, d), jnp.float32),
                   jax.ShapeDtypeStruct((n, d), jnp.bfloat16)),
        grid=(n // tq,),
        in_specs=[pl.BlockSpec((tq, d), lambda i: (i, 0)),
                  pl.BlockSpec((1, d), const),
                  pl.BlockSpec((d, qw), const),
                  pl.BlockSpec((mem_len, 2 * qw), lambda i: (i // (seq_len // tq), 0)),
                  pl.BlockSpec((qw, d), const),
                  pl.BlockSpec((1, d), const)],
        out_specs=(pl.BlockSpec((tq, d), lambda i: (i, 0)),
                   pl.BlockSpec((tq, d), lambda i: (i, 0))),
        compiler_params=_params("parallel"),
        name="cross_sublayer",
    )(x, g_cross.reshape(1, d), w_cq, kv, w_co, g_mlp.reshape(1, d))


def kernel(x, mem, ln_mix_g, w_in, moba_out_g, lambda_q1, lambda_k1, lambda_q2, lambda_k2,
           diff_subln_g, w_out, ln_cross_g, ln_mem_g, w_cq, w_ckv, w_co, ln_mlp_g,
           w_up, w_down, final_g):
    b, s, d = x.shape
    mem_len = mem.shape[1]
    depth = w_in.shape[0]
    mix_width = w_out.shape[1]
    moba_width = mix_width // 2
    moba_heads = moba_width // HEAD_DIM
    diff_heads = (mix_width - moba_width) // (2 * HEAD_DIM)
    bf = jnp.bfloat16

    xf = x.reshape(b * s, d)
    memf = mem.reshape(b * mem_len, d)
    for l in range(depth):
        lambda_init = 0.8 - 0.6 * math.exp(-0.3 * l)

        h = _rmsnorm(xf, ln_mix_g[l], bf)
        w_in_bf = w_in[l].astype(bf)
        qk = _inproj_qk(h, w_in_bf, s, moba_width).reshape(b, s, -1)
        bpg = moba_width // 1024
        vproj, w_out_bf, w_up_bf = _matmul(
            h, w_in_bf, bm=1024, bn=1024, out_dtype=bf, n_out=2 * moba_width,
            w_col_block=lambda j: (2 + 3 * (j // bpg)) * bpg + j % bpg,
            round_also=(w_out, w_up), layer=l, name="inproj_v")
        vproj = vproj.reshape(b, s, -1)
        merged = _moba_attention(qk, vproj, moba_out_g[l], moba_heads, mix_width)
        merged = _diff_attention(qk, vproj, merged, lambda_q1[l], lambda_k1[l], lambda_q2[l],
                                 lambda_k2[l], diff_subln_g[l], diff_heads, lambda_init)
        x1 = _matmul(merged.reshape(b * s, mix_width), w_out_bf, bm=1024, bn=1024,
                     out_dtype=jnp.float32, res=xf, name="outproj")

        mn = _rmsnorm(memf, ln_mem_g[l], bf)
        kv = _matmul(mn, w_ckv, layer=l, bm=b * mem_len, bn=512, out_dtype=bf, name="mem_kv")
        x2, hm = _cross_sublayer(x1, kv, ln_cross_g[l], w_cq[l].astype(bf), w_co[l].astype(bf),
                                 ln_mlp_g[l], s, mem_len)

        u, w_down_bf = _matmul(hm, w_up_bf, bm=1024, bn=1024, out_dtype=bf, act="relu2",
                               round_also=(w_down,), layer=l, name="mlp_up")
        xf = _matmul_ksplit(u, w_down_bf, x2, bm=1024, bn=1024, bk=4096, name="mlp_down")
    return _rmsnorm(xf, final_g, jnp.float32).reshape(b, s, d)
```

```python
import functools
import math

import jax
import jax.numpy as jnp
from jax import lax
from jax.experimental import pallas as pl
from jax.experimental.pallas import tpu as pltpu

HEAD_DIM = 128
MOBA_BLOCK = 256
MOBA_TOPK = 3
ROT_DIM = HEAD_DIM // 4
ROPE_THETA = 500000.0
MEM_HEADS = 4
EPS = 1e-5

VMEM_LIMIT_BYTES = 60 * 2**20
NEG_BIG = -1e30
LOG2E = 1.4426950408889634
QK_SCALE = HEAD_DIM ** -0.5 * LOG2E
NT_DIMS = (((1,), (1,)), ((), ()))
ATT_BLOCK = 256
MOBA_GROUPS = (4, 2)
DIFF_GROUPS = (2,)
MM_CHUNK = 256


def _params(*sem):
    return pltpu.CompilerParams(dimension_semantics=sem, vmem_limit_bytes=VMEM_LIMIT_BYTES)


def _rms(x, g):
    return x * lax.rsqrt(jnp.mean(x * x, axis=-1, keepdims=True) + EPS) * g


def _rmsnorm_kernel(x_ref, g_ref, o_ref):
    o_ref[...] = _rms(x_ref[...], g_ref[...]).astype(o_ref.dtype)


def _rmsnorm(x, g, out_dtype, bm=512):
    n, d = x.shape
    return pl.pallas_call(
        _rmsnorm_kernel,
        out_shape=jax.ShapeDtypeStruct((n, d), out_dtype),
        grid=(n // bm,),
        in_specs=[pl.BlockSpec((bm, d), lambda i: (i, 0)),
                  pl.BlockSpec((1, d), lambda i: (0, 0))],
        out_specs=pl.BlockSpec((bm, d), lambda i: (i, 0)),
        compiler_params=_params("parallel"),
        name="rmsnorm",
    )(x, g.reshape(1, d))


def _weight_spec(w, layer, bk, bn, index):
    if w.ndim == 2:
        return pl.BlockSpec((bk, bn), index)
    return pl.BlockSpec((None, bk, bn), lambda *g: (layer,) + tuple(index(*g)))


def _mm_kernel(*refs, act, has_res, n_side):
    a_ref, w_ref = refs[0], refs[1]
    res_ref = refs[2] if has_res else None
    side_in = refs[2 + has_res:2 + has_res + n_side]
    o_ref = refs[2 + has_res + n_side]
    side_out = refs[3 + has_res + n_side:]
    for src, dst in zip(side_in, side_out):
        dst[...] = src[...].astype(jnp.bfloat16)
    for c in range(o_ref.shape[1] // MM_CHUNK):
        cols = slice(c * MM_CHUNK, (c + 1) * MM_CHUNK)
        acc = jnp.dot(a_ref[...], w_ref[:, cols].astype(jnp.bfloat16),
                      preferred_element_type=jnp.float32)
        if act == "relu2":
            acc = jnp.square(jnp.maximum(acc, 0.0))
        if has_res:
            acc = acc + res_ref[:, cols]
        o_ref[:, cols] = acc.astype(o_ref.dtype)


def _matmul(a, w, *, bm, bn, out_dtype, act=None, res=None, layer=0, n_out=None,
            w_col_block=lambda j: j, round_also=(), name):
    m, kdim = a.shape
    n = w.shape[-1] if n_out is None else n_out
    nj = n // bn
    in_specs = [pl.BlockSpec((bm, kdim), lambda i, j: (i, 0)),
                _weight_spec(w, layer, kdim, bn, lambda i, j: (0, w_col_block(j)))]
    args = [a, w]
    if res is not None:
        in_specs.append(pl.BlockSpec((bm, bn), lambda i, j: (i, j)))
        args.append(res)
    out_shape = [jax.ShapeDtypeStruct((m, n), out_dtype)]
    out_specs = [pl.BlockSpec((bm, bn), lambda i, j: (i, j))]
    for side in round_also:
        _, rows, cols = side.shape
        slab = rows // ((m // bm) * nj)
        in_specs.append(pl.BlockSpec((None, slab, cols), lambda i, j: (layer, i * nj + j, 0)))
        args.append(side)
        out_shape.append(jax.ShapeDtypeStruct((rows, cols), jnp.bfloat16))
        out_specs.append(pl.BlockSpec((slab, cols), lambda i, j: (i * nj + j, 0)))
    out = pl.pallas_call(
        functools.partial(_mm_kernel, act=act, has_res=res is not None, n_side=len(round_also)),
        out_shape=out_shape,
        grid=(m // bm, nj),
        in_specs=in_specs,
        out_specs=out_specs,
        compiler_params=_params("parallel", "parallel"),
        name=name,
    )(*args)
    return out if round_also else out[0]


def _mm_ksplit_kernel(a_ref, w_ref, res_ref, o_ref):
    @pl.when(pl.program_id(2) == 0)
    def _():
        o_ref[...] = res_ref[...]

    for c in range(o_ref.shape[1] // MM_CHUNK):
        cols = slice(c * MM_CHUNK, (c + 1) * MM_CHUNK)
        o_ref[:, cols] = (jnp.dot(a_ref[...], w_ref[:, cols], preferred_element_type=jnp.float32)
                          + o_ref[:, cols])


def _matmul_ksplit(a, w, res, *, bm, bn, bk, name):
    m, kdim = a.shape
    n = w.shape[1]
    return pl.pallas_call(
        _mm_ksplit_kernel,
        out_shape=jax.ShapeDtypeStruct((m, n), jnp.float32),
        grid=(m // bm, n // bn, kdim // bk),
        in_specs=[pl.BlockSpec((bm, bk), lambda i, j, k: (i, k)),
                  pl.BlockSpec((bk, bn), lambda i, j, k: (k, j)),
                  pl.BlockSpec((bm, bn), lambda i, j, k: (i, j))],
        out_specs=pl.BlockSpec((bm, bn), lambda i, j, k: (i, j)),
        compiler_params=_params("parallel", "parallel", "arbitrary"),
        name=name,
    )(a, w, res)


def _inproj_kernel(h_ref, w_ref, cos_ref, sa_ref, sb_ref, o_ref, acc_ref):
    @pl.when(pl.program_id(0) == 0)
    def _():
        acc_ref[...] = jnp.zeros_like(acc_ref)

    half_rows = acc_ref.shape[0] // 2
    for r0 in (0, half_rows):
        rows = slice(r0, r0 + half_rows)
        cos, sa, sb = cos_ref[rows, :], sa_ref[rows, :], sb_ref[rows, :]
        for c in range(o_ref.shape[1] // HEAD_DIM):
            cols = slice(c * HEAD_DIM, (c + 1) * HEAD_DIM)
            a = acc_ref[rows, cols]
            r = (a * cos + pltpu.roll(a, HEAD_DIM - ROT_DIM // 2, 1) * sa
                 + pltpu.roll(a, ROT_DIM // 2, 1) * sb)
            o_ref[rows, cols] = r.astype(o_ref.dtype)
    acc_ref[...] = jnp.dot(h_ref[...], w_ref[...], preferred_element_type=jnp.float32)


def _rope_tables(seq_len):
    half = ROT_DIM // 2
    pos = jnp.arange(seq_len, dtype=jnp.float32)
    inv_freq = ROPE_THETA ** (-jnp.arange(0, ROT_DIM, 2, dtype=jnp.float32) / ROT_DIM)
    ang = pos[:, None] * inv_freq[None, :]
    cos, sin = jnp.cos(ang), jnp.sin(ang)
    z = lambda w: jnp.zeros((seq_len, w), jnp.float32)
    cos_t = jnp.concatenate([cos, cos, jnp.ones((seq_len, HEAD_DIM - ROT_DIM), jnp.float32)], axis=1)
    sin_a = jnp.concatenate([-sin, z(HEAD_DIM - half)], axis=1)
    sin_b = jnp.concatenate([z(half), sin, z(HEAD_DIM - ROT_DIM)], axis=1)
    return cos_t, sin_a, sin_b


def _inproj_qk(h, w, seq_len, group_width, bm=1024, bn=1024):
    m, kdim = h.shape
    bpg = group_width // bn
    nj = 4 * bpg
    n_blocks = (m // bm) * nj
    cos_t, sin_a, sin_b = (jnp.stack([t * QK_SCALE, t]) for t in _rope_tables(seq_len))

    mul = lambda t: jnp.minimum(t, n_blocks - 1)
    fin = lambda t: jnp.maximum(t - 1, 0)

    def w_block(t):
        out_group = (mul(t) % nj) // bpg
        return 0, (out_group + out_group // 2) * bpg + mul(t) % bpg

    tspec = pl.BlockSpec(
        (None, bm, HEAD_DIM),
        lambda t: (((fin(t) % nj) // bpg) % 2, (fin(t) // nj) % (seq_len // bm), 0))
    return pl.pallas_call(
        _inproj_kernel,
        out_shape=jax.ShapeDtypeStruct((m, 4 * group_width), jnp.bfloat16),
        grid=(n_blocks + 1,),
        in_specs=[pl.BlockSpec((bm, kdim), lambda t: (mul(t) // nj, 0)),
                  pl.BlockSpec((kdim, bn), w_block),
                  tspec, tspec, tspec],
        out_specs=pl.BlockSpec((bm, bn), lambda t: (fin(t) // nj, fin(t) % nj)),
        scratch_shapes=[pltpu.VMEM((bm, bn), jnp.float32)],
        compiler_params=_params("arbitrary"),
        name="inproj_qk_rope",
    )(h, w, cos_t, sin_a, sin_b)


VT_PAD = 16


def _scores(k, q):
    return lax.dot_general(k, q, NT_DIMS, preferred_element_type=jnp.float32)


def _transposed_values(v_blk):
    vt = v_blk.astype(jnp.float32).T.astype(jnp.bfloat16)
    row = lax.broadcasted_iota(jnp.int32, (VT_PAD, v_blk.shape[0]), 0)
    return jnp.concatenate([vt, jnp.where(row == 0, 1.0, 0.0).astype(jnp.bfloat16)], axis=0)


def _weights(x):
    return jnp.exp2(x).astype(jnp.bfloat16)


def _diag_state(s, vt):
    key = lax.broadcasted_iota(jnp.int32, s.shape, 0)
    qry = lax.broadcasted_iota(jnp.int32, s.shape, 1)
    s = jnp.where(key <= qry, s, NEG_BIG)
    m = jnp.max(s, axis=0, keepdims=True)
    return m, jnp.dot(vt, _weights(s - m), preferred_element_type=jnp.float32)


def _normalised(acc):
    v_dim = acc.shape[0] - VT_PAD
    return acc[:v_dim] / acc[v_dim:v_dim + 1]


def _update_state(pieces, biases, vts, state):
    m, acc = state
    m_new = m
    for s, b in zip(pieces, biases):
        m_new = jnp.maximum(m_new, jnp.max(s, axis=0, keepdims=True) + b)
    acc = jnp.exp2(m - m_new) * acc
    for s, b, vt in zip(pieces, biases, vts):
        acc = acc + jnp.dot(vt, _weights(s - (m_new - b)), preferred_element_type=jnp.float32)
    return m_new, acc


STALE_MAX_CAP = 40.0


def _update_state_stale(pieces, biases, vts, state):
    m, acc = state
    m_new = m
    for s, b, vt in zip(pieces, biases, vts):
        m_new = jnp.maximum(m_new, jnp.max(s, axis=0, keepdims=True) + b)
        acc = acc + jnp.dot(vt, _weights(s - (m - b)), preferred_element_type=jnp.float32)
    rise = m_new - m
    return (m_new, jnp.exp2(-rise) * acc), rise


def _attend_groups(n_blocks, group_sizes, group_inputs, states, fin_ref):
    def fast(inputs, carry):
        states, worst = carry
        new, rises = zip(*(_update_state_stale(*inp, st) for inp, st in zip(inputs, states)))
        return tuple(new), functools.reduce(jnp.maximum, rises + (worst,))

    def exact(inputs, states):
        return tuple(_update_state(*inp, st) for inp, st in zip(inputs, states))

    def fold(update, carry):
        first = 0
        for idx, size in enumerate(group_sizes):
            slack = size - 1 if idx == len(group_sizes) - 1 else 0
            trips = lax.div(n_blocks - first + slack, size)
            carry = lax.fori_loop(
                0, trips,
                lambda i, c, size=size, first=first: update(group_inputs(first + i * size, size), c),
                carry)
            first = first + trips * size
        return carry

    def store(states):
        for c, (_, acc) in enumerate(states):
            fin_ref[c] = acc

    done, worst = fold(fast, (states, jnp.zeros_like(states[0][0])))
    store(done)

    @pl.when(jnp.max(worst) > STALE_MAX_CAP)
    def _():
        store(fold(exact, states))


class _LaggedGrid:
    def __init__(self, n_steps, n_groups, n_blocks):
        self.n_steps, self.n_groups, self.n_blocks = n_steps, n_groups, n_blocks

    def _decode(self, item, index):
        per_batch = self.n_groups * self.n_blocks
        return index(item // per_batch, (item // self.n_blocks) % self.n_groups,
                     item % self.n_blocks)

    def cur(self, index):
        return lambda t: self._decode(jnp.minimum(t, self.n_steps - 1), index)

    def prev(self, index):
        return lambda t: self._decode(jnp.maximum(t - 1, 0), index)


def _key_rows(k_ref, first_block, n_blocks, cols=slice(None)):
    rows = n_blocks * ATT_BLOCK
    return k_ref[pl.ds(pl.multiple_of(first_block * ATT_BLOCK, ATT_BLOCK), rows), cols]


def _moba_kernel(q_ref, k_ref, v_ref, g_ref, o_ref, kmh_ref, kml_ref, vt_ref, bias_ref, fin_ref,
                 *, n_blocks, n_heads, n_steps):
    t = pl.program_id(0)
    qi = lax.rem(jnp.minimum(t, n_steps - 1), n_blocks)
    heads = [slice(h * HEAD_DIM, (h + 1) * HEAD_DIM) for h in range(n_heads)]

    @pl.when(t == 0)
    def _():
        fin_ref[...] = jnp.ones_like(fin_ref)

    @pl.when(qi == 0)
    def _():
        for h, cols in enumerate(heads):
            kf = k_ref[:, cols].astype(jnp.float32).reshape(n_blocks, MOBA_BLOCK, HEAD_DIM)
            km = jnp.sum(kf, axis=1) * (1.0 / MOBA_BLOCK)
            hi = km.astype(jnp.bfloat16)
            kmh_ref[h] = hi
            kml_ref[h] = (km - hi.astype(jnp.float32)).astype(jnp.bfloat16)
            for n in range(n_blocks):
                vt_ref[h * n_blocks + n] = _transposed_values(
                    v_ref[n * ATT_BLOCK:(n + 1) * ATT_BLOCK, cols])

    for h, cols in enumerate(heads):
        o_ref[:, cols] = _rms(_normalised(fin_ref[h]).T, g_ref[:, cols]).astype(o_ref.dtype)

    qs = [q_ref[:, cols] for cols in heads]
    for h, q in enumerate(qs):
        gate = (lax.dot_general(kmh_ref[h], q, NT_DIMS, preferred_element_type=jnp.float32)
                + lax.dot_general(kml_ref[h], q, NT_DIMS, preferred_element_type=jnp.float32))
        blk = lax.broadcasted_iota(jnp.int32, gate.shape, 0)
        past = blk < qi
        g = jnp.where(past, gate, -jnp.inf)
        sel = jnp.zeros(gate.shape, jnp.float32)
        for _ in range(MOBA_TOPK):
            mx = jnp.max(g, axis=0, keepdims=True)
            first = jnp.min(jnp.where(g == mx, blk, n_blocks), axis=0, keepdims=True)
            hit = blk == first
            sel = jnp.where(hit, jnp.where(past, 1.0, 0.0), sel)
            g = jnp.where(hit, -jnp.inf, g)
        bias_ref[h] = jnp.where(sel > 0.0, 0.0, NEG_BIG)

    diag = [_scores(_key_rows(k_ref, qi, 1, cols), q) for q, cols in zip(qs, heads)]
    states = tuple(_diag_state(s, vt_ref[h * n_blocks + qi]) for h, s in enumerate(diag))

    def group_inputs(first, size):
        scores = [_scores(_key_rows(k_ref, first, size, cols), q) for q, cols in zip(qs, heads)]
        inputs = []
        for h, s in enumerate(scores):
            pieces = [s[n * ATT_BLOCK:(n + 1) * ATT_BLOCK] for n in range(size)]
            biases = [bias_ref[h, pl.ds(first + n, 1), :] for n in range(size)]
            vts = [vt_ref[h * n_blocks + first + n] for n in range(size)]
            inputs.append((pieces, biases, vts))
        return inputs

    _attend_groups(qi, MOBA_GROUPS, group_inputs, states, fin_ref)


def _moba_attention(qk, vproj, out_g, n_heads, mix_width, heads_per_step=8):
    b, s, _ = qk.shape
    n_blocks = s // MOBA_BLOCK
    hp = heads_per_step
    w = hp * HEAD_DIM
    n_groups = n_heads // hp
    n_steps = b * n_groups * n_blocks
    grid = _LaggedGrid(n_steps, n_groups, n_blocks)
    return pl.pallas_call(
        functools.partial(_moba_kernel, n_blocks=n_blocks, n_heads=hp, n_steps=n_steps),
        out_shape=jax.ShapeDtypeStruct((b, s, mix_width), jnp.bfloat16),
        grid=(n_steps + 1,),
        in_specs=[pl.BlockSpec((None, MOBA_BLOCK, w), grid.cur(lambda bi, h, qi: (bi, qi, h))),
                  pl.BlockSpec((None, s, w), grid.cur(lambda bi, h, qi: (bi, 0, n_groups + h))),
                  pl.BlockSpec((None, s, w), grid.cur(lambda bi, h, qi: (bi, 0, h))),
                  pl.BlockSpec((None, 1, w), grid.prev(lambda bi, h, qi: (h, 0, 0)))],
        out_specs=pl.BlockSpec((None, MOBA_BLOCK, w), grid.prev(lambda bi, h, qi: (bi, qi, h))),
        scratch_shapes=[pltpu.VMEM((hp, n_blocks, HEAD_DIM), jnp.bfloat16),
                        pltpu.VMEM((hp, n_blocks, HEAD_DIM), jnp.bfloat16),
                        pltpu.VMEM((hp * n_blocks, HEAD_DIM + VT_PAD, MOBA_BLOCK), jnp.bfloat16),
                        pltpu.VMEM((hp, n_blocks, MOBA_BLOCK), jnp.float32),
                        pltpu.VMEM((hp, HEAD_DIM + VT_PAD, MOBA_BLOCK), jnp.float32)],
        compiler_params=_params("arbitrary"),
        name="moba_attention",
    )(qk, qk, vproj, out_g.reshape(n_groups, 1, w))


def _diff_kernel(lq1_ref, lk1_ref, lq2_ref, lk2_ref, q_ref, k_ref, v_ref, g_ref, merged_ref,
                 o_ref, vt_ref, fin_ref, *, lambda_init, n_heads, n_steps):
    del merged_ref
    t = pl.program_id(0)
    n_blocks = vt_ref.shape[0] // n_heads
    qi = lax.rem(jnp.minimum(t, n_steps - 1), n_blocks)
    vw = 2 * HEAD_DIM

    @pl.when(t == 0)
    def _():
        fin_ref[...] = jnp.ones_like(fin_ref)

    @pl.when(qi == 0)
    def _():
        for h in range(n_heads):
            for n in range(n_blocks):
                vt_ref[h * n_blocks + n] = _transposed_values(
                    v_ref[n * ATT_BLOCK:(n + 1) * ATT_BLOCK, h * vw:(h + 1) * vw])

    lam = (jnp.exp(jnp.sum(lq1_ref[...] * lk1_ref[...], axis=1, keepdims=True))
           - jnp.exp(jnp.sum(lq2_ref[...] * lk2_ref[...], axis=1, keepdims=True))
           + lambda_init)
    for h in range(n_heads):
        o = (_normalised(fin_ref[2 * h]) - lam * _normalised(fin_ref[2 * h + 1])).T
        o_ref[:, h * vw:(h + 1) * vw] = (_rms(o, g_ref[...]) * (1.0 - lambda_init)).astype(o_ref.dtype)

    subs = [slice(j * HEAD_DIM, (j + 1) * HEAD_DIM) for j in range(2 * n_heads)]
    qs = [q_ref[:, cols] for cols in subs]
    diag = [_scores(_key_rows(k_ref, qi, 1, cols), q) for q, cols in zip(qs, subs)]
    states = tuple(_diag_state(s, vt_ref[(j // 2) * n_blocks + qi]) for j, s in enumerate(diag))

    def group_inputs(first, size):
        scores = [_scores(_key_rows(k_ref, first, size, cols), q) for q, cols in zip(qs, subs)]
        biases = [jnp.where(first + n < qi, 0.0, NEG_BIG) for n in range(size)]
        inputs = []
        for j, s in enumerate(scores):
            vts = [vt_ref[(j // 2) * n_blocks + first + n] for n in range(size)]
            pieces = [s[n * ATT_BLOCK:(n + 1) * ATT_BLOCK] for n in range(size)]
            inputs.append((pieces, biases, vts))
        return inputs

    _attend_groups(qi, DIFF_GROUPS, group_inputs, states, fin_ref)


def _diff_attention(qk, vproj, merged, lq1, lk1, lq2, lk2, subln_g, n_heads, lambda_init,
                    heads_per_step=4):
    b, s, qk_width = qk.shape
    v_col0 = out_col0 = vproj.shape[2] // 2
    col0 = qk_width // 2
    hp = heads_per_step
    w = hp * 2 * HEAD_DIM
    n_groups = n_heads // hp
    tq = ATT_BLOCK
    n_steps = b * n_groups * (s // tq)
    grid = _LaggedGrid(n_steps, n_groups, s // tq)
    vec = lambda x: x.reshape(1, HEAD_DIM).astype(jnp.float32)
    vspec = pl.BlockSpec((1, HEAD_DIM), lambda t: (0, 0))
    qb, kb, vb, ob = col0 // w, col0 // w + n_groups, v_col0 // w, out_col0 // w
    return pl.pallas_call(
        functools.partial(_diff_kernel, lambda_init=lambda_init, n_heads=hp, n_steps=n_steps),
        out_shape=jax.ShapeDtypeStruct(merged.shape, merged.dtype),
        grid=(n_steps + 1,),
        in_specs=[vspec, vspec, vspec, vspec,
                  pl.BlockSpec((None, tq, w), grid.cur(lambda bi, h, qi: (bi, qi, qb + h))),
                  pl.BlockSpec((None, s, w), grid.cur(lambda bi, h, qi: (bi, 0, kb + h))),
                  pl.BlockSpec((None, s, w), grid.cur(lambda bi, h, qi: (bi, 0, vb + h))),
                  pl.BlockSpec((1, 2 * HEAD_DIM), lambda t: (0, 0)),
                  pl.BlockSpec(memory_space=pl.ANY)],
        out_specs=pl.BlockSpec((None, tq, w), grid.prev(lambda bi, h, qi: (bi, qi, ob + h))),
        scratch_shapes=[pltpu.VMEM((hp * (s // ATT_BLOCK), 2 * HEAD_DIM + VT_PAD, ATT_BLOCK),
                                   jnp.bfloat16),
                        pltpu.VMEM((2 * hp, 2 * HEAD_DIM + VT_PAD, ATT_BLOCK), jnp.float32)],
        input_output_aliases={8: 0},
        compiler_params=_params("arbitrary"),
        name="diff_attention",
    )(vec(lq1), vec(lk1), vec(lq2), vec(lk2), qk, qk, vproj,
      subln_g.reshape(1, 2 * HEAD_DIM), merged)


def _cross_kernel(x_ref, gc_ref, wq_ref, kv_ref, wo_ref, gm_ref, x2_ref, hm_ref):
    c = QK_SCALE
    x = x_ref[...]
    hc = _rms(x, gc_ref[...]).astype(jnp.bfloat16)
    q = jnp.dot(hc, wq_ref[...], preferred_element_type=jnp.float32).astype(jnp.bfloat16)
    kv_w = MEM_HEADS * HEAD_DIM
    outs = []
    for h in range(MEM_HEADS):
        k = kv_ref[:, h * HEAD_DIM:(h + 1) * HEAD_DIM]
        v = kv_ref[:, kv_w + h * HEAD_DIM:kv_w + (h + 1) * HEAD_DIM]
        s = lax.dot_general(q[:, h * HEAD_DIM:(h + 1) * HEAD_DIM], k, NT_DIMS,
                            preferred_element_type=jnp.float32)
        m = jnp.max(s, axis=1, keepdims=True)
        p = jnp.exp2((s - m) * c)
        l = jnp.sum(p, axis=1, keepdims=True)
        o = jnp.dot(p.astype(jnp.bfloat16), v, preferred_element_type=jnp.float32) / l
        outs.append(o.astype(jnp.bfloat16))
    o = jnp.concatenate(outs, axis=1)
    x2 = x + jnp.dot(o, wo_ref[...], preferred_element_type=jnp.float32)
    x2_ref[...] = x2
    hm_ref[...] = _rms(x2, gm_ref[...]).astype(hm_ref.dtype)


def _cross_sublayer(x, kv, g_cross, w_cq, w_co, g_mlp, seq_len, mem_len, tq=256):
    n, d = x.shape
    qw = w_cq.shape[1]
    const = lambda i: (0, 0)
    return pl.pallas_call(
        _cross_kernel,
        out_shape=(jax.ShapeDtypeStruct((n, d), jnp.float32),
                   jax.ShapeDtypeStruct((n, d), jnp.bfloat16)),
        grid=(n // tq,),
        in_specs=[pl.BlockSpec((tq, d), lambda i: (i, 0)),
                  pl.BlockSpec((1, d), const),
                  pl.BlockSpec((d, qw), const),
                  pl.BlockSpec((mem_len, 2 * qw), lambda i: (i // (seq_len // tq), 0)),
                  pl.BlockSpec((qw, d), const),
                  pl.BlockSpec((1, d), const)],
        out_specs=(pl.BlockSpec((tq, d), lambda i: (i, 0)),
                   pl.BlockSpec((tq, d), lambda i: (i, 0))),
        compiler_params=_params("parallel"),
        name="cross_sublayer",
    )(x, g_cross.reshape(1, d), w_cq, kv, w_co, g_mlp.reshape(1, d))


def kernel(x, mem, ln_mix_g, w_in, moba_out_g, lambda_q1, lambda_k1, lambda_q2, lambda_k2,
           diff_subln_g, w_out, ln_cross_g, ln_mem_g, w_cq, w_ckv, w_co, ln_mlp_g,
           w_up, w_down, final_g):
    b, s, d = x.shape
    mem_len = mem.shape[1]
    depth = w_in.shape[0]
    mix_width = w_out.shape[1]
    moba_width = mix_width // 2
    moba_heads = moba_width // HEAD_DIM
    diff_heads = (mix_width - moba_width) // (2 * HEAD_DIM)
    bf = jnp.bfloat16

    xf = x.reshape(b * s, d)
    memf = mem.reshape(b * mem_len, d)
    for l in range(depth):
        lambda_init = 0.8 - 0.6 * math.exp(-0.3 * l)

        h = _rmsnorm(xf, ln_mix_g[l], bf)
        w_in_bf = w_in[l].astype(bf)
        qk = _inproj_qk(h, w_in_bf, s, moba_width).reshape(b, s, -1)
        bpg = moba_width // 1024
        vproj, w_out_bf, w_up_bf = _matmul(
            h, w_in_bf, bm=1024, bn=1024, out_dtype=bf, n_out=2 * moba_width,
            w_col_block=lambda j: (2 + 3 * (j // bpg)) * bpg + j % bpg,
            round_also=(w_out, w_up), layer=l, name="inproj_v")
        vproj = vproj.reshape(b, s, -1)
        merged = _moba_attention(qk, vproj, moba_out_g[l], moba_heads, mix_width)
        merged = _diff_attention(qk, vproj, merged, lambda_q1[l], lambda_k1[l], lambda_q2[l],
                                 lambda_k2[l], diff_subln_g[l], diff_heads, lambda_init)
        x1 = _matmul(merged.reshape(b * s, mix_width), w_out_bf, bm=1024, bn=1024,
                     out_dtype=jnp.float32, res=xf, name="outproj")

        mn = _rmsnorm(memf, ln_mem_g[l], bf)
        kv = _matmul(mn, w_ckv, layer=l, bm=b * mem_len, bn=512, out_dtype=bf, name="mem_kv")
        x2, hm = _cross_sublayer(x1, kv, ln_cross_g[l], w_cq[l].astype(bf), w_co[l].astype(bf),
                                 ln_mlp_g[l], s, mem_len)

        u, w_down_bf = _matmul(hm, w_up_bf, bm=1024, bn=1024, out_dtype=bf, act="relu2",
                               round_also=(w_down,), layer=l, name="mlp_up")
        xf = _matmul_ksplit(u, w_down_bf, x2, bm=1024, bn=1024, bk=4096, name="mlp_down")
    return _rmsnorm(xf, final_g, jnp.float32).reshape(b, s, d)
```

```python
import functools
import math

import jax
import jax.numpy as jnp
from jax import lax
from jax.experimental import pallas as pl
from jax.experimental.pallas import tpu as pltpu

HEAD_DIM = 128
MOBA_BLOCK = 256
MOBA_TOPK = 3
ROT_DIM = HEAD_DIM // 4
ROPE_THETA = 500000.0
MEM_HEADS = 4
EPS = 1e-5

VMEM_LIMIT_BYTES = 60 * 2**20
NEG_BIG = -1e30
LOG2E = 1.4426950408889634
QK_SCALE = HEAD_DIM ** -0.5 * LOG2E
NT_DIMS = (((1,), (1,)), ((), ()))
ATT_BLOCK = 256
MOBA_GROUPS = (4, 2)
DIFF_GROUPS = (2,)
MM_CHUNK = 256


def _params(*sem):
    return pltpu.CompilerParams(dimension_semantics=sem, vmem_limit_bytes=VMEM_LIMIT_BYTES)


def _rms(x, g):
    return x * lax.rsqrt(jnp.mean(x * x, axis=-1, keepdims=True) + EPS) * g


def _rmsnorm_kernel(x_ref, g_ref, o_ref):
    o_ref[...] = _rms(x_ref[...], g_ref[...]).astype(o_ref.dtype)


def _rmsnorm(x, g, out_dtype, bm=512):
    n, d = x.shape
    return pl.pallas_call(
        _rmsnorm_kernel,
        out_shape=jax.ShapeDtypeStruct((n, d), out_dtype),
        grid=(n // bm,),
        in_specs=[pl.BlockSpec((bm, d), lambda i: (i, 0)),
                  pl.BlockSpec((1, d), lambda i: (0, 0))],
        out_specs=pl.BlockSpec((bm, d), lambda i: (i, 0)),
        compiler_params=_params("parallel"),
        name="rmsnorm",
    )(x, g.reshape(1, d))


def _weight_spec(w, layer, bk, bn, index):
    if w.ndim == 2:
        return pl.BlockSpec((bk, bn), index)
    return pl.BlockSpec((None, bk, bn), lambda *g: (layer,) + tuple(index(*g)))


def _mm_kernel(*refs, n_a, act, has_res, n_side):
    a_refs, w_ref = refs[:n_a], refs[n_a]
    refs = refs[n_a + 1:]
    res_ref = refs[0] if has_res else None
    side_in = refs[has_res:has_res + n_side]
    o_ref = refs[has_res + n_side]
    side_out = refs[1 + has_res + n_side:]
    for src, dst in zip(side_in, side_out):
        dst[...] = src[...].astype(jnp.bfloat16)
    a = jnp.concatenate([r[...] for r in a_refs], axis=1)
    for c in range(o_ref.shape[1] // MM_CHUNK):
        cols = slice(c * MM_CHUNK, (c + 1) * MM_CHUNK)
        acc = jnp.dot(a, w_ref[:, cols].astype(jnp.bfloat16), preferred_element_type=jnp.float32)
        if act == "relu2":
            acc = jnp.square(jnp.maximum(acc, 0.0))
        if has_res:
            acc = acc + res_ref[:, cols]
        o_ref[:, cols] = acc.astype(o_ref.dtype)


def _matmul(a, w, *, bm, bn, out_dtype, act=None, res=None, layer=0, n_out=None,
            w_col_block=lambda j: j, round_also=(), name):
    a_parts = a if isinstance(a, tuple) else (a,)
    m = a_parts[0].shape[0]
    kdim = sum(p.shape[1] for p in a_parts)
    n = w.shape[-1] if n_out is None else n_out
    nj = n // bn
    in_specs = [pl.BlockSpec((bm, p.shape[1]), lambda i, j: (i, 0)) for p in a_parts]
    in_specs.append(_weight_spec(w, layer, kdim, bn, lambda i, j: (0, w_col_block(j))))
    args = [*a_parts, w]
    if res is not None:
        in_specs.append(pl.BlockSpec((bm, bn), lambda i, j: (i, j)))
        args.append(res)
    out_shape = [jax.ShapeDtypeStruct((m, n), out_dtype)]
    out_specs = [pl.BlockSpec((bm, bn), lambda i, j: (i, j))]
    for side in round_also:
        _, rows, cols = side.shape
        slab = rows // ((m // bm) * nj)
        in_specs.append(pl.BlockSpec((None, slab, cols), lambda i, j: (layer, i * nj + j, 0)))
        args.append(side)
        out_shape.append(jax.ShapeDtypeStruct((rows, cols), jnp.bfloat16))
        out_specs.append(pl.BlockSpec((slab, cols), lambda i, j: (i * nj + j, 0)))
    out = pl.pallas_call(
        functools.partial(_mm_kernel, n_a=len(a_parts), act=act, has_res=res is not None,
                          n_side=len(round_also)),
        out_shape=out_shape,
        grid=(m // bm, nj),
        in_specs=in_specs,
        out_specs=out_specs,
        compiler_params=_params("parallel", "parallel"),
        name=name,
    )(*args)
    return out if round_also else out[0]


def _mm_ksplit_kernel(a_ref, w_ref, res_ref, o_ref):
    @pl.when(pl.program_id(2) == 0)
    def _():
        o_ref[...] = res_ref[...]

    for c in range(o_ref.shape[1] // MM_CHUNK):
        cols = slice(c * MM_CHUNK, (c + 1) * MM_CHUNK)
        o_ref[:, cols] = (jnp.dot(a_ref[...], w_ref[:, cols], preferred_element_type=jnp.float32)
                          + o_ref[:, cols])


def _matmul_ksplit(a, w, res, *, bm, bn, bk, name):
    m, kdim = a.shape
    n = w.shape[1]
    return pl.pallas_call(
        _mm_ksplit_kernel,
        out_shape=jax.ShapeDtypeStruct((m, n), jnp.float32),
        grid=(m // bm, n // bn, kdim // bk),
        in_specs=[pl.BlockSpec((bm, bk), lambda i, j, k: (i, k)),
                  pl.BlockSpec((bk, bn), lambda i, j, k: (k, j)),
                  pl.BlockSpec((bm, bn), lambda i, j, k: (i, j))],
        out_specs=pl.BlockSpec((bm, bn), lambda i, j, k: (i, j)),
        compiler_params=_params("parallel", "parallel", "arbitrary"),
        name=name,
    )(a, w, res)


def _inproj_kernel(h_ref, w_ref, cos_ref, sa_ref, sb_ref, o_ref, acc_ref):
    @pl.when(pl.program_id(0) == 0)
    def _():
        acc_ref[...] = jnp.zeros_like(acc_ref)

    half_rows = acc_ref.shape[0] // 2
    for r0 in (0, half_rows):
        rows = slice(r0, r0 + half_rows)
        cos, sa, sb = cos_ref[rows, :], sa_ref[rows, :], sb_ref[rows, :]
        for c in range(o_ref.shape[1] // HEAD_DIM):
            cols = slice(c * HEAD_DIM, (c + 1) * HEAD_DIM)
            a = acc_ref[rows, cols]
            r = (a * cos + pltpu.roll(a, HEAD_DIM - ROT_DIM // 2, 1) * sa
                 + pltpu.roll(a, ROT_DIM // 2, 1) * sb)
            o_ref[rows, cols] = r.astype(o_ref.dtype)
    acc_ref[...] = jnp.dot(h_ref[...], w_ref[...], preferred_element_type=jnp.float32)


def _rope_tables(seq_len):
    half = ROT_DIM // 2
    pos = jnp.arange(seq_len, dtype=jnp.float32)
    inv_freq = ROPE_THETA ** (-jnp.arange(0, ROT_DIM, 2, dtype=jnp.float32) / ROT_DIM)
    ang = pos[:, None] * inv_freq[None, :]
    cos, sin = jnp.cos(ang), jnp.sin(ang)
    z = lambda w: jnp.zeros((seq_len, w), jnp.float32)
    cos_t = jnp.concatenate([cos, cos, jnp.ones((seq_len, HEAD_DIM - ROT_DIM), jnp.float32)], axis=1)
    sin_a = jnp.concatenate([-sin, z(HEAD_DIM - half)], axis=1)
    sin_b = jnp.concatenate([z(half), sin, z(HEAD_DIM - ROT_DIM)], axis=1)
    return cos_t, sin_a, sin_b


def _inproj_qk(h, w, seq_len, group_width, bm=1024, bn=1024):
    m, kdim = h.shape
    bpg = group_width // bn
    nj = 4 * bpg
    n_blocks = (m // bm) * nj
    cos_t, sin_a, sin_b = (jnp.stack([t * QK_SCALE, t]) for t in _rope_tables(seq_len))

    mul = lambda t: jnp.minimum(t, n_blocks - 1)
    fin = lambda t: jnp.maximum(t - 1, 0)

    def w_block(t):
        out_group = (mul(t) % nj) // bpg
        return 0, (out_group + out_group // 2) * bpg + mul(t) % bpg

    tspec = pl.BlockSpec(
        (None, bm, HEAD_DIM),
        lambda t: (((fin(t) % nj) // bpg) % 2, (fin(t) // nj) % (seq_len // bm), 0))
    return pl.pallas_call(
        _inproj_kernel,
        out_shape=jax.ShapeDtypeStruct((m, 4 * group_width), jnp.bfloat16),
        grid=(n_blocks + 1,),
        in_specs=[pl.BlockSpec((bm, kdim), lambda t: (mul(t) // nj, 0)),
                  pl.BlockSpec((kdim, bn), w_block),
                  tspec, tspec, tspec],
        out_specs=pl.BlockSpec((bm, bn), lambda t: (fin(t) // nj, fin(t) % nj)),
        scratch_shapes=[pltpu.VMEM((bm, bn), jnp.float32)],
        compiler_params=_params("arbitrary"),
        name="inproj_qk_rope",
    )(h, w, cos_t, sin_a, sin_b)


VT_PAD = 16


def _scores(k, q):
    return lax.dot_general(k, q, NT_DIMS, preferred_element_type=jnp.float32)


def _transposed_values(v_blk):
    vt = v_blk.astype(jnp.float32).T.astype(jnp.bfloat16)
    row = lax.broadcasted_iota(jnp.int32, (VT_PAD, v_blk.shape[0]), 0)
    return jnp.concatenate([vt, jnp.where(row == 0, 1.0, 0.0).astype(jnp.bfloat16)], axis=0)


def _weights(x):
    return jnp.exp2(x).astype(jnp.bfloat16)


def _diag_state(s, vt):
    key = lax.broadcasted_iota(jnp.int32, s.shape, 0)
    qry = lax.broadcasted_iota(jnp.int32, s.shape, 1)
    s = jnp.where(key <= qry, s, NEG_BIG)
    m = jnp.max(s, axis=0, keepdims=True)
    return m, jnp.dot(vt, _weights(s - m), preferred_element_type=jnp.float32)


def _normalised(acc):
    v_dim = acc.shape[0] - VT_PAD
    return acc[:v_dim] / acc[v_dim:v_dim + 1]


def _update_state(pieces, biases, vts, state):
    m, acc = state
    m_new = m
    for s, b in zip(pieces, biases):
        m_new = jnp.maximum(m_new, jnp.max(s, axis=0, keepdims=True) + b)
    acc = jnp.exp2(m - m_new) * acc
    for s, b, vt in zip(pieces, biases, vts):
        acc = acc + jnp.dot(vt, _weights(s - (m_new - b)), preferred_element_type=jnp.float32)
    return m_new, acc


STALE_MAX_CAP = 40.0


def _update_state_stale(pieces, biases, vts, state):
    m, acc = state
    m_new = m
    for s, b, vt in zip(pieces, biases, vts):
        m_new = jnp.maximum(m_new, jnp.max(s, axis=0, keepdims=True) + b)
        acc = acc + jnp.dot(vt, _weights(s - (m - b)), preferred_element_type=jnp.float32)
    rise = m_new - m
    return (m_new, jnp.exp2(-rise) * acc), rise


def _attend_groups(n_blocks, group_sizes, group_inputs, states, fin_ref):
    def fast(inputs, carry):
        states, worst = carry
        new, rises = zip(*(_update_state_stale(*inp, st) for inp, st in zip(inputs, states)))
        return tuple(new), functools.reduce(jnp.maximum, rises + (worst,))

    def exact(inputs, states):
        return tuple(_update_state(*inp, st) for inp, st in zip(inputs, states))

    def fold(update, carry):
        first = 0
        for idx, size in enumerate(group_sizes):
            slack = size - 1 if idx == len(group_sizes) - 1 else 0
            trips = lax.div(n_blocks - first + slack, size)
            carry = lax.fori_loop(
                0, trips,
                lambda i, c, size=size, first=first: update(group_inputs(first + i * size, size), c),
                carry)
            first = first + trips * size
        return carry

    def store(states):
        for c, (_, acc) in enumerate(states):
            fin_ref[c] = acc

    done, worst = fold(fast, (states, jnp.zeros_like(states[0][0])))
    store(done)

    @pl.when(jnp.max(worst) > STALE_MAX_CAP)
    def _():
        store(fold(exact, states))


class _LaggedGrid:
    def __init__(self, n_steps, n_groups, n_blocks):
        self.n_steps, self.n_groups, self.n_blocks = n_steps, n_groups, n_blocks

    def _decode(self, item, index):
        per_batch = self.n_groups * self.n_blocks
        return index(item // per_batch, (item // self.n_blocks) % self.n_groups,
                     item % self.n_blocks)

    def cur(self, index):
        return lambda t: self._decode(jnp.minimum(t, self.n_steps - 1), index)

    def prev(self, index):
        return lambda t: self._decode(jnp.maximum(t - 1, 0), index)


def _key_rows(k_ref, first_block, n_blocks, cols=slice(None)):
    rows = n_blocks * ATT_BLOCK
    return k_ref[pl.ds(pl.multiple_of(first_block * ATT_BLOCK, ATT_BLOCK), rows), cols]


def _moba_kernel(q_ref, k_ref, v_ref, g_ref, o_ref, kmh_ref, kml_ref, vt_ref, bias_ref, fin_ref,
                 *, n_blocks, n_heads, n_steps):
    t = pl.program_id(0)
    qi = lax.rem(jnp.minimum(t, n_steps - 1), n_blocks)
    heads = [slice(h * HEAD_DIM, (h + 1) * HEAD_DIM) for h in range(n_heads)]

    @pl.when(t == 0)
    def _():
        fin_ref[...] = jnp.ones_like(fin_ref)

    @pl.when(qi == 0)
    def _():
        for h, cols in enumerate(heads):
            kf = k_ref[:, cols].astype(jnp.float32).reshape(n_blocks, MOBA_BLOCK, HEAD_DIM)
            km = jnp.sum(kf, axis=1) * (1.0 / MOBA_BLOCK)
            hi = km.astype(jnp.bfloat16)
            kmh_ref[h] = hi
            kml_ref[h] = (km - hi.astype(jnp.float32)).astype(jnp.bfloat16)
            for n in range(n_blocks):
                vt_ref[h * n_blocks + n] = _transposed_values(
                    v_ref[n * ATT_BLOCK:(n + 1) * ATT_BLOCK, cols])

    for h, cols in enumerate(heads):
        o_ref[:, cols] = _rms(_normalised(fin_ref[h]).T, g_ref[:, cols]).astype(o_ref.dtype)

    qs = [q_ref[:, cols] for cols in heads]
    for h, q in enumerate(qs):
        gate = (lax.dot_general(kmh_ref[h], q, NT_DIMS, preferred_element_type=jnp.float32)
                + lax.dot_general(kml_ref[h], q, NT_DIMS, preferred_element_type=jnp.float32))
        blk = lax.broadcasted_iota(jnp.int32, gate.shape, 0)
        past = blk < qi
        g = jnp.where(past, gate, -jnp.inf)
        sel = jnp.zeros(gate.shape, jnp.float32)
        for _ in range(MOBA_TOPK):
            mx = jnp.max(g, axis=0, keepdims=True)
            first = jnp.min(jnp.where(g == mx, blk, n_blocks), axis=0, keepdims=True)
            hit = blk == first
            sel = jnp.where(hit, jnp.where(past, 1.0, 0.0), sel)
            g = jnp.where(hit, -jnp.inf, g)
        bias_ref[h] = jnp.where(sel > 0.0, 0.0, NEG_BIG)

    diag = [_scores(_key_rows(k_ref, qi, 1, cols), q) for q, cols in zip(qs, heads)]
    states = tuple(_diag_state(s, vt_ref[h * n_blocks + qi]) for h, s in enumerate(diag))

    def group_inputs(first, size):
        scores = [_scores(_key_rows(k_ref, first, size, cols), q) for q, cols in zip(qs, heads)]
        inputs = []
        for h, s in enumerate(scores):
            pieces = [s[n * ATT_BLOCK:(n + 1) * ATT_BLOCK] for n in range(size)]
            biases = [bias_ref[h, pl.ds(first + n, 1), :] for n in range(size)]
            vts = [vt_ref[h * n_blocks + first + n] for n in range(size)]
            inputs.append((pieces, biases, vts))
        return inputs

    _attend_groups(qi, MOBA_GROUPS, group_inputs, states, fin_ref)


def _moba_attention(qk, vproj, out_g, n_heads, heads_per_step=8):
    b, s, _ = qk.shape
    n_blocks = s // MOBA_BLOCK
    hp = heads_per_step
    w = hp * HEAD_DIM
    n_groups = n_heads // hp
    n_steps = b * n_groups * n_blocks
    grid = _LaggedGrid(n_steps, n_groups, n_blocks)
    return pl.pallas_call(
        functools.partial(_moba_kernel, n_blocks=n_blocks, n_heads=hp, n_steps=n_steps),
        out_shape=jax.ShapeDtypeStruct((b, s, n_heads * HEAD_DIM), jnp.bfloat16),
        grid=(n_steps + 1,),
        in_specs=[pl.BlockSpec((None, MOBA_BLOCK, w), grid.cur(lambda bi, h, qi: (bi, qi, h))),
                  pl.BlockSpec((None, s, w), grid.cur(lambda bi, h, qi: (bi, 0, n_groups + h))),
                  pl.BlockSpec((None, s, w), grid.cur(lambda bi, h, qi: (bi, 0, h))),
                  pl.BlockSpec((None, 1, w), grid.prev(lambda bi, h, qi: (h, 0, 0)))],
        out_specs=pl.BlockSpec((None, MOBA_BLOCK, w), grid.prev(lambda bi, h, qi: (bi, qi, h))),
        scratch_shapes=[pltpu.VMEM((hp, n_blocks, HEAD_DIM), jnp.bfloat16),
                        pltpu.VMEM((hp, n_blocks, HEAD_DIM), jnp.bfloat16),
                        pltpu.VMEM((hp * n_blocks, HEAD_DIM + VT_PAD, MOBA_BLOCK), jnp.bfloat16),
                        pltpu.VMEM((hp, n_blocks, MOBA_BLOCK), jnp.float32),
                        pltpu.VMEM((hp, HEAD_DIM + VT_PAD, MOBA_BLOCK), jnp.float32)],
        compiler_params=_params("arbitrary"),
        name="moba_attention",
    )(qk, qk, vproj, out_g.reshape(n_groups, 1, w))


def _diff_kernel(lq1_ref, lk1_ref, lq2_ref, lk2_ref, q_ref, k_ref, v_ref, g_ref,
                 o_ref, vt_ref, fin_ref, *, lambda_init, n_heads, n_steps):
    t = pl.program_id(0)
    n_blocks = vt_ref.shape[0] // n_heads
    qi = lax.rem(jnp.minimum(t, n_steps - 1), n_blocks)
    vw = 2 * HEAD_DIM

    @pl.when(t == 0)
    def _():
        fin_ref[...] = jnp.ones_like(fin_ref)

    @pl.when(qi == 0)
    def _():
        for h in range(n_heads):
            for n in range(n_blocks):
                vt_ref[h * n_blocks + n] = _transposed_values(
                    v_ref[n * ATT_BLOCK:(n + 1) * ATT_BLOCK, h * vw:(h + 1) * vw])

    lam = (jnp.exp(jnp.sum(lq1_ref[...] * lk1_ref[...], axis=1, keepdims=True))
           - jnp.exp(jnp.sum(lq2_ref[...] * lk2_ref[...], axis=1, keepdims=True))
           + lambda_init)
    for h in range(n_heads):
        o = (_normalised(fin_ref[2 * h]) - lam * _normalised(fin_ref[2 * h + 1])).T
        o_ref[:, h * vw:(h + 1) * vw] = (_rms(o, g_ref[...]) * (1.0 - lambda_init)).astype(o_ref.dtype)

    subs = [slice(j * HEAD_DIM, (j + 1) * HEAD_DIM) for j in range(2 * n_heads)]
    qs = [q_ref[:, cols] for cols in subs]
    diag = [_scores(_key_rows(k_ref, qi, 1, cols), q) for q, cols in zip(qs, subs)]
    states = tuple(_diag_state(s, vt_ref[(j // 2) * n_blocks + qi]) for j, s in enumerate(diag))

    def group_inputs(first, size):
        scores = [_scores(_key_rows(k_ref, first, size, cols), q) for q, cols in zip(qs, subs)]
        biases = [jnp.where(first + n < qi, 0.0, NEG_BIG) for n in range(size)]
        inputs = []
        for j, s in enumerate(scores):
            vts = [vt_ref[(j // 2) * n_blocks + first + n] for n in range(size)]
            pieces = [s[n * ATT_BLOCK:(n + 1) * ATT_BLOCK] for n in range(size)]
            inputs.append((pieces, biases, vts))
        return inputs

    _attend_groups(qi, DIFF_GROUPS, group_inputs, states, fin_ref)


def _diff_attention(qk, vproj, lq1, lk1, lq2, lk2, subln_g, n_heads, lambda_init,
                    heads_per_step=4):
    b, s, qk_width = qk.shape
    v_col0 = vproj.shape[2] // 2
    col0 = qk_width // 2
    hp = heads_per_step
    w = hp * 2 * HEAD_DIM
    n_groups = n_heads // hp
    tq = ATT_BLOCK
    n_steps = b * n_groups * (s // tq)
    grid = _LaggedGrid(n_steps, n_groups, s // tq)
    vec = lambda x: x.reshape(1, HEAD_DIM).astype(jnp.float32)
    vspec = pl.BlockSpec((1, HEAD_DIM), lambda t: (0, 0))
    qb, kb, vb = col0 // w, col0 // w + n_groups, v_col0 // w
    return pl.pallas_call(
        functools.partial(_diff_kernel, lambda_init=lambda_init, n_heads=hp, n_steps=n_steps),
        out_shape=jax.ShapeDtypeStruct((b, s, n_heads * 2 * HEAD_DIM), jnp.bfloat16),
        grid=(n_steps + 1,),
        in_specs=[vspec, vspec, vspec, vspec,
                  pl.BlockSpec((None, tq, w), grid.cur(lambda bi, h, qi: (bi, qi, qb + h))),
                  pl.BlockSpec((None, s, w), grid.cur(lambda bi, h, qi: (bi, 0, kb + h))),
                  pl.BlockSpec((None, s, w), grid.cur(lambda bi, h, qi: (bi, 0, vb + h))),
                  pl.BlockSpec((1, 2 * HEAD_DIM), lambda t: (0, 0))],
        out_specs=pl.BlockSpec((None, tq, w), grid.prev(lambda bi, h, qi: (bi, qi, h))),
        scratch_shapes=[pltpu.VMEM((hp * (s // ATT_BLOCK), 2 * HEAD_DIM + VT_PAD, ATT_BLOCK),
                                   jnp.bfloat16),
                        pltpu.VMEM((2 * hp, 2 * HEAD_DIM + VT_PAD, ATT_BLOCK), jnp.float32)],
        compiler_params=_params("arbitrary"),
        name="diff_attention",
    )(vec(lq1), vec(lk1), vec(lq2), vec(lk2), qk, qk, vproj, subln_g.reshape(1, 2 * HEAD_DIM))


def _cross_kernel(x_ref, gc_ref, wq_ref, kv_ref, wo_ref, gm_ref, x2_ref, hm_ref):
    c = QK_SCALE
    x = x_ref[...]
    hc = _rms(x, gc_ref[...]).astype(jnp.bfloat16)
    q = jnp.dot(hc, wq_ref[...], preferred_element_type=jnp.float32).astype(jnp.bfloat16)
    kv_w = MEM_HEADS * HEAD_DIM
    outs = []
    for h in range(MEM_HEADS):
        k = kv_ref[:, h * HEAD_DIM:(h + 1) * HEAD_DIM]
        v = kv_ref[:, kv_w + h * HEAD_DIM:kv_w + (h + 1) * HEAD_DIM]
        s = lax.dot_general(q[:, h * HEAD_DIM:(h + 1) * HEAD_DIM], k, NT_DIMS,
                            preferred_element_type=jnp.float32)
        m = jnp.max(s, axis=1, keepdims=True)
        p = jnp.exp2((s - m) * c)
        l = jnp.sum(p, axis=1, keepdims=True)
        o = jnp.dot(p.astype(jnp.bfloat16), v, preferred_element_type=jnp.float32) / l
        outs.append(o.astype(jnp.bfloat16))
    o = jnp.concatenate(outs, axis=1)
    x2 = x + jnp.dot(o, wo_ref[...], preferred_element_type=jnp.float32)
    x2_ref[...] = x2
    hm_ref[...] = _rms(x2, gm_ref[...]).astype(hm_ref.dtype)


def _cross_sublayer(x, kv, g_cross, w_cq, w_co, g_mlp, seq_len, mem_len, tq=256):
    n, d = x.shape
    qw = w_cq.shape[1]
    const = lambda i: (0, 0)
    return pl.pallas_call(
        _cross_kernel,
        out_shape=(jax.ShapeDtypeStruct((n, d), jnp.float32),
                   jax.ShapeDtypeStruct((n, d), jnp.bfloat16)),
        grid=(n // tq,),
        in_specs=[pl.BlockSpec((tq, d), lambda i: (i, 0)),
                  pl.BlockSpec((1, d), const),
                  pl.BlockSpec((d, qw), const),
                  pl.BlockSpec((mem_len, 2 * qw), lambda i: (i // (seq_len // tq), 0)),
                  pl.BlockSpec((qw, d), const),
                  pl.BlockSpec((1, d), const)],
        out_specs=(pl.BlockSpec((tq, d), lambda i: (i, 0)),
                   pl.BlockSpec((tq, d), lambda i: (i, 0))),
        compiler_params=_params("parallel"),
        name="cross_sublayer",
    )(x, g_cross.reshape(1, d), w_cq, kv, w_co, g_mlp.reshape(1, d))


def kernel(x, mem, ln_mix_g, w_in, moba_out_g, lambda_q1, lambda_k1, lambda_q2, lambda_k2,
           diff_subln_g, w_out, ln_cross_g, ln_mem_g, w_cq, w_ckv, w_co, ln_mlp_g,
           w_up, w_down, final_g):
    b, s, d = x.shape
    mem_len = mem.shape[1]
    depth = w_in.shape[0]
    mix_width = w_out.shape[1]
    moba_width = mix_width // 2
    moba_heads = moba_width // HEAD_DIM
    diff_heads = (mix_width - moba_width) // (2 * HEAD_DIM)
    bf = jnp.bfloat16

    xf = x.reshape(b * s, d)
    memf = mem.reshape(b * mem_len, d)
    for l in range(depth):
        lambda_init = 0.8 - 0.6 * math.exp(-0.3 * l)

        h = _rmsnorm(xf, ln_mix_g[l], bf)
        w_in_bf = w_in[l].astype(bf)
        qk = _inproj_qk(h, w_in_bf, s, moba_width).reshape(b, s, -1)
        bpg = moba_width // 1024
        vproj, w_out_bf, w_up_bf = _matmul(
            h, w_in_bf, bm=1024, bn=1024, out_dtype=bf, n_out=2 * moba_width,
            w_col_block=lambda j: (2 + 3 * (j // bpg)) * bpg + j % bpg,
            round_also=(w_out, w_up), layer=l, name="inproj_v")
        vproj = vproj.reshape(b, s, -1)
        mo = _moba_attention(qk, vproj, moba_out_g[l], moba_heads)
        do = _diff_attention(qk, vproj, lambda_q1[l], lambda_k1[l], lambda_q2[l], lambda_k2[l],
                             diff_subln_g[l], diff_heads, lambda_init)
        x1 = _matmul((mo.reshape(b * s, -1), do.reshape(b * s, -1)), w_out_bf, bm=1024, bn=1024,
                     out_dtype=jnp.float32, res=xf, name="outproj")

        mn = _rmsnorm(memf, ln_mem_g[l], bf)
        kv = _matmul(mn, w_ckv, layer=l, bm=b * mem_len, bn=512, out_dtype=bf, name="mem_kv")
        x2, hm = _cross_sublayer(x1, kv, ln_cross_g[l], w_cq[l].astype(bf), w_co[l].astype(bf),
                                 ln_mlp_g[l], s, mem_len)

        u, w_down_bf = _matmul(hm, w_up_bf, bm=1024, bn=1024, out_dtype=bf, act="relu2",
                               round_also=(w_down,), layer=l, name="mlp_up")
        xf = _matmul_ksplit(u, w_down_bf, x2, bm=1024, bn=1024, bk=4096, name="mlp_down")
    return _rmsnorm(xf, final_g, jnp.float32).reshape(b, s, d)
```

```python
import functools
import math

import jax
import jax.numpy as jnp
from jax import lax
from jax.experimental import pallas as pl
from jax.experimental.pallas import tpu as pltpu

HEAD_DIM = 128
MOBA_BLOCK = 256
MOBA_TOPK = 3
ROT_DIM = HEAD_DIM // 4
ROPE_THETA = 500000.0
MEM_HEADS = 4
EPS = 1e-5

VMEM_LIMIT_BYTES = 60 * 2**20
NEG_BIG = -1e30
LOG2E = 1.4426950408889634
QK_SCALE = HEAD_DIM ** -0.5 * LOG2E
NT_DIMS = (((1,), (1,)), ((), ()))
ATT_BLOCK = 256
MOBA_GROUPS = (4, 2)
DIFF_GROUPS = (4, 2)
MM_CHUNK = 256


def _params(*sem):
    return pltpu.CompilerParams(dimension_semantics=sem, vmem_limit_bytes=VMEM_LIMIT_BYTES)


def _rms(x, g):
    return x * lax.rsqrt(jnp.mean(x * x, axis=-1, keepdims=True) + EPS) * g


def _rmsnorm_kernel(x_ref, g_ref, o_ref):
    o_ref[...] = _rms(x_ref[...], g_ref[...]).astype(o_ref.dtype)


def _rmsnorm(x, g, out_dtype, bm=512):
    n, d = x.shape
    return pl.pallas_call(
        _rmsnorm_kernel,
        out_shape=jax.ShapeDtypeStruct((n, d), out_dtype),
        grid=(n // bm,),
        in_specs=[pl.BlockSpec((bm, d), lambda i: (i, 0)),
                  pl.BlockSpec((1, d), lambda i: (0, 0))],
        out_specs=pl.BlockSpec((bm, d), lambda i: (i, 0)),
        compiler_params=_params("parallel"),
        name="rmsnorm",
    )(x, g.reshape(1, d))


def _weight_spec(w, layer, bk, bn, index):
    if w.ndim == 2:
        return pl.BlockSpec((bk, bn), index)
    return pl.BlockSpec((None, bk, bn), lambda *g: (layer,) + tuple(index(*g)))


def _mm_kernel(*refs, n_a, act, has_res, n_side):
    a_refs, w_ref = refs[:n_a], refs[n_a]
    refs = refs[n_a + 1:]
    res_ref = refs[0] if has_res else None
    side_in = refs[has_res:has_res + n_side]
    o_ref = refs[has_res + n_side]
    side_out = refs[1 + has_res + n_side:]
    for src, dst in zip(side_in, side_out):
        dst[...] = src[...].astype(jnp.bfloat16)
    a = jnp.concatenate([r[...] for r in a_refs], axis=1)
    for c in range(o_ref.shape[1] // MM_CHUNK):
        cols = slice(c * MM_CHUNK, (c + 1) * MM_CHUNK)
        acc = jnp.dot(a, w_ref[:, cols].astype(jnp.bfloat16), preferred_element_type=jnp.float32)
        if act == "relu2":
            acc = jnp.square(jnp.maximum(acc, 0.0))
        if has_res:
            acc = acc + res_ref[:, cols]
        o_ref[:, cols] = acc.astype(o_ref.dtype)


def _matmul(a, w, *, bm, bn, out_dtype, act=None, res=None, layer=0, n_out=None,
            w_col_block=lambda j: j, round_also=(), name):
    a_parts = a if isinstance(a, tuple) else (a,)
    m = a_parts[0].shape[0]
    kdim = sum(p.shape[1] for p in a_parts)
    n = w.shape[-1] if n_out is None else n_out
    nj = n // bn
    in_specs = [pl.BlockSpec((bm, p.shape[1]), lambda i, j: (i, 0)) for p in a_parts]
    in_specs.append(_weight_spec(w, layer, kdim, bn, lambda i, j: (0, w_col_block(j))))
    args = [*a_parts, w]
    if res is not None:
        in_specs.append(pl.BlockSpec((bm, bn), lambda i, j: (i, j)))
        args.append(res)
    out_shape = [jax.ShapeDtypeStruct((m, n), out_dtype)]
    out_specs = [pl.BlockSpec((bm, bn), lambda i, j: (i, j))]
    for side in round_also:
        _, rows, cols = side.shape
        slab = rows // ((m // bm) * nj)
        in_specs.append(pl.BlockSpec((None, slab, cols), lambda i, j: (layer, i * nj + j, 0)))
        args.append(side)
        out_shape.append(jax.ShapeDtypeStruct((rows, cols), jnp.bfloat16))
        out_specs.append(pl.BlockSpec((slab, cols), lambda i, j: (i * nj + j, 0)))
    out = pl.pallas_call(
        functools.partial(_mm_kernel, n_a=len(a_parts), act=act, has_res=res is not None,
                          n_side=len(round_also)),
        out_shape=out_shape,
        grid=(m // bm, nj),
        in_specs=in_specs,
        out_specs=out_specs,
        compiler_params=_params("parallel", "parallel"),
        name=name,
    )(*args)
    return out if round_also else out[0]


def _mm_ksplit_kernel(a_ref, w_ref, res_ref, o_ref):
    @pl.when(pl.program_id(2) == 0)
    def _():
        o_ref[...] = res_ref[...]

    for c in range(o_ref.shape[1] // MM_CHUNK):
        cols = slice(c * MM_CHUNK, (c + 1) * MM_CHUNK)
        o_ref[:, cols] = (jnp.dot(a_ref[...], w_ref[:, cols], preferred_element_type=jnp.float32)
                          + o_ref[:, cols])


def _matmul_ksplit(a, w, res, *, bm, bn, bk, name):
    m, kdim = a.shape
    n = w.shape[1]
    return pl.pallas_call(
        _mm_ksplit_kernel,
        out_shape=jax.ShapeDtypeStruct((m, n), jnp.float32),
        grid=(m // bm, n // bn, kdim // bk),
        in_specs=[pl.BlockSpec((bm, bk), lambda i, j, k: (i, k)),
                  pl.BlockSpec((bk, bn), lambda i, j, k: (k, j)),
                  pl.BlockSpec((bm, bn), lambda i, j, k: (i, j))],
        out_specs=pl.BlockSpec((bm, bn), lambda i, j, k: (i, j)),
        compiler_params=_params("parallel", "parallel", "arbitrary"),
        name=name,
    )(a, w, res)


def _inproj_kernel(h_ref, w_ref, cos_ref, sa_ref, sb_ref, o_ref, acc_ref):
    @pl.when(pl.program_id(0) == 0)
    def _():
        acc_ref[...] = jnp.zeros_like(acc_ref)

    half_rows = acc_ref.shape[0] // 2
    for r0 in (0, half_rows):
        rows = slice(r0, r0 + half_rows)
        cos, sa, sb = cos_ref[rows, :], sa_ref[rows, :], sb_ref[rows, :]
        for c in range(o_ref.shape[1] // HEAD_DIM):
            cols = slice(c * HEAD_DIM, (c + 1) * HEAD_DIM)
            a = acc_ref[rows, cols]
            r = (a * cos + pltpu.roll(a, HEAD_DIM - ROT_DIM // 2, 1) * sa
                 + pltpu.roll(a, ROT_DIM // 2, 1) * sb)
            o_ref[rows, cols] = r.astype(o_ref.dtype)
    acc_ref[...] = jnp.dot(h_ref[...], w_ref[...], preferred_element_type=jnp.float32)


def _rope_tables(seq_len):
    half = ROT_DIM // 2
    pos = jnp.arange(seq_len, dtype=jnp.float32)
    inv_freq = ROPE_THETA ** (-jnp.arange(0, ROT_DIM, 2, dtype=jnp.float32) / ROT_DIM)
    ang = pos[:, None] * inv_freq[None, :]
    cos, sin = jnp.cos(ang), jnp.sin(ang)
    z = lambda w: jnp.zeros((seq_len, w), jnp.float32)
    cos_t = jnp.concatenate([cos, cos, jnp.ones((seq_len, HEAD_DIM - ROT_DIM), jnp.float32)], axis=1)
    sin_a = jnp.concatenate([-sin, z(HEAD_DIM - half)], axis=1)
    sin_b = jnp.concatenate([z(half), sin, z(HEAD_DIM - ROT_DIM)], axis=1)
    return cos_t, sin_a, sin_b


def _inproj_qk(h, w, seq_len, group_width, bm=1024, bn=1024):
    m, kdim = h.shape
    bpg = group_width // bn
    nj = 4 * bpg
    n_blocks = (m // bm) * nj
    cos_t, sin_a, sin_b = (jnp.stack([t * QK_SCALE, t]) for t in _rope_tables(seq_len))

    mul = lambda t: jnp.minimum(t, n_blocks - 1)
    fin = lambda t: jnp.maximum(t - 1, 0)

    def w_block(t):
        out_group = (mul(t) % nj) // bpg
        return 0, (out_group + out_group // 2) * bpg + mul(t) % bpg

    tspec = pl.BlockSpec(
        (None, bm, HEAD_DIM),
        lambda t: (((fin(t) % nj) // bpg) % 2, (fin(t) // nj) % (seq_len // bm), 0))
    return pl.pallas_call(
        _inproj_kernel,
        out_shape=jax.ShapeDtypeStruct((m, 4 * group_width), jnp.bfloat16),
        grid=(n_blocks + 1,),
        in_specs=[pl.BlockSpec((bm, kdim), lambda t: (mul(t) // nj, 0)),
                  pl.BlockSpec((kdim, bn), w_block),
                  tspec, tspec, tspec],
        out_specs=pl.BlockSpec((bm, bn), lambda t: (fin(t) // nj, fin(t) % nj)),
        scratch_shapes=[pltpu.VMEM((bm, bn), jnp.float32)],
        compiler_params=_params("arbitrary"),
        name="inproj_qk_rope",
    )(h, w, cos_t, sin_a, sin_b)


VT_PAD = 16


def _scores(k, q):
    return lax.dot_general(k, q, NT_DIMS, preferred_element_type=jnp.float32)


def _transposed_values(v_blk):
    vt = v_blk.astype(jnp.float32).T.astype(jnp.bfloat16)
    row = lax.broadcasted_iota(jnp.int32, (VT_PAD, v_blk.shape[0]), 0)
    return jnp.concatenate([vt, jnp.where(row == 0, 1.0, 0.0).astype(jnp.bfloat16)], axis=0)


def _weights(x):
    return jnp.exp2(x).astype(jnp.bfloat16)


def _diag_state(s, vt):
    key = lax.broadcasted_iota(jnp.int32, s.shape, 0)
    qry = lax.broadcasted_iota(jnp.int32, s.shape, 1)
    s = jnp.where(key <= qry, s, NEG_BIG)
    m = jnp.max(s, axis=0, keepdims=True)
    return m, jnp.dot(vt, _weights(s - m), preferred_element_type=jnp.float32)


def _normalised(acc):
    v_dim = acc.shape[0] - VT_PAD
    return acc[:v_dim] / acc[v_dim:v_dim + 1]


def _update_state(pieces, biases, vts, state):
    m, acc = state
    m_new = m
    for s, b in zip(pieces, biases):
        m_new = jnp.maximum(m_new, jnp.max(s, axis=0, keepdims=True) + b)
    acc = jnp.exp2(m - m_new) * acc
    for s, b, vt in zip(pieces, biases, vts):
        acc = acc + jnp.dot(vt, _weights(s - (m_new - b)), preferred_element_type=jnp.float32)
    return m_new, acc


STALE_MAX_CAP = 40.0


def _update_state_stale(pieces, biases, vts, state):
    m, acc = state
    m_new = m
    for s, b, vt in zip(pieces, biases, vts):
        m_new = jnp.maximum(m_new, jnp.max(s, axis=0, keepdims=True) + b)
        acc = acc + jnp.dot(vt, _weights(s - (m - b)), preferred_element_type=jnp.float32)
    rise = m_new - m
    return (m_new, jnp.exp2(-rise) * acc), rise


def _attend_groups(n_blocks, group_sizes, group_inputs, states, fin_ref):
    def fast(inputs, carry):
        states, worst = carry
        new, rises = zip(*(_update_state_stale(*inp, st) for inp, st in zip(inputs, states)))
        return tuple(new), functools.reduce(jnp.maximum, rises + (worst,))

    def exact(inputs, states):
        return tuple(_update_state(*inp, st) for inp, st in zip(inputs, states))

    def fold(update, carry):
        first = 0
        for idx, size in enumerate(group_sizes):
            slack = size - 1 if idx == len(group_sizes) - 1 else 0
            trips = lax.div(n_blocks - first + slack, size)
            carry = lax.fori_loop(
                0, trips,
                lambda i, c, size=size, first=first: update(group_inputs(first + i * size, size), c),
                carry)
            first = first + trips * size
        return carry

    def store(states):
        for c, (_, acc) in enumerate(states):
            fin_ref[c] = acc

    done, worst = fold(fast, (states, jnp.zeros_like(states[0][0])))
    store(done)

    @pl.when(jnp.max(worst) > STALE_MAX_CAP)
    def _():
        store(fold(exact, states))


class _LaggedGrid:
    def __init__(self, n_steps, n_groups, n_blocks):
        self.n_steps, self.n_groups, self.n_blocks = n_steps, n_groups, n_blocks

    def _decode(self, item, index):
        per_batch = self.n_groups * self.n_blocks
        return index(item // per_batch, (item // self.n_blocks) % self.n_groups,
                     item % self.n_blocks)

    def cur(self, index):
        return lambda t: self._decode(jnp.minimum(t, self.n_steps - 1), index)

    def prev(self, index):
        return lambda t: self._decode(jnp.maximum(t - 1, 0), index)


def _key_rows(k_ref, first_block, n_blocks, cols=slice(None)):
    rows = n_blocks * ATT_BLOCK
    return k_ref[pl.ds(pl.multiple_of(first_block * ATT_BLOCK, ATT_BLOCK), rows), cols]


def _moba_kernel(q_ref, k_ref, v_ref, g_ref, o_ref, kmh_ref, kml_ref, vt_ref, bias_ref, fin_ref,
                 *, n_blocks, n_heads, n_steps):
    t = pl.program_id(0)
    qi = lax.rem(jnp.minimum(t, n_steps - 1), n_blocks)
    heads = [slice(h * HEAD_DIM, (h + 1) * HEAD_DIM) for h in range(n_heads)]

    @pl.when(t == 0)
    def _():
        fin_ref[...] = jnp.ones_like(fin_ref)

    @pl.when(qi == 0)
    def _():
        for h, cols in enumerate(heads):
            kf = k_ref[:, cols].astype(jnp.float32).reshape(n_blocks, MOBA_BLOCK, HEAD_DIM)
            km = jnp.sum(kf, axis=1) * (1.0 / MOBA_BLOCK)
            hi = km.astype(jnp.bfloat16)
            kmh_ref[h] = hi
            kml_ref[h] = (km - hi.astype(jnp.float32)).astype(jnp.bfloat16)
            for n in range(n_blocks):
                vt_ref[h * n_blocks + n] = _transposed_values(
                    v_ref[n * ATT_BLOCK:(n + 1) * ATT_BLOCK, cols])

    for h, cols in enumerate(heads):
        o_ref[:, cols] = _rms(_normalised(fin_ref[h]).T, g_ref[:, cols]).astype(o_ref.dtype)

    qs = [q_ref[:, cols] for cols in heads]
    for h, q in enumerate(qs):
        gate = (lax.dot_general(kmh_ref[h], q, NT_DIMS, preferred_element_type=jnp.float32)
                + lax.dot_general(kml_ref[h], q, NT_DIMS, preferred_element_type=jnp.float32))
        blk = lax.broadcasted_iota(jnp.int32, gate.shape, 0)
        past = blk < qi
        g = jnp.where(past, gate, -jnp.inf)
        sel = jnp.zeros(gate.shape, jnp.float32)
        for _ in range(MOBA_TOPK):
            mx = jnp.max(g, axis=0, keepdims=True)
            first = jnp.min(jnp.where(g == mx, blk, n_blocks), axis=0, keepdims=True)
            hit = blk == first
            sel = jnp.where(hit, jnp.where(past, 1.0, 0.0), sel)
            g = jnp.where(hit, -jnp.inf, g)
        bias_ref[h] = jnp.where(sel > 0.0, 0.0, NEG_BIG)

    diag = [_scores(_key_rows(k_ref, qi, 1, cols), q) for q, cols in zip(qs, heads)]
    states = tuple(_diag_state(s, vt_ref[h * n_blocks + qi]) for h, s in enumerate(diag))

    def group_inputs(first, size):
        scores = [_scores(_key_rows(k_ref, first, size, cols), q) for q, cols in zip(qs, heads)]
        inputs = []
        for h, s in enumerate(scores):
            pieces = [s[n * ATT_BLOCK:(n + 1) * ATT_BLOCK] for n in range(size)]
            biases = [bias_ref[h, pl.ds(first + n, 1), :] for n in range(size)]
            vts = [vt_ref[h * n_blocks + first + n] for n in range(size)]
            inputs.append((pieces, biases, vts))
        return inputs

    _attend_groups(qi, MOBA_GROUPS, group_inputs, states, fin_ref)


def _moba_attention(qk, vproj, out_g, n_heads, heads_per_step=8):
    b, s, _ = qk.shape
    n_blocks = s // MOBA_BLOCK
    hp = heads_per_step
    w = hp * HEAD_DIM
    n_groups = n_heads // hp
    n_steps = b * n_groups * n_blocks
    grid = _LaggedGrid(n_steps, n_groups, n_blocks)
    return pl.pallas_call(
        functools.partial(_moba_kernel, n_blocks=n_blocks, n_heads=hp, n_steps=n_steps),
        out_shape=jax.ShapeDtypeStruct((b, s, n_heads * HEAD_DIM), jnp.bfloat16),
        grid=(n_steps + 1,),
        in_specs=[pl.BlockSpec((None, MOBA_BLOCK, w), grid.cur(lambda bi, h, qi: (bi, qi, h))),
                  pl.BlockSpec((None, s, w), grid.cur(lambda bi, h, qi: (bi, 0, n_groups + h))),
                  pl.BlockSpec((None, s, w), grid.cur(lambda bi, h, qi: (bi, 0, h))),
                  pl.BlockSpec((None, 1, w), grid.prev(lambda bi, h, qi: (h, 0, 0)))],
        out_specs=pl.BlockSpec((None, MOBA_BLOCK, w), grid.prev(lambda bi, h, qi: (bi, qi, h))),
        scratch_shapes=[pltpu.VMEM((hp, n_blocks, HEAD_DIM), jnp.bfloat16),
                        pltpu.VMEM((hp, n_blocks, HEAD_DIM), jnp.bfloat16),
                        pltpu.VMEM((hp * n_blocks, HEAD_DIM + VT_PAD, MOBA_BLOCK), jnp.bfloat16),
                        pltpu.VMEM((hp, n_blocks, MOBA_BLOCK), jnp.float32),
                        pltpu.VMEM((hp, HEAD_DIM + VT_PAD, MOBA_BLOCK), jnp.float32)],
        compiler_params=_params("arbitrary"),
        name="moba_attention",
    )(qk, qk, vproj, out_g.reshape(n_groups, 1, w))


def _diff_kernel(lq1_ref, lk1_ref, lq2_ref, lk2_ref, q_ref, k_ref, v_ref, g_ref,
                 o_ref, vt_ref, fin_ref, *, lambda_init, n_heads, n_steps):
    t = pl.program_id(0)
    n_blocks = vt_ref.shape[0] // n_heads
    qi = lax.rem(jnp.minimum(t, n_steps - 1), n_blocks)
    vw = 2 * HEAD_DIM

    @pl.when(t == 0)
    def _():
        fin_ref[...] = jnp.ones_like(fin_ref)

    @pl.when(qi == 0)
    def _():
        for h in range(n_heads):
            for n in range(n_blocks):
                vt_ref[h * n_blocks + n] = _transposed_values(
                    v_ref[n * ATT_BLOCK:(n + 1) * ATT_BLOCK, h * vw:(h + 1) * vw])

    lam = (jnp.exp(jnp.sum(lq1_ref[...] * lk1_ref[...], axis=1, keepdims=True))
           - jnp.exp(jnp.sum(lq2_ref[...] * lk2_ref[...], axis=1, keepdims=True))
           + lambda_init)
    for h in range(n_heads):
        o = (_normalised(fin_ref[2 * h]) - lam * _normalised(fin_ref[2 * h + 1])).T
        o_ref[:, h * vw:(h + 1) * vw] = (_rms(o, g_ref[...]) * (1.0 - lambda_init)).astype(o_ref.dtype)

    subs = [slice(j * HEAD_DIM, (j + 1) * HEAD_DIM) for j in range(2 * n_heads)]
    qs = [q_ref[:, cols] for cols in subs]
    diag = [_scores(_key_rows(k_ref, qi, 1, cols), q) for q, cols in zip(qs, subs)]
    states = tuple(_diag_state(s, vt_ref[(j // 2) * n_blocks + qi]) for j, s in enumerate(diag))

    def group_inputs(first, size):
        scores = [_scores(_key_rows(k_ref, first, size, cols), q) for q, cols in zip(qs, subs)]
        biases = [jnp.where(first + n < qi, 0.0, NEG_BIG) for n in range(size)]
        inputs = []
        for j, s in enumerate(scores):
            vts = [vt_ref[(j // 2) * n_blocks + first + n] for n in range(size)]
            pieces = [s[n * ATT_BLOCK:(n + 1) * ATT_BLOCK] for n in range(size)]
            inputs.append((pieces, biases, vts))
        return inputs

    _attend_groups(qi, DIFF_GROUPS, group_inputs, states, fin_ref)


def _diff_attention(qk, vproj, lq1, lk1, lq2, lk2, subln_g, n_heads, lambda_init,
                    heads_per_step=4):
    b, s, qk_width = qk.shape
    v_col0 = vproj.shape[2] // 2
    col0 = qk_width // 2
    hp = heads_per_step
    w = hp * 2 * HEAD_DIM
    n_groups = n_heads // hp
    tq = ATT_BLOCK
    n_steps = b * n_groups * (s // tq)
    grid = _LaggedGrid(n_steps, n_groups, s // tq)
    vec = lambda x: x.reshape(1, HEAD_DIM).astype(jnp.float32)
    vspec = pl.BlockSpec((1, HEAD_DIM), lambda t: (0, 0))
    qb, kb, vb = col0 // w, col0 // w + n_groups, v_col0 // w
    return pl.pallas_call(
        functools.partial(_diff_kernel, lambda_init=lambda_init, n_heads=hp, n_steps=n_steps),
        out_shape=jax.ShapeDtypeStruct((b, s, n_heads * 2 * HEAD_DIM), jnp.bfloat16),
        grid=(n_steps + 1,),
        in_specs=[vspec, vspec, vspec, vspec,
                  pl.BlockSpec((None, tq, w), grid.cur(lambda bi, h, qi: (bi, qi, qb + h))),
                  pl.BlockSpec((None, s, w), grid.cur(lambda bi, h, qi: (bi, 0, kb + h))),
                  pl.BlockSpec((None, s, w), grid.cur(lambda bi, h, qi: (bi, 0, vb + h)),
                               pipeline_mode=pl.Buffered(1)),
                  pl.BlockSpec((1, 2 * HEAD_DIM), lambda t: (0, 0))],
        out_specs=pl.BlockSpec((None, tq, w), grid.prev(lambda bi, h, qi: (bi, qi, h))),
        scratch_shapes=[pltpu.VMEM((hp * (s // ATT_BLOCK), 2 * HEAD_DIM + VT_PAD, ATT_BLOCK),
                                   jnp.bfloat16),
                        pltpu.VMEM((2 * hp, 2 * HEAD_DIM + VT_PAD, ATT_BLOCK), jnp.float32)],
        compiler_params=_params("arbitrary"),
        name="diff_attention",
    )(vec(lq1), vec(lk1), vec(lq2), vec(lk2), qk, qk, vproj, subln_g.reshape(1, 2 * HEAD_DIM))


def _cross_kernel(x_ref, gc_ref, wq_ref, kv_ref, wo_ref, gm_ref, x2_ref, hm_ref):
    c = QK_SCALE
    x = x_ref[...]
    hc = _rms(x, gc_ref[...]).astype(jnp.bfloat16)
    q = jnp.dot(hc, wq_ref[...], preferred_element_type=jnp.float32).astype(jnp.bfloat16)
    kv_w = MEM_HEADS * HEAD_DIM
    outs = []
    for h in range(MEM_HEADS):
        k = kv_ref[:, h * HEAD_DIM:(h + 1) * HEAD_DIM]
        v = kv_ref[:, kv_w + h * HEAD_DIM:kv_w + (h + 1) * HEAD_DIM]
        s = lax.dot_general(q[:, h * HEAD_DIM:(h + 1) * HEAD_DIM], k, NT_DIMS,
                            preferred_element_type=jnp.float32)
        m = jnp.max(s, axis=1, keepdims=True)
        p = jnp.exp2((s - m) * c)
        l = jnp.sum(p, axis=1, keepdims=True)
        o = jnp.dot(p.astype(jnp.bfloat16), v, preferred_element_type=jnp.float32) / l
        outs.append(o.astype(jnp.bfloat16))
    o = jnp.concatenate(outs, axis=1)
    x2 = x + jnp.dot(o, wo_ref[...], preferred_element_type=jnp.float32)
    x2_ref[...] = x2
    hm_ref[...] = _rms(x2, gm_ref[...]).astype(hm_ref.dtype)


def _cross_sublayer(x, kv, g_cross, w_cq, w_co, g_mlp, seq_len, mem_len, tq=512):
    n, d = x.shape
    qw = w_cq.shape[1]
    const = lambda i: (0, 0)
    return pl.pallas_call(
        _cross_kernel,
        out_shape=(jax.ShapeDtypeStruct((n, d), jnp.float32),
                   jax.ShapeDtypeStruct((n, d), jnp.bfloat16)),
        grid=(n // tq,),
        in_specs=[pl.BlockSpec((tq, d), lambda i: (i, 0)),
                  pl.BlockSpec((1, d), const),
                  pl.BlockSpec((d, qw), const, pipeline_mode=pl.Buffered(1)),
                  pl.BlockSpec((mem_len, 2 * qw), lambda i: (i // (seq_len // tq), 0)),
                  pl.BlockSpec((qw, d), const, pipeline_mode=pl.Buffered(1)),
                  pl.BlockSpec((1, d), const)],
        out_specs=(pl.BlockSpec((tq, d), lambda i: (i, 0)),
                   pl.BlockSpec((tq, d), lambda i: (i, 0))),
        compiler_params=_params("parallel"),
        name="cross_sublayer",
    )(x, g_cross.reshape(1, d), w_cq, kv, w_co, g_mlp.reshape(1, d))


def kernel(x, mem, ln_mix_g, w_in, moba_out_g, lambda_q1, lambda_k1, lambda_q2, lambda_k2,
           diff_subln_g, w_out, ln_cross_g, ln_mem_g, w_cq, w_ckv, w_co, ln_mlp_g,
           w_up, w_down, final_g):
    b, s, d = x.shape
    mem_len = mem.shape[1]
    depth = w_in.shape[0]
    mix_width = w_out.shape[1]
    moba_width = mix_width // 2
    moba_heads = moba_width // HEAD_DIM
    diff_heads = (mix_width - moba_width) // (2 * HEAD_DIM)
    bf = jnp.bfloat16

    xf = x.reshape(b * s, d)
    memf = mem.reshape(b * mem_len, d)
    for l in range(depth):
        lambda_init = 0.8 - 0.6 * math.exp(-0.3 * l)

        h = _rmsnorm(xf, ln_mix_g[l], bf)
        w_in_bf = w_in[l].astype(bf)
        qk = _inproj_qk(h, w_in_bf, s, moba_width).reshape(b, s, -1)
        bpg = moba_width // 1024
        vproj, w_out_bf, w_up_bf = _matmul(
            h, w_in_bf, bm=1024, bn=1024, out_dtype=bf, n_out=2 * moba_width,
            w_col_block=lambda j: (2 + 3 * (j // bpg)) * bpg + j % bpg,
            round_also=(w_out, w_up), layer=l, name="inproj_v")
        vproj = vproj.reshape(b, s, -1)
        mo = _moba_attention(qk, vproj, moba_out_g[l], moba_heads)
        do = _diff_attention(qk, vproj, lambda_q1[l], lambda_k1[l], lambda_q2[l], lambda_k2[l],
                             diff_subln_g[l], diff_heads, lambda_init)
        x1 = _matmul((mo.reshape(b * s, -1), do.reshape(b * s, -1)), w_out_bf, bm=1024, bn=1024,
                     out_dtype=jnp.float32, res=xf, name="outproj")

        mn = _rmsnorm(memf, ln_mem_g[l], bf)
        kv = _matmul(mn, w_ckv, layer=l, bm=b * mem_len, bn=512, out_dtype=bf, name="mem_kv")
        x2, hm = _cross_sublayer(x1, kv, ln_cross_g[l], w_cq[l].astype(bf), w_co[l].astype(bf),
                                 ln_mlp_g[l], s, mem_len)

        u, w_down_bf = _matmul(hm, w_up_bf, bm=1024, bn=1024, out_dtype=bf, act="relu2",
                               round_also=(w_down,), layer=l, name="mlp_up")
        xf = _matmul_ksplit(u, w_down_bf, x2, bm=1024, bn=1024, bk=4096, name="mlp_down")
    return _rmsnorm(xf, final_g, jnp.float32).reshape(b, s, d)
```

```python
import functools
import math

import jax
import jax.numpy as jnp
from jax import lax
from jax.experimental import pallas as pl
from jax.experimental.pallas import tpu as pltpu

HEAD_DIM = 128
MOBA_BLOCK = 256
MOBA_TOPK = 3
ROT_DIM = HEAD_DIM // 4
ROPE_THETA = 500000.0
MEM_HEADS = 4
EPS = 1e-5

VMEM_LIMIT_BYTES = 60 * 2**20
NEG_BIG = -1e30
LOG2E = 1.4426950408889634
QK_SCALE = HEAD_DIM ** -0.5 * LOG2E
NT_DIMS = (((1,), (1,)), ((), ()))
ATT_BLOCK = 256
MOBA_GROUPS = (4, 2)
DIFF_GROUPS = (4, 2)
MM_CHUNK = 256


def _params(*sem):
    return pltpu.CompilerParams(dimension_semantics=sem, vmem_limit_bytes=VMEM_LIMIT_BYTES)


def _rms(x, g):
    return x * lax.rsqrt(jnp.mean(x * x, axis=-1, keepdims=True) + EPS) * g


def _rmsnorm_kernel(x_ref, g_ref, o_ref):
    o_ref[...] = _rms(x_ref[...], g_ref[...]).astype(o_ref.dtype)


def _rmsnorm(x, g, out_dtype, bm=512):
    n, d = x.shape
    return pl.pallas_call(
        _rmsnorm_kernel,
        out_shape=jax.ShapeDtypeStruct((n, d), out_dtype),
        grid=(n // bm,),
        in_specs=[pl.BlockSpec((bm, d), lambda i: (i, 0)),
                  pl.BlockSpec((1, d), lambda i: (0, 0))],
        out_specs=pl.BlockSpec((bm, d), lambda i: (i, 0)),
        compiler_params=_params("parallel"),
        name="rmsnorm",
    )(x, g.reshape(1, d))


def _weight_spec(w, layer, bk, bn, index):
    if w.ndim == 2:
        return pl.BlockSpec((bk, bn), index)
    return pl.BlockSpec((None, bk, bn), lambda *g: (layer,) + tuple(index(*g)))


def _mm_kernel(*refs, n_a, act, has_res, n_side):
    a_refs, w_ref = refs[:n_a], refs[n_a]
    refs = refs[n_a + 1:]
    res_ref = refs[0] if has_res else None
    side_in = refs[has_res:has_res + n_side]
    o_ref = refs[has_res + n_side]
    side_out = refs[1 + has_res + n_side:]
    for src, dst in zip(side_in, side_out):
        dst[...] = src[...].astype(jnp.bfloat16)
    a = jnp.concatenate([r[...] for r in a_refs], axis=1)
    for c in range(o_ref.shape[1] // MM_CHUNK):
        cols = slice(c * MM_CHUNK, (c + 1) * MM_CHUNK)
        acc = jnp.dot(a, w_ref[:, cols].astype(jnp.bfloat16), preferred_element_type=jnp.float32)
        if act == "relu2":
            acc = jnp.square(jnp.maximum(acc, 0.0))
        if has_res:
            acc = acc + res_ref[:, cols]
        o_ref[:, cols] = acc.astype(o_ref.dtype)


def _matmul(a, w, *, bm, bn, out_dtype, act=None, res=None, layer=0, n_out=None,
            w_col_block=lambda j: j, round_also=(), name):
    a_parts = a if isinstance(a, tuple) else (a,)
    m = a_parts[0].shape[0]
    kdim = sum(p.shape[1] for p in a_parts)
    n = w.shape[-1] if n_out is None else n_out
    nj = n // bn
    in_specs = [pl.BlockSpec((bm, p.shape[1]), lambda i, j: (i, 0)) for p in a_parts]
    in_specs.append(_weight_spec(w, layer, kdim, bn, lambda i, j: (0, w_col_block(j))))
    args = [*a_parts, w]
    if res is not None:
        in_specs.append(pl.BlockSpec((bm, bn), lambda i, j: (i, j)))
        args.append(res)
    out_shape = [jax.ShapeDtypeStruct((m, n), out_dtype)]
    out_specs = [pl.BlockSpec((bm, bn), lambda i, j: (i, j))]
    for side in round_also:
        _, rows, cols = side.shape
        slab = rows // ((m // bm) * nj)
        in_specs.append(pl.BlockSpec((None, slab, cols), lambda i, j: (layer, i * nj + j, 0)))
        args.append(side)
        out_shape.append(jax.ShapeDtypeStruct((rows, cols), jnp.bfloat16))
        out_specs.append(pl.BlockSpec((slab, cols), lambda i, j: (i * nj + j, 0)))
    out = pl.pallas_call(
        functools.partial(_mm_kernel, n_a=len(a_parts), act=act, has_res=res is not None,
                          n_side=len(round_also)),
        out_shape=out_shape,
        grid=(m // bm, nj),
        in_specs=in_specs,
        out_specs=out_specs,
        compiler_params=_params("parallel", "parallel"),
        name=name,
    )(*args)
    return out if round_also else out[0]


def _mm_ksplit_kernel(a_ref, w_ref, res_ref, o_ref):
    @pl.when(pl.program_id(2) == 0)
    def _():
        o_ref[...] = res_ref[...]

    for c in range(o_ref.shape[1] // MM_CHUNK):
        cols = slice(c * MM_CHUNK, (c + 1) * MM_CHUNK)
        o_ref[:, cols] = (jnp.dot(a_ref[...], w_ref[:, cols], preferred_element_type=jnp.float32)
                          + o_ref[:, cols])


def _matmul_ksplit(a, w, res, *, bm, bn, bk, name):
    m, kdim = a.shape
    n = w.shape[1]
    return pl.pallas_call(
        _mm_ksplit_kernel,
        out_shape=jax.ShapeDtypeStruct((m, n), jnp.float32),
        grid=(m // bm, n // bn, kdim // bk),
        in_specs=[pl.BlockSpec((bm, bk), lambda i, j, k: (i, k)),
                  pl.BlockSpec((bk, bn), lambda i, j, k: (k, j)),
                  pl.BlockSpec((bm, bn), lambda i, j, k: (i, j))],
        out_specs=pl.BlockSpec((bm, bn), lambda i, j, k: (i, j)),
        compiler_params=_params("parallel", "parallel", "arbitrary"),
        name=name,
    )(a, w, res)


def _norm_v_kernel(x_ref, g_ref, w_ref, *refs, n_side):
    side_in, h_ref, o_ref, side_out = refs[:n_side], refs[n_side], refs[n_side + 1], refs[n_side + 2:]

    @pl.when(pl.program_id(1) == 0)
    def _():
        h_ref[...] = _rms(x_ref[...], g_ref[...]).astype(h_ref.dtype)

    for src, dst in zip(side_in, side_out):
        dst[...] = src[...].astype(jnp.bfloat16)
    for c in range(o_ref.shape[1] // MM_CHUNK):
        cols = slice(c * MM_CHUNK, (c + 1) * MM_CHUNK)
        o_ref[:, cols] = jnp.dot(h_ref[...], w_ref[:, cols],
                                 preferred_element_type=jnp.float32).astype(o_ref.dtype)


def _norm_inproj_v(x, g, w, group_width, round_also, layer, bm=512, bn=1024):
    m, d = x.shape
    bpg = group_width // bn
    nj = 2 * bpg
    in_specs = [pl.BlockSpec((bm, d), lambda i, j: (i, 0)),
                pl.BlockSpec((1, d), lambda i, j: (0, 0)),
                pl.BlockSpec((d, bn), lambda i, j: (0, (2 + 3 * (j // bpg)) * bpg + j % bpg))]
    args = [x, g.reshape(1, d), w]
    out_shape = [jax.ShapeDtypeStruct((m, d), jnp.bfloat16),
                 jax.ShapeDtypeStruct((m, 2 * group_width), jnp.bfloat16)]
    out_specs = [pl.BlockSpec((bm, d), lambda i, j: (i, 0)),
                 pl.BlockSpec((bm, bn), lambda i, j: (i, j))]
    for side in round_also:
        _, rows, cols = side.shape
        slab = rows // ((m // bm) * nj)
        in_specs.append(pl.BlockSpec((None, slab, cols), lambda i, j: (layer, i * nj + j, 0)))
        args.append(side)
        out_shape.append(jax.ShapeDtypeStruct((rows, cols), jnp.bfloat16))
        out_specs.append(pl.BlockSpec((slab, cols), lambda i, j: (i * nj + j, 0)))
    return pl.pallas_call(
        functools.partial(_norm_v_kernel, n_side=len(round_also)),
        out_shape=out_shape,
        grid=(m // bm, nj),
        in_specs=in_specs,
        out_specs=out_specs,
        compiler_params=_params("parallel", "arbitrary"),
        name="norm_inproj_v",
    )(*args)


def _inproj_kernel(h_ref, w_ref, cos_ref, sa_ref, sb_ref, o_ref, acc_ref):
    @pl.when(pl.program_id(0) == 0)
    def _():
        acc_ref[...] = jnp.zeros_like(acc_ref)

    half_rows = acc_ref.shape[0] // 2
    for r0 in (0, half_rows):
        rows = slice(r0, r0 + half_rows)
        cos, sa, sb = cos_ref[rows, :], sa_ref[rows, :], sb_ref[rows, :]
        for c in range(o_ref.shape[1] // HEAD_DIM):
            cols = slice(c * HEAD_DIM, (c + 1) * HEAD_DIM)
            a = acc_ref[rows, cols]
            r = (a * cos + pltpu.roll(a, HEAD_DIM - ROT_DIM // 2, 1) * sa
                 + pltpu.roll(a, ROT_DIM // 2, 1) * sb)
            o_ref[rows, cols] = r.astype(o_ref.dtype)
    acc_ref[...] = jnp.dot(h_ref[...], w_ref[...], preferred_element_type=jnp.float32)


def _rope_tables(seq_len):
    half = ROT_DIM // 2
    pos = jnp.arange(seq_len, dtype=jnp.float32)
    inv_freq = ROPE_THETA ** (-jnp.arange(0, ROT_DIM, 2, dtype=jnp.float32) / ROT_DIM)
    ang = pos[:, None] * inv_freq[None, :]
    cos, sin = jnp.cos(ang), jnp.sin(ang)
    z = lambda w: jnp.zeros((seq_len, w), jnp.float32)
    cos_t = jnp.concatenate([cos, cos, jnp.ones((seq_len, HEAD_DIM - ROT_DIM), jnp.float32)], axis=1)
    sin_a = jnp.concatenate([-sin, z(HEAD_DIM - half)], axis=1)
    sin_b = jnp.concatenate([z(half), sin, z(HEAD_DIM - ROT_DIM)], axis=1)
    return cos_t, sin_a, sin_b


def _inproj_qk(h, w, seq_len, group_width, bm=1024, bn=1024):
    m, kdim = h.shape
    bpg = group_width // bn
    nj = 4 * bpg
    n_blocks = (m // bm) * nj
    cos_t, sin_a, sin_b = (jnp.stack([t * QK_SCALE, t]) for t in _rope_tables(seq_len))

    mul = lambda t: jnp.minimum(t, n_blocks - 1)
    fin = lambda t: jnp.maximum(t - 1, 0)

    def w_block(t):
        out_group = (mul(t) % nj) // bpg
        return 0, (out_group + out_group // 2) * bpg + mul(t) % bpg

    tspec = pl.BlockSpec(
        (None, bm, HEAD_DIM),
        lambda t: (((fin(t) % nj) // bpg) % 2, (fin(t) // nj) % (seq_len // bm), 0))
    return pl.pallas_call(
        _inproj_kernel,
        out_shape=jax.ShapeDtypeStruct((m, 4 * group_width), jnp.bfloat16),
        grid=(n_blocks + 1,),
        in_specs=[pl.BlockSpec((bm, kdim), lambda t: (mul(t) // nj, 0)),
                  pl.BlockSpec((kdim, bn), w_block),
                  tspec, tspec, tspec],
        out_specs=pl.BlockSpec((bm, bn), lambda t: (fin(t) // nj, fin(t) % nj)),
        scratch_shapes=[pltpu.VMEM((bm, bn), jnp.float32)],
        compiler_params=_params("arbitrary"),
        name="inproj_qk_rope",
    )(h, w, cos_t, sin_a, sin_b)


VT_PAD = 16


def _scores(k, q):
    return lax.dot_general(k, q, NT_DIMS, preferred_element_type=jnp.float32)


def _transposed_values(v_blk):
    vt = v_blk.astype(jnp.float32).T.astype(jnp.bfloat16)
    row = lax.broadcasted_iota(jnp.int32, (VT_PAD, v_blk.shape[0]), 0)
    return jnp.concatenate([vt, jnp.where(row == 0, 1.0, 0.0).astype(jnp.bfloat16)], axis=0)


def _weights(x):
    return jnp.exp2(x).astype(jnp.bfloat16)


def _diag_state(s, vt):
    key = lax.broadcasted_iota(jnp.int32, s.shape, 0)
    qry = lax.broadcasted_iota(jnp.int32, s.shape, 1)
    s = jnp.where(key <= qry, s, NEG_BIG)
    m = jnp.max(s, axis=0, keepdims=True)
    return m, jnp.dot(vt, _weights(s - m), preferred_element_type=jnp.float32)


def _normalised(acc):
    v_dim = acc.shape[0] - VT_PAD
    return acc[:v_dim] / acc[v_dim:v_dim + 1]


def _update_state(pieces, biases, vts, state):
    m, acc = state
    m_new = m
    for s, b in zip(pieces, biases):
        m_new = jnp.maximum(m_new, jnp.max(s, axis=0, keepdims=True) + b)
    acc = jnp.exp2(m - m_new) * acc
    for s, b, vt in zip(pieces, biases, vts):
        acc = acc + jnp.dot(vt, _weights(s - (m_new - b)), preferred_element_type=jnp.float32)
    return m_new, acc


STALE_MAX_CAP = 40.0


def _update_state_stale(pieces, biases, vts, state):
    m, acc = state
    m_new = m
    for s, b, vt in zip(pieces, biases, vts):
        m_new = jnp.maximum(m_new, jnp.max(s, axis=0, keepdims=True) + b)
        acc = acc + jnp.dot(vt, _weights(s - (m - b)), preferred_element_type=jnp.float32)
    rise = m_new - m
    return (m_new, jnp.exp2(-rise) * acc), rise


def _attend_groups(n_blocks, group_sizes, group_inputs, states, fin_ref):
    def fast(inputs, carry):
        states, worst = carry
        new, rises = zip(*(_update_state_stale(*inp, st) for inp, st in zip(inputs, states)))
        return tuple(new), functools.reduce(jnp.maximum, rises + (worst,))

    def exact(inputs, states):
        return tuple(_update_state(*inp, st) for inp, st in zip(inputs, states))

    def fold(update, carry):
        first = 0
        for idx, size in enumerate(group_sizes):
            slack = size - 1 if idx == len(group_sizes) - 1 else 0
            trips = lax.div(n_blocks - first + slack, size)
            carry = lax.fori_loop(
                0, trips,
                lambda i, c, size=size, first=first: update(group_inputs(first + i * size, size), c),
                carry)
            first = first + trips * size
        return carry

    def store(states):
        for c, (_, acc) in enumerate(states):
            fin_ref[c] = acc

    done, worst = fold(fast, (states, jnp.zeros_like(states[0][0])))
    store(done)

    @pl.when(jnp.max(worst) > STALE_MAX_CAP)
    def _():
        store(fold(exact, states))


class _LaggedGrid:
    def __init__(self, n_steps, n_groups, n_blocks):
        self.n_steps, self.n_groups, self.n_blocks = n_steps, n_groups, n_blocks

    def _decode(self, item, index):
        per_batch = self.n_groups * self.n_blocks
        return index(item // per_batch, (item // self.n_blocks) % self.n_groups,
                     item % self.n_blocks)

    def cur(self, index):
        return lambda t: self._decode(jnp.minimum(t, self.n_steps - 1), index)

    def prev(self, index):
        return lambda t: self._decode(jnp.maximum(t - 1, 0), index)


def _key_rows(k_ref, first_block, n_blocks, cols=slice(None)):
    rows = n_blocks * ATT_BLOCK
    return k_ref[pl.ds(pl.multiple_of(first_block * ATT_BLOCK, ATT_BLOCK), rows), cols]


def _moba_kernel(q_ref, k_ref, v_ref, g_ref, o_ref, kmh_ref, kml_ref, vt_ref, bias_ref, fin_ref,
                 *, n_blocks, n_heads, n_steps):
    t = pl.program_id(0)
    qi = lax.rem(jnp.minimum(t, n_steps - 1), n_blocks)
    heads = [slice(h * HEAD_DIM, (h + 1) * HEAD_DIM) for h in range(n_heads)]

    @pl.when(t == 0)
    def _():
        fin_ref[...] = jnp.ones_like(fin_ref)

    @pl.when(qi == 0)
    def _():
        for h, cols in enumerate(heads):
            kf = k_ref[:, cols].astype(jnp.float32).reshape(n_blocks, MOBA_BLOCK, HEAD_DIM)
            km = jnp.sum(kf, axis=1) * (1.0 / MOBA_BLOCK)
            hi = km.astype(jnp.bfloat16)
            kmh_ref[h] = hi
            kml_ref[h] = (km - hi.astype(jnp.float32)).astype(jnp.bfloat16)
            for n in range(n_blocks):
                vt_ref[h * n_blocks + n] = _transposed_values(
                    v_ref[n * ATT_BLOCK:(n + 1) * ATT_BLOCK, cols])

    for h, cols in enumerate(heads):
        o_ref[:, cols] = _rms(_normalised(fin_ref[h]).T, g_ref[:, cols]).astype(o_ref.dtype)

    qs = [q_ref[:, cols] for cols in heads]
    for h, q in enumerate(qs):
        gate = (lax.dot_general(kmh_ref[h], q, NT_DIMS, preferred_element_type=jnp.float32)
                + lax.dot_general(kml_ref[h], q, NT_DIMS, preferred_element_type=jnp.float32))
        blk = lax.broadcasted_iota(jnp.int32, gate.shape, 0)
        past = blk < qi
        g = jnp.where(past, gate, -jnp.inf)
        sel = jnp.zeros(gate.shape, jnp.float32)
        for _ in range(MOBA_TOPK):
            mx = jnp.max(g, axis=0, keepdims=True)
            first = jnp.min(jnp.where(g == mx, blk, n_blocks), axis=0, keepdims=True)
            hit = blk == first
            sel = jnp.where(hit, jnp.where(past, 1.0, 0.0), sel)
            g = jnp.where(hit, -jnp.inf, g)
        bias_ref[h] = jnp.where(sel > 0.0, 0.0, NEG_BIG)

    diag = [_scores(_key_rows(k_ref, qi, 1, cols), q) for q, cols in zip(qs, heads)]
    states = tuple(_diag_state(s, vt_ref[h * n_blocks + qi]) for h, s in enumerate(diag))

    def group_inputs(first, size):
        scores = [_scores(_key_rows(k_ref, first, size, cols), q) for q, cols in zip(qs, heads)]
        inputs = []
        for h, s in enumerate(scores):
            pieces = [s[n * ATT_BLOCK:(n + 1) * ATT_BLOCK] for n in range(size)]
            biases = [bias_ref[h, pl.ds(first + n, 1), :] for n in range(size)]
            vts = [vt_ref[h * n_blocks + first + n] for n in range(size)]
            inputs.append((pieces, biases, vts))
        return inputs

    _attend_groups(qi, MOBA_GROUPS, group_inputs, states, fin_ref)


def _moba_attention(qk, vproj, out_g, n_heads, heads_per_step=8):
    b, s, _ = qk.shape
    n_blocks = s // MOBA_BLOCK
    hp = heads_per_step
    w = hp * HEAD_DIM
    n_groups = n_heads // hp
    n_steps = b * n_groups * n_blocks
    grid = _LaggedGrid(n_steps, n_groups, n_blocks)
    return pl.pallas_call(
        functools.partial(_moba_kernel, n_blocks=n_blocks, n_heads=hp, n_steps=n_steps),
        out_shape=jax.ShapeDtypeStruct((b, s, n_heads * HEAD_DIM), jnp.bfloat16),
        grid=(n_steps + 1,),
        in_specs=[pl.BlockSpec((None, MOBA_BLOCK, w), grid.cur(lambda bi, h, qi: (bi, qi, h))),
                  pl.BlockSpec((None, s, w), grid.cur(lambda bi, h, qi: (bi, 0, n_groups + h))),
                  pl.BlockSpec((None, s, w), grid.cur(lambda bi, h, qi: (bi, 0, h))),
                  pl.BlockSpec((None, 1, w), grid.prev(lambda bi, h, qi: (h, 0, 0)))],
        out_specs=pl.BlockSpec((None, MOBA_BLOCK, w), grid.prev(lambda bi, h, qi: (bi, qi, h))),
        scratch_shapes=[pltpu.VMEM((hp, n_blocks, HEAD_DIM), jnp.bfloat16),
                        pltpu.VMEM((hp, n_blocks, HEAD_DIM), jnp.bfloat16),
                        pltpu.VMEM((hp * n_blocks, HEAD_DIM + VT_PAD, MOBA_BLOCK), jnp.bfloat16),
                        pltpu.VMEM((hp, n_blocks, MOBA_BLOCK), jnp.float32),
                        pltpu.VMEM((hp, HEAD_DIM + VT_PAD, MOBA_BLOCK), jnp.float32)],
        compiler_params=_params("arbitrary"),
        name="moba_attention",
    )(qk, qk, vproj, out_g.reshape(n_groups, 1, w))


def _diff_kernel(lq1_ref, lk1_ref, lq2_ref, lk2_ref, q_ref, k_ref, v_ref, g_ref,
                 o_ref, vt_ref, fin_ref, *, lambda_init, n_heads, n_steps):
    t = pl.program_id(0)
    n_blocks = vt_ref.shape[0] // n_heads
    qi = lax.rem(jnp.minimum(t, n_steps - 1), n_blocks)
    vw = 2 * HEAD_DIM

    @pl.when(t == 0)
    def _():
        fin_ref[...] = jnp.ones_like(fin_ref)

    @pl.when(qi == 0)
    def _():
        for h in range(n_heads):
            for n in range(n_blocks):
                vt_ref[h * n_blocks + n] = _transposed_values(
                    v_ref[n * ATT_BLOCK:(n + 1) * ATT_BLOCK, h * vw:(h + 1) * vw])

    lam = (jnp.exp(jnp.sum(lq1_ref[...] * lk1_ref[...], axis=1, keepdims=True))
           - jnp.exp(jnp.sum(lq2_ref[...] * lk2_ref[...], axis=1, keepdims=True))
           + lambda_init)
    for h in range(n_heads):
        o = (_normalised(fin_ref[2 * h]) - lam * _normalised(fin_ref[2 * h + 1])).T
        o_ref[:, h * vw:(h + 1) * vw] = (_rms(o, g_ref[...]) * (1.0 - lambda_init)).astype(o_ref.dtype)

    subs = [slice(j * HEAD_DIM, (j + 1) * HEAD_DIM) for j in range(2 * n_heads)]
    qs = [q_ref[:, cols] for cols in subs]
    diag = [_scores(_key_rows(k_ref, qi, 1, cols), q) for q, cols in zip(qs, subs)]
    states = tuple(_diag_state(s, vt_ref[(j // 2) * n_blocks + qi]) for j, s in enumerate(diag))

    def group_inputs(first, size):
        scores = [_scores(_key_rows(k_ref, first, size, cols), q) for q, cols in zip(qs, subs)]
        biases = [jnp.where(first + n < qi, 0.0, NEG_BIG) for n in range(size)]
        inputs = []
        for j, s in enumerate(scores):
            vts = [vt_ref[(j // 2) * n_blocks + first + n] for n in range(size)]
            pieces = [s[n * ATT_BLOCK:(n + 1) * ATT_BLOCK] for n in range(size)]
            inputs.append((pieces, biases, vts))
        return inputs

    _attend_groups(qi, DIFF_GROUPS, group_inputs, states, fin_ref)


def _diff_attention(qk, vproj, lq1, lk1, lq2, lk2, subln_g, n_heads, lambda_init,
                    heads_per_step=4):
    b, s, qk_width = qk.shape
    v_col0 = vproj.shape[2] // 2
    col0 = qk_width // 2
    hp = heads_per_step
    w = hp * 2 * HEAD_DIM
    n_groups = n_heads // hp
    tq = ATT_BLOCK
    n_steps = b * n_groups * (s // tq)
    grid = _LaggedGrid(n_steps, n_groups, s // tq)
    vec = lambda x: x.reshape(1, HEAD_DIM).astype(jnp.float32)
    vspec = pl.BlockSpec((1, HEAD_DIM), lambda t: (0, 0))
    qb, kb, vb = col0 // w, col0 // w + n_groups, v_col0 // w
    return pl.pallas_call(
        functools.partial(_diff_kernel, lambda_init=lambda_init, n_heads=hp, n_steps=n_steps),
        out_shape=jax.ShapeDtypeStruct((b, s, n_heads * 2 * HEAD_DIM), jnp.bfloat16),
        grid=(n_steps + 1,),
        in_specs=[vspec, vspec, vspec, vspec,
                  pl.BlockSpec((None, tq, w), grid.cur(lambda bi, h, qi: (bi, qi, qb + h))),
                  pl.BlockSpec((None, s, w), grid.cur(lambda bi, h, qi: (bi, 0, kb + h))),
                  pl.BlockSpec((None, s, w), grid.cur(lambda bi, h, qi: (bi, 0, vb + h)),
                               pipeline_mode=pl.Buffered(1)),
                  pl.BlockSpec((1, 2 * HEAD_DIM), lambda t: (0, 0))],
        out_specs=pl.BlockSpec((None, tq, w), grid.prev(lambda bi, h, qi: (bi, qi, h))),
        scratch_shapes=[pltpu.VMEM((hp * (s // ATT_BLOCK), 2 * HEAD_DIM + VT_PAD, ATT_BLOCK),
                                   jnp.bfloat16),
                        pltpu.VMEM((2 * hp, 2 * HEAD_DIM + VT_PAD, ATT_BLOCK), jnp.float32)],
        compiler_params=_params("arbitrary"),
        name="diff_attention",
    )(vec(lq1), vec(lk1), vec(lq2), vec(lk2), qk, qk, vproj, subln_g.reshape(1, 2 * HEAD_DIM))


def _cross_kernel(x_ref, gc_ref, wq_ref, kv_ref, wo_ref, gm_ref, x2_ref, hm_ref):
    c = QK_SCALE
    x = x_ref[...]
    hc = _rms(x, gc_ref[...]).astype(jnp.bfloat16)
    q = jnp.dot(hc, wq_ref[...], preferred_element_type=jnp.float32).astype(jnp.bfloat16)
    kv_w = MEM_HEADS * HEAD_DIM
    outs = []
    for h in range(MEM_HEADS):
        k = kv_ref[:, h * HEAD_DIM:(h + 1) * HEAD_DIM]
        v = kv_ref[:, kv_w + h * HEAD_DIM:kv_w + (h + 1) * HEAD_DIM]
        s = lax.dot_general(q[:, h * HEAD_DIM:(h + 1) * HEAD_DIM], k, NT_DIMS,
                            preferred_element_type=jnp.float32)
        m = jnp.max(s, axis=1, keepdims=True)
        p = jnp.exp2((s - m) * c)
        l = jnp.sum(p, axis=1, keepdims=True)
        o = jnp.dot(p.astype(jnp.bfloat16), v, preferred_element_type=jnp.float32) / l
        outs.append(o.astype(jnp.bfloat16))
    o = jnp.concatenate(outs, axis=1)
    x2 = x + jnp.dot(o, wo_ref[...], preferred_element_type=jnp.float32)
    x2_ref[...] = x2
    hm_ref[...] = _rms(x2, gm_ref[...]).astype(hm_ref.dtype)


def _cross_sublayer(x, kv, g_cross, w_cq, w_co, g_mlp, seq_len, mem_len, tq=512):
    n, d = x.shape
    qw = w_cq.shape[1]
    const = lambda i: (0, 0)
    return pl.pallas_call(
        _cross_kernel,
        out_shape=(jax.ShapeDtypeStruct((n, d), jnp.float32),
                   jax.ShapeDtypeStruct((n, d), jnp.bfloat16)),
        grid=(n // tq,),
        in_specs=[pl.BlockSpec((tq, d), lambda i: (i, 0)),
                  pl.BlockSpec((1, d), const),
                  pl.BlockSpec((d, qw), const, pipeline_mode=pl.Buffered(1)),
                  pl.BlockSpec((mem_len, 2 * qw), lambda i: (i // (seq_len // tq), 0)),
                  pl.BlockSpec((qw, d), const, pipeline_mode=pl.Buffered(1)),
                  pl.BlockSpec((1, d), const)],
        out_specs=(pl.BlockSpec((tq, d), lambda i: (i, 0)),
                   pl.BlockSpec((tq, d), lambda i: (i, 0))),
        compiler_params=_params("parallel"),
        name="cross_sublayer",
    )(x, g_cross.reshape(1, d), w_cq, kv, w_co, g_mlp.reshape(1, d))


def kernel(x, mem, ln_mix_g, w_in, moba_out_g, lambda_q1, lambda_k1, lambda_q2, lambda_k2,
           diff_subln_g, w_out, ln_cross_g, ln_mem_g, w_cq, w_ckv, w_co, ln_mlp_g,
           w_up, w_down, final_g):
    b, s, d = x.shape
    mem_len = mem.shape[1]
    depth = w_in.shape[0]
    mix_width = w_out.shape[1]
    moba_width = mix_width // 2
    moba_heads = moba_width // HEAD_DIM
    diff_heads = (mix_width - moba_width) // (2 * HEAD_DIM)
    bf = jnp.bfloat16

    xf = x.reshape(b * s, d)
    memf = mem.reshape(b * mem_len, d)
    for l in range(depth):
        lambda_init = 0.8 - 0.6 * math.exp(-0.3 * l)

        w_in_bf = w_in[l].astype(bf)
        h, vproj, w_out_bf, w_up_bf = _norm_inproj_v(xf, ln_mix_g[l], w_in_bf, moba_width,
                                                     (w_out, w_up), l)
        qk = _inproj_qk(h, w_in_bf, s, moba_width).reshape(b, s, -1)
        vproj = vproj.reshape(b, s, -1)
        mo = _moba_attention(qk, vproj, moba_out_g[l], moba_heads)
        do = _diff_attention(qk, vproj, lambda_q1[l], lambda_k1[l], lambda_q2[l], lambda_k2[l],
                             diff_subln_g[l], diff_heads, lambda_init)
        x1 = _matmul((mo.reshape(b * s, -1), do.reshape(b * s, -1)), w_out_bf, bm=1024, bn=1024,
                     out_dtype=jnp.float32, res=xf, name="outproj")

        mn = _rmsnorm(memf, ln_mem_g[l], bf)
        kv = _matmul(mn, w_ckv, layer=l, bm=b * mem_len, bn=512, out_dtype=bf, name="mem_kv")
        x2, hm = _cross_sublayer(x1, kv, ln_cross_g[l], w_cq[l].astype(bf), w_co[l].astype(bf),
                                 ln_mlp_g[l], s, mem_len)

        u, w_down_bf = _matmul(hm, w_up_bf, bm=1024, bn=1024, out_dtype=bf, act="relu2",
                               round_also=(w_down,), layer=l, name="mlp_up")
        xf = _matmul_ksplit(u, w_down_bf, x2, bm=1024, bn=1024, bk=4096, name="mlp_down")
    return _rmsnorm(xf, final_g, jnp.float32).reshape(b, s, d)
```

```python
import functools
import math

import jax
import jax.numpy as jnp
from jax import lax
from jax.experimental import pallas as pl
from jax.experimental.pallas import tpu as pltpu

HEAD_DIM = 128
MOBA_BLOCK = 256
MOBA_TOPK = 3
ROT_DIM = HEAD_DIM // 4
ROPE_THETA = 500000.0
MEM_HEADS = 4
EPS = 1e-5

VMEM_LIMIT_BYTES = 60 * 2**20
NEG_BIG = -1e30
LOG2E = 1.4426950408889634
QK_SCALE = HEAD_DIM ** -0.5 * LOG2E
NT_DIMS = (((1,), (1,)), ((), ()))
ATT_BLOCK = 256
MOBA_GROUPS = (4, 2)
DIFF_GROUPS = (4, 2)
MM_CHUNK = 256


def _params(*sem):
    return pltpu.CompilerParams(dimension_semantics=sem, vmem_limit_bytes=VMEM_LIMIT_BYTES)


def _rms(x, g):
    return x * lax.rsqrt(jnp.mean(x * x, axis=-1, keepdims=True) + EPS) * g


def _rmsnorm_kernel(x_ref, g_ref, o_ref):
    o_ref[...] = _rms(x_ref[...], g_ref[...]).astype(o_ref.dtype)


def _rmsnorm(x, g, out_dtype, bm=512):
    n, d = x.shape
    return pl.pallas_call(
        _rmsnorm_kernel,
        out_shape=jax.ShapeDtypeStruct((n, d), out_dtype),
        grid=(n // bm,),
        in_specs=[pl.BlockSpec((bm, d), lambda i: (i, 0)),
                  pl.BlockSpec((1, d), lambda i: (0, 0))],
        out_specs=pl.BlockSpec((bm, d), lambda i: (i, 0)),
        compiler_params=_params("parallel"),
        name="rmsnorm",
    )(x, g.reshape(1, d))


def _weight_spec(w, layer, bk, bn, index):
    if w.ndim == 2:
        return pl.BlockSpec((bk, bn), index)
    return pl.BlockSpec((None, bk, bn), lambda *g: (layer,) + tuple(index(*g)))


def _mm_kernel(*refs, n_a, act, has_res, n_side):
    a_refs, w_ref = refs[:n_a], refs[n_a]
    refs = refs[n_a + 1:]
    res_ref = refs[0] if has_res else None
    side_in = refs[has_res:has_res + n_side]
    o_ref = refs[has_res + n_side]
    side_out = refs[1 + has_res + n_side:]
    for src, dst in zip(side_in, side_out):
        dst[...] = src[...].astype(jnp.bfloat16)
    a = jnp.concatenate([r[...] for r in a_refs], axis=1)
    for c in range(o_ref.shape[1] // MM_CHUNK):
        cols = slice(c * MM_CHUNK, (c + 1) * MM_CHUNK)
        acc = jnp.dot(a, w_ref[:, cols].astype(jnp.bfloat16), preferred_element_type=jnp.float32)
        if act == "relu2":
            acc = jnp.square(jnp.maximum(acc, 0.0))
        if has_res:
            acc = acc + res_ref[:, cols]
        o_ref[:, cols] = acc.astype(o_ref.dtype)


def _matmul(a, w, *, bm, bn, out_dtype, act=None, res=None, layer=0, n_out=None,
            w_col_block=lambda j: j, round_also=(), name):
    a_parts = a if isinstance(a, tuple) else (a,)
    m = a_parts[0].shape[0]
    kdim = sum(p.shape[1] for p in a_parts)
    n = w.shape[-1] if n_out is None else n_out
    nj = n // bn
    in_specs = [pl.BlockSpec((bm, p.shape[1]), lambda i, j: (i, 0)) for p in a_parts]
    in_specs.append(_weight_spec(w, layer, kdim, bn, lambda i, j: (0, w_col_block(j))))
    args = [*a_parts, w]
    if res is not None:
        in_specs.append(pl.BlockSpec((bm, bn), lambda i, j: (i, j)))
        args.append(res)
    out_shape = [jax.ShapeDtypeStruct((m, n), out_dtype)]
    out_specs = [pl.BlockSpec((bm, bn), lambda i, j: (i, j))]
    for side in round_also:
        _, rows, cols = side.shape
        slab = rows // ((m // bm) * nj)
        in_specs.append(pl.BlockSpec((None, slab, cols), lambda i, j: (layer, i * nj + j, 0)))
        args.append(side)
        out_shape.append(jax.ShapeDtypeStruct((rows, cols), jnp.bfloat16))
        out_specs.append(pl.BlockSpec((slab, cols), lambda i, j: (i * nj + j, 0)))
    out = pl.pallas_call(
        functools.partial(_mm_kernel, n_a=len(a_parts), act=act, has_res=res is not None,
                          n_side=len(round_also)),
        out_shape=out_shape,
        grid=(m // bm, nj),
        in_specs=in_specs,
        out_specs=out_specs,
        compiler_params=_params("parallel", "parallel"),
        name=name,
    )(*args)
    return out if round_also else out[0]


def _mm_ksplit_kernel(a_ref, w_ref, res_ref, o_ref):
    @pl.when(pl.program_id(2) == 0)
    def _():
        o_ref[...] = res_ref[...]

    for c in range(o_ref.shape[1] // MM_CHUNK):
        cols = slice(c * MM_CHUNK, (c + 1) * MM_CHUNK)
        o_ref[:, cols] = (jnp.dot(a_ref[...], w_ref[:, cols], preferred_element_type=jnp.float32)
                          + o_ref[:, cols])


def _matmul_ksplit(a, w, res, *, bm, bn, bk, name):
    m, kdim = a.shape
    n = w.shape[1]
    return pl.pallas_call(
        _mm_ksplit_kernel,
        out_shape=jax.ShapeDtypeStruct((m, n), jnp.float32),
        grid=(m // bm, n // bn, kdim // bk),
        in_specs=[pl.BlockSpec((bm, bk), lambda i, j, k: (i, k)),
                  pl.BlockSpec((bk, bn), lambda i, j, k: (k, j)),
                  pl.BlockSpec((bm, bn), lambda i, j, k: (i, j))],
        out_specs=pl.BlockSpec((bm, bn), lambda i, j, k: (i, j)),
        compiler_params=_params("parallel", "parallel", "arbitrary"),
        name=name,
    )(a, w, res)


def _inproj_kernel(h_ref, w_ref, cos_ref, sa_ref, sb_ref, o_ref, acc_ref):
    @pl.when(pl.program_id(0) == 0)
    def _():
        acc_ref[...] = jnp.zeros_like(acc_ref)

    half_rows = acc_ref.shape[0] // 2
    for r0 in (0, half_rows):
        rows = slice(r0, r0 + half_rows)
        cos, sa, sb = cos_ref[rows, :], sa_ref[rows, :], sb_ref[rows, :]
        for c in range(o_ref.shape[1] // HEAD_DIM):
            cols = slice(c * HEAD_DIM, (c + 1) * HEAD_DIM)
            a = acc_ref[rows, cols]
            r = (a * cos + pltpu.roll(a, HEAD_DIM - ROT_DIM // 2, 1) * sa
                 + pltpu.roll(a, ROT_DIM // 2, 1) * sb)
            o_ref[rows, cols] = r.astype(o_ref.dtype)
    acc_ref[...] = jnp.dot(h_ref[...], w_ref[...], preferred_element_type=jnp.float32)


def _rope_tables(seq_len):
    half = ROT_DIM // 2
    pos = jnp.arange(seq_len, dtype=jnp.float32)
    inv_freq = ROPE_THETA ** (-jnp.arange(0, ROT_DIM, 2, dtype=jnp.float32) / ROT_DIM)
    ang = pos[:, None] * inv_freq[None, :]
    cos, sin = jnp.cos(ang), jnp.sin(ang)
    z = lambda w: jnp.zeros((seq_len, w), jnp.float32)
    cos_t = jnp.concatenate([cos, cos, jnp.ones((seq_len, HEAD_DIM - ROT_DIM), jnp.float32)], axis=1)
    sin_a = jnp.concatenate([-sin, z(HEAD_DIM - half)], axis=1)
    sin_b = jnp.concatenate([z(half), sin, z(HEAD_DIM - ROT_DIM)], axis=1)
    return cos_t, sin_a, sin_b


def _inproj_qk(h, w, seq_len, group_width, bm=1024, bn=1024):
    m, kdim = h.shape
    bpg = group_width // bn
    nj = 4 * bpg
    n_blocks = (m // bm) * nj
    cos_t, sin_a, sin_b = (jnp.stack([t * QK_SCALE, t]) for t in _rope_tables(seq_len))

    mul = lambda t: jnp.minimum(t, n_blocks - 1)
    fin = lambda t: jnp.maximum(t - 1, 0)

    def w_block(t):
        out_group = (mul(t) % nj) // bpg
        return 0, (out_group + out_group // 2) * bpg + mul(t) % bpg

    tspec = pl.BlockSpec(
        (None, bm, HEAD_DIM),
        lambda t: (((fin(t) % nj) // bpg) % 2, (fin(t) // nj) % (seq_len // bm), 0))
    return pl.pallas_call(
        _inproj_kernel,
        out_shape=jax.ShapeDtypeStruct((m, 4 * group_width), jnp.bfloat16),
        grid=(n_blocks + 1,),
        in_specs=[pl.BlockSpec((bm, kdim), lambda t: (mul(t) // nj, 0)),
                  pl.BlockSpec((kdim, bn), w_block),
                  tspec, tspec, tspec],
        out_specs=pl.BlockSpec((bm, bn), lambda t: (fin(t) // nj, fin(t) % nj)),
        scratch_shapes=[pltpu.VMEM((bm, bn), jnp.float32)],
        compiler_params=_params("arbitrary"),
        name="inproj_qk_rope",
    )(h, w, cos_t, sin_a, sin_b)


VT_PAD = 16


def _scores(k, q):
    return lax.dot_general(k, q, NT_DIMS, preferred_element_type=jnp.float32)


def _transposed_values(v_blk):
    vt = v_blk.astype(jnp.float32).T.astype(jnp.bfloat16)
    row = lax.broadcasted_iota(jnp.int32, (VT_PAD, v_blk.shape[0]), 0)
    return jnp.concatenate([vt, jnp.where(row == 0, 1.0, 0.0).astype(jnp.bfloat16)], axis=0)


def _weights(x):
    return jnp.exp2(x).astype(jnp.bfloat16)


def _diag_state(s, vt):
    key = lax.broadcasted_iota(jnp.int32, s.shape, 0)
    qry = lax.broadcasted_iota(jnp.int32, s.shape, 1)
    s = jnp.where(key <= qry, s, NEG_BIG)
    m = jnp.max(s, axis=0, keepdims=True)
    return m, jnp.dot(vt, _weights(s - m), preferred_element_type=jnp.float32)


def _normalised(acc):
    v_dim = acc.shape[0] - VT_PAD
    return acc[:v_dim] / acc[v_dim:v_dim + 1]


def _update_state(pieces, biases, vts, state):
    m, acc = state
    m_new = m
    for s, b in zip(pieces, biases):
        m_new = jnp.maximum(m_new, jnp.max(s, axis=0, keepdims=True) + b)
    acc = jnp.exp2(m - m_new) * acc
    for s, b, vt in zip(pieces, biases, vts):
        acc = acc + jnp.dot(vt, _weights(s - (m_new - b)), preferred_element_type=jnp.float32)
    return m_new, acc


STALE_MAX_CAP = 40.0


def _update_state_stale(pieces, biases, vts, state):
    m, acc = state
    m_new = m
    for s, b, vt in zip(pieces, biases, vts):
        m_new = jnp.maximum(m_new, jnp.max(s, axis=0, keepdims=True) + b)
        acc = acc + jnp.dot(vt, _weights(s - (m - b)), preferred_element_type=jnp.float32)
    rise = m_new - m
    return (m_new, jnp.exp2(-rise) * acc), rise


def _attend_groups(n_blocks, group_sizes, group_inputs, states, fin_ref):
    def fast(inputs, carry):
        states, worst = carry
        new, rises = zip(*(_update_state_stale(*inp, st) for inp, st in zip(inputs, states)))
        return tuple(new), functools.reduce(jnp.maximum, rises + (worst,))

    def exact(inputs, states):
        return tuple(_update_state(*inp, st) for inp, st in zip(inputs, states))

    def fold(update, carry):
        first = 0
        for idx, size in enumerate(group_sizes):
            slack = size - 1 if idx == len(group_sizes) - 1 else 0
            trips = lax.div(n_blocks - first + slack, size)
            carry = lax.fori_loop(
                0, trips,
                lambda i, c, size=size, first=first: update(group_inputs(first + i * size, size), c),
                carry)
            first = first + trips * size
        return carry

    def store(states):
        for c, (_, acc) in enumerate(states):
            fin_ref[c] = acc

    done, worst = fold(fast, (states, jnp.zeros_like(states[0][0])))
    store(done)

    @pl.when(jnp.max(worst) > STALE_MAX_CAP)
    def _():
        store(fold(exact, states))


class _LaggedGrid:
    def __init__(self, n_steps, n_groups, n_blocks):
        self.n_steps, self.n_groups, self.n_blocks = n_steps, n_groups, n_blocks

    def _decode(self, item, index):
        per_batch = self.n_groups * self.n_blocks
        return index(item // per_batch, (item // self.n_blocks) % self.n_groups,
                     item % self.n_blocks)

    def cur(self, index):
        return lambda t: self._decode(jnp.minimum(t, self.n_steps - 1), index)

    def prev(self, index):
        return lambda t: self._decode(jnp.maximum(t - 1, 0), index)


def _key_rows(k_ref, first_block, n_blocks, cols=slice(None)):
    rows = n_blocks * ATT_BLOCK
    return k_ref[pl.ds(pl.multiple_of(first_block * ATT_BLOCK, ATT_BLOCK), rows), cols]


def _moba_kernel(q_ref, k_ref, v_ref, g_ref, o_ref, kmh_ref, kml_ref, vt_ref, bias_ref, fin_ref,
                 *, n_blocks, n_heads, n_steps):
    t = pl.program_id(0)
    qi = lax.rem(jnp.minimum(t, n_steps - 1), n_blocks)
    heads = [slice(h * HEAD_DIM, (h + 1) * HEAD_DIM) for h in range(n_heads)]

    @pl.when(t == 0)
    def _():
        fin_ref[...] = jnp.ones_like(fin_ref)

    @pl.when(qi == 0)
    def _():
        for h, cols in enumerate(heads):
            kf = k_ref[:, cols].astype(jnp.float32).reshape(n_blocks, MOBA_BLOCK, HEAD_DIM)
            km = jnp.sum(kf, axis=1) * (1.0 / MOBA_BLOCK)
            hi = km.astype(jnp.bfloat16)
            kmh_ref[h] = hi
            kml_ref[h] = (km - hi.astype(jnp.float32)).astype(jnp.bfloat16)
            for n in range(n_blocks):
                vt_ref[h * n_blocks + n] = _transposed_values(
                    v_ref[n * ATT_BLOCK:(n + 1) * ATT_BLOCK, cols])

    for h, cols in enumerate(heads):
        o_ref[:, cols] = _rms(_normalised(fin_ref[h]).T, g_ref[:, cols]).astype(o_ref.dtype)

    qs = [q_ref[:, cols] for cols in heads]
    fused = [_scores(jnp.concatenate([kmh_ref[h], kml_ref[h], _key_rows(k_ref, qi, 1, cols)], axis=0),
                     q) for h, (q, cols) in enumerate(zip(qs, heads))]
    for h in range(n_heads):
        gate = fused[h][:n_blocks] + fused[h][n_blocks:2 * n_blocks]
        blk = lax.broadcasted_iota(jnp.int32, gate.shape, 0)
        past = blk < qi
        g = jnp.where(past, gate, -jnp.inf)
        sel = jnp.zeros(gate.shape, jnp.float32)
        for _ in range(MOBA_TOPK):
            mx = jnp.max(g, axis=0, keepdims=True)
            first = jnp.min(jnp.where(g == mx, blk, n_blocks), axis=0, keepdims=True)
            hit = blk == first
            sel = jnp.where(hit, jnp.where(past, 1.0, 0.0), sel)
            g = jnp.where(hit, -jnp.inf, g)
        bias_ref[h] = jnp.where(sel > 0.0, 0.0, NEG_BIG)

    states = tuple(_diag_state(s[2 * n_blocks:], vt_ref[h * n_blocks + qi])
                   for h, s in enumerate(fused))

    def group_inputs(first, size):
        scores = [_scores(_key_rows(k_ref, first, size, cols), q) for q, cols in zip(qs, heads)]
        inputs = []
        for h, s in enumerate(scores):
            pieces = [s[n * ATT_BLOCK:(n + 1) * ATT_BLOCK] for n in range(size)]
            biases = [bias_ref[h, pl.ds(first + n, 1), :] for n in range(size)]
            vts = [vt_ref[h * n_blocks + first + n] for n in range(size)]
            inputs.append((pieces, biases, vts))
        return inputs

    _attend_groups(qi, MOBA_GROUPS, group_inputs, states, fin_ref)


def _moba_attention(qk, vproj, out_g, n_heads, heads_per_step=8):
    b, s, _ = qk.shape
    n_blocks = s // MOBA_BLOCK
    hp = heads_per_step
    w = hp * HEAD_DIM
    n_groups = n_heads // hp
    n_steps = b * n_groups * n_blocks
    grid = _LaggedGrid(n_steps, n_groups, n_blocks)
    return pl.pallas_call(
        functools.partial(_moba_kernel, n_blocks=n_blocks, n_heads=hp, n_steps=n_steps),
        out_shape=jax.ShapeDtypeStruct((b, s, n_heads * HEAD_DIM), jnp.bfloat16),
        grid=(n_steps + 1,),
        in_specs=[pl.BlockSpec((None, MOBA_BLOCK, w), grid.cur(lambda bi, h, qi: (bi, qi, h))),
                  pl.BlockSpec((None, s, w), grid.cur(lambda bi, h, qi: (bi, 0, n_groups + h))),
                  pl.BlockSpec((None, s, w), grid.cur(lambda bi, h, qi: (bi, 0, h))),
                  pl.BlockSpec((None, 1, w), grid.prev(lambda bi, h, qi: (h, 0, 0)))],
        out_specs=pl.BlockSpec((None, MOBA_BLOCK, w), grid.prev(lambda bi, h, qi: (bi, qi, h))),
        scratch_shapes=[pltpu.VMEM((hp, n_blocks, HEAD_DIM), jnp.bfloat16),
                        pltpu.VMEM((hp, n_blocks, HEAD_DIM), jnp.bfloat16),
                        pltpu.VMEM((hp * n_blocks, HEAD_DIM + VT_PAD, MOBA_BLOCK), jnp.bfloat16),
                        pltpu.VMEM((hp, n_blocks, MOBA_BLOCK), jnp.float32),
                        pltpu.VMEM((hp, HEAD_DIM + VT_PAD, MOBA_BLOCK), jnp.float32)],
        compiler_params=_params("arbitrary"),
        name="moba_attention",
    )(qk, qk, vproj, out_g.reshape(n_groups, 1, w))


def _diff_kernel(lq1_ref, lk1_ref, lq2_ref, lk2_ref, q_ref, k_ref, v_ref, g_ref,
                 o_ref, vt_ref, fin_ref, *, lambda_init, n_heads, n_steps):
    t = pl.program_id(0)
    n_blocks = vt_ref.shape[0] // n_heads
    qi = lax.rem(jnp.minimum(t, n_steps - 1), n_blocks)
    vw = 2 * HEAD_DIM

    @pl.when(t == 0)
    def _():
        fin_ref[...] = jnp.ones_like(fin_ref)

    @pl.when(qi == 0)
    def _():
        for h in range(n_heads):
            for n in range(n_blocks):
                vt_ref[h * n_blocks + n] = _transposed_values(
                    v_ref[n * ATT_BLOCK:(n + 1) * ATT_BLOCK, h * vw:(h + 1) * vw])

    lam = (jnp.exp(jnp.sum(lq1_ref[...] * lk1_ref[...], axis=1, keepdims=True))
           - jnp.exp(jnp.sum(lq2_ref[...] * lk2_ref[...], axis=1, keepdims=True))
           + lambda_init)
    for h in range(n_heads):
        o = (_normalised(fin_ref[2 * h]) - lam * _normalised(fin_ref[2 * h + 1])).T
        o_ref[:, h * vw:(h + 1) * vw] = (_rms(o, g_ref[...]) * (1.0 - lambda_init)).astype(o_ref.dtype)

    subs = [slice(j * HEAD_DIM, (j + 1) * HEAD_DIM) for j in range(2 * n_heads)]
    qs = [q_ref[:, cols] for cols in subs]
    diag = [_scores(_key_rows(k_ref, qi, 1, cols), q) for q, cols in zip(qs, subs)]
    states = tuple(_diag_state(s, vt_ref[(j // 2) * n_blocks + qi]) for j, s in enumerate(diag))

    def group_inputs(first, size):
        scores = [_scores(_key_rows(k_ref, first, size, cols), q) for q, cols in zip(qs, subs)]
        biases = [jnp.where(first + n < qi, 0.0, NEG_BIG) for n in range(size)]
        inputs = []
        for j, s in enumerate(scores):
            vts = [vt_ref[(j // 2) * n_blocks + first + n] for n in range(size)]
            pieces = [s[n * ATT_BLOCK:(n + 1) * ATT_BLOCK] for n in range(size)]
            inputs.append((pieces, biases, vts))
        return inputs

    _attend_groups(qi, DIFF_GROUPS, group_inputs, states, fin_ref)


def _diff_attention(qk, vproj, lq1, lk1, lq2, lk2, subln_g, n_heads, lambda_init,
                    heads_per_step=4):
    b, s, qk_width = qk.shape
    v_col0 = vproj.shape[2] // 2
    col0 = qk_width // 2
    hp = heads_per_step
    w = hp * 2 * HEAD_DIM
    n_groups = n_heads // hp
    tq = ATT_BLOCK
    n_steps = b * n_groups * (s // tq)
    grid = _LaggedGrid(n_steps, n_groups, s // tq)
    vec = lambda x: x.reshape(1, HEAD_DIM).astype(jnp.float32)
    vspec = pl.BlockSpec((1, HEAD_DIM), lambda t: (0, 0))
    qb, kb, vb = col0 // w, col0 // w + n_groups, v_col0 // w
    return pl.pallas_call(
        functools.partial(_diff_kernel, lambda_init=lambda_init, n_heads=hp, n_steps=n_steps),
        out_shape=jax.ShapeDtypeStruct((b, s, n_heads * 2 * HEAD_DIM), jnp.bfloat16),
        grid=(n_steps + 1,),
        in_specs=[vspec, vspec, vspec, vspec,
                  pl.BlockSpec((None, tq, w), grid.cur(lambda bi, h, qi: (bi, qi, qb + h))),
                  pl.BlockSpec((None, s, w), grid.cur(lambda bi, h, qi: (bi, 0, kb + h))),
                  pl.BlockSpec((None, s, w), grid.cur(lambda bi, h, qi: (bi, 0, vb + h)),
                               pipeline_mode=pl.Buffered(1)),
                  pl.BlockSpec((1, 2 * HEAD_DIM), lambda t: (0, 0))],
        out_specs=pl.BlockSpec((None, tq, w), grid.prev(lambda bi, h, qi: (bi, qi, h))),
        scratch_shapes=[pltpu.VMEM((hp * (s // ATT_BLOCK), 2 * HEAD_DIM + VT_PAD, ATT_BLOCK),
                                   jnp.bfloat16),
                        pltpu.VMEM((2 * hp, 2 * HEAD_DIM + VT_PAD, ATT_BLOCK), jnp.float32)],
        compiler_params=_params("arbitrary"),
        name="diff_attention",
    )(vec(lq1), vec(lk1), vec(lq2), vec(lk2), qk, qk, vproj, subln_g.reshape(1, 2 * HEAD_DIM))


def _cross_kernel(x_ref, gc_ref, wq_ref, kv_ref, wo_ref, gm_ref, x2_ref, hm_ref):
    c = QK_SCALE
    x = x_ref[...]
    hc = _rms(x, gc_ref[...]).astype(jnp.bfloat16)
    q = jnp.dot(hc, wq_ref[...], preferred_element_type=jnp.float32).astype(jnp.bfloat16)
    kv_w = MEM_HEADS * HEAD_DIM
    outs = []
    for h in range(MEM_HEADS):
        k = kv_ref[:, h * HEAD_DIM:(h + 1) * HEAD_DIM]
        v = kv_ref[:, kv_w + h * HEAD_DIM:kv_w + (h + 1) * HEAD_DIM]
        s = lax.dot_general(q[:, h * HEAD_DIM:(h + 1) * HEAD_DIM], k, NT_DIMS,
                            preferred_element_type=jnp.float32)
        m = jnp.max(s, axis=1, keepdims=True)
        p = jnp.exp2((s - m) * c)
        l = jnp.sum(p, axis=1, keepdims=True)
        o = jnp.dot(p.astype(jnp.bfloat16), v, preferred_element_type=jnp.float32) / l
        outs.append(o.astype(jnp.bfloat16))
    o = jnp.concatenate(outs, axis=1)
    x2 = x + jnp.dot(o, wo_ref[...], preferred_element_type=jnp.float32)
    x2_ref[...] = x2
    hm_ref[...] = _rms(x2, gm_ref[...]).astype(hm_ref.dtype)


def _cross_sublayer(x, kv, g_cross, w_cq, w_co, g_mlp, seq_len, mem_len, tq=512):
    n, d = x.shape
    qw = w_cq.shape[1]
    const = lambda i: (0, 0)
    return pl.pallas_call(
        _cross_kernel,
        out_shape=(jax.ShapeDtypeStruct((n, d), jnp.float32),
                   jax.ShapeDtypeStruct((n, d), jnp.bfloat16)),
        grid=(n // tq,),
        in_specs=[pl.BlockSpec((tq, d), lambda i: (i, 0)),
                  pl.BlockSpec((1, d), const),
                  pl.BlockSpec((d, qw), const, pipeline_mode=pl.Buffered(1)),
                  pl.BlockSpec((mem_len, 2 * qw), lambda i: (i // (seq_len // tq), 0)),
                  pl.BlockSpec((qw, d), const, pipeline_mode=pl.Buffered(1)),
                  pl.BlockSpec((1, d), const)],
        out_specs=(pl.BlockSpec((tq, d), lambda i: (i, 0)),
                   pl.BlockSpec((tq, d), lambda i: (i, 0))),
        compiler_params=_params("parallel"),
        name="cross_sublayer",
    )(x, g_cross.reshape(1, d), w_cq, kv, w_co, g_mlp.reshape(1, d))


def kernel(x, mem, ln_mix_g, w_in, moba_out_g, lambda_q1, lambda_k1, lambda_q2, lambda_k2,
           diff_subln_g, w_out, ln_cross_g, ln_mem_g, w_cq, w_ckv, w_co, ln_mlp_g,
           w_up, w_down, final_g):
    b, s, d = x.shape
    mem_len = mem.shape[1]
    depth = w_in.shape[0]
    mix_width = w_out.shape[1]
    moba_width = mix_width // 2
    moba_heads = moba_width // HEAD_DIM
    diff_heads = (mix_width - moba_width) // (2 * HEAD_DIM)
    bf = jnp.bfloat16

    xf = x.reshape(b * s, d)
    memf = mem.reshape(b * mem_len, d)
    for l in range(depth):
        lambda_init = 0.8 - 0.6 * math.exp(-0.3 * l)

        h = _rmsnorm(xf, ln_mix_g[l], bf)
        w_in_bf = w_in[l].astype(bf)
        qk = _inproj_qk(h, w_in_bf, s, moba_width).reshape(b, s, -1)
        bpg = moba_width // 1024
        vproj, w_out_bf, w_up_bf = _matmul(
            h, w_in_bf, bm=1024, bn=1024, out_dtype=bf, n_out=2 * moba_width,
            w_col_block=lambda j: (2 + 3 * (j // bpg)) * bpg + j % bpg,
            round_also=(w_out, w_up), layer=l, name="inproj_v")
        vproj = vproj.reshape(b, s, -1)
        mo = _moba_attention(qk, vproj, moba_out_g[l], moba_heads)
        do = _diff_attention(qk, vproj, lambda_q1[l], lambda_k1[l], lambda_q2[l], lambda_k2[l],
                             diff_subln_g[l], diff_heads, lambda_init)
        x1 = _matmul((mo.reshape(b * s, -1), do.reshape(b * s, -1)), w_out_bf, bm=1024, bn=1024,
                     out_dtype=jnp.float32, res=xf, name="outproj")

        mn = _rmsnorm(memf, ln_mem_g[l], bf)
        kv = _matmul(mn, w_ckv, layer=l, bm=b * mem_len, bn=512, out_dtype=bf, name="mem_kv")
        x2, hm = _cross_sublayer(x1, kv, ln_cross_g[l], w_cq[l].astype(bf), w_co[l].astype(bf),
                                 ln_mlp_g[l], s, mem_len)

        u, w_down_bf = _matmul(hm, w_up_bf, bm=1024, bn=1024, out_dtype=bf, act="relu2",
                               round_also=(w_down,), layer=l, name="mlp_up")
        xf = _matmul_ksplit(u, w_down_bf, x2, bm=1024, bn=1024, bk=4096, name="mlp_down")
    return _rmsnorm(xf, final_g, jnp.float32).reshape(b, s, d)
```

```python
import functools
import math

import jax
import jax.numpy as jnp
from jax import lax
from jax.experimental import pallas as pl
from jax.experimental.pallas import tpu as pltpu

HEAD_DIM = 128
MOBA_BLOCK = 256
MOBA_TOPK = 3
ROT_DIM = HEAD_DIM // 4
ROPE_THETA = 500000.0
MEM_HEADS = 4
EPS = 1e-5

VMEM_LIMIT_BYTES = 60 * 2**20
NEG_BIG = -1e30
LOG2E = 1.4426950408889634
QK_SCALE = HEAD_DIM ** -0.5 * LOG2E
NT_DIMS = (((1,), (1,)), ((), ()))
ATT_BLOCK = 256
MOBA_GROUPS = (4, 2)
DIFF_GROUPS = (4, 2)
MM_CHUNK = 256


def _params(*sem):
    return pltpu.CompilerParams(dimension_semantics=sem, vmem_limit_bytes=VMEM_LIMIT_BYTES)


def _rms(x, g):
    return x * lax.rsqrt(jnp.mean(x * x, axis=-1, keepdims=True) + EPS) * g


def _rmsnorm_kernel(x_ref, g_ref, o_ref):
    o_ref[...] = _rms(x_ref[...], g_ref[...]).astype(o_ref.dtype)


def _rmsnorm(x, g, out_dtype, bm=512):
    n, d = x.shape
    return pl.pallas_call(
        _rmsnorm_kernel,
        out_shape=jax.ShapeDtypeStruct((n, d), out_dtype),
        grid=(n // bm,),
        in_specs=[pl.BlockSpec((bm, d), lambda i: (i, 0)),
                  pl.BlockSpec((1, d), lambda i: (0, 0))],
        out_specs=pl.BlockSpec((bm, d), lambda i: (i, 0)),
        compiler_params=_params("parallel"),
        name="rmsnorm",
    )(x, g.reshape(1, d))


def _weight_spec(w, layer, bk, bn, index):
    if w.ndim == 2:
        return pl.BlockSpec((bk, bn), index)
    return pl.BlockSpec((None, bk, bn), lambda *g: (layer,) + tuple(index(*g)))


def _mm_kernel(*refs, n_a, act, has_res, n_side):
    a_refs, w_ref = refs[:n_a], refs[n_a]
    refs = refs[n_a + 1:]
    res_ref = refs[0] if has_res else None
    side_in = refs[has_res:has_res + n_side]
    o_ref = refs[has_res + n_side]
    side_out = refs[1 + has_res + n_side:]
    for src, dst in zip(side_in, side_out):
        dst[...] = src[...].astype(jnp.bfloat16)
    a = jnp.concatenate([r[...] for r in a_refs], axis=1)
    for c in range(o_ref.shape[1] // MM_CHUNK):
        cols = slice(c * MM_CHUNK, (c + 1) * MM_CHUNK)
        acc = jnp.dot(a, w_ref[:, cols].astype(jnp.bfloat16), preferred_element_type=jnp.float32)
        if act == "relu2":
            acc = jnp.square(jnp.maximum(acc, 0.0))
        if has_res:
            acc = acc + res_ref[:, cols]
        o_ref[:, cols] = acc.astype(o_ref.dtype)


def _matmul(a, w, *, bm, bn, out_dtype, act=None, res=None, layer=0, n_out=None,
            w_col_block=lambda j: j, round_also=(), name):
    a_parts = a if isinstance(a, tuple) else (a,)
    m = a_parts[0].shape[0]
    kdim = sum(p.shape[1] for p in a_parts)
    n = w.shape[-1] if n_out is None else n_out
    nj = n // bn
    in_specs = [pl.BlockSpec((bm, p.shape[1]), lambda i, j: (i, 0)) for p in a_parts]
    in_specs.append(_weight_spec(w, layer, kdim, bn, lambda i, j: (0, w_col_block(j))))
    args = [*a_parts, w]
    if res is not None:
        in_specs.append(pl.BlockSpec((bm, bn), lambda i, j: (i, j)))
        args.append(res)
    out_shape = [jax.ShapeDtypeStruct((m, n), out_dtype)]
    out_specs = [pl.BlockSpec((bm, bn), lambda i, j: (i, j))]
    for side in round_also:
        _, rows, cols = side.shape
        slab = rows // ((m // bm) * nj)
        in_specs.append(pl.BlockSpec((None, slab, cols), lambda i, j: (layer, i * nj + j, 0)))
        args.append(side)
        out_shape.append(jax.ShapeDtypeStruct((rows, cols), jnp.bfloat16))
        out_specs.append(pl.BlockSpec((slab, cols), lambda i, j: (i * nj + j, 0)))
    out = pl.pallas_call(
        functools.partial(_mm_kernel, n_a=len(a_parts), act=act, has_res=res is not None,
                          n_side=len(round_also)),
        out_shape=out_shape,
        grid=(m // bm, nj),
        in_specs=in_specs,
        out_specs=out_specs,
        compiler_params=_params("parallel", "parallel"),
        name=name,
    )(*args)
    return out if round_also else out[0]


def _norm_mm_kernel(x_ref, g_ref, w_ref, o_ref):
    h = _rms(x_ref[...], g_ref[...]).astype(jnp.bfloat16)
    for c in range(o_ref.shape[1] // MM_CHUNK):
        cols = slice(c * MM_CHUNK, (c + 1) * MM_CHUNK)
        o_ref[:, cols] = jnp.dot(h, w_ref[:, cols].astype(jnp.bfloat16),
                                 preferred_element_type=jnp.float32).astype(o_ref.dtype)


def _norm_matmul(x, g, w, layer, *, bm, bn, name):
    m, d = x.shape
    n = w.shape[-1]
    return pl.pallas_call(
        _norm_mm_kernel,
        out_shape=jax.ShapeDtypeStruct((m, n), jnp.bfloat16),
        grid=(m // bm, n // bn),
        in_specs=[pl.BlockSpec((bm, d), lambda i, j: (i, 0)),
                  pl.BlockSpec((1, d), lambda i, j: (0, 0)),
                  pl.BlockSpec((None, d, bn), lambda i, j: (layer, 0, j))],
        out_specs=pl.BlockSpec((bm, bn), lambda i, j: (i, j)),
        compiler_params=_params("parallel", "parallel"),
        name=name,
    )(x, g.reshape(1, d), w)


def _mm_ksplit_kernel(a_ref, w_ref, res_ref, o_ref):
    @pl.when(pl.program_id(2) == 0)
    def _():
        o_ref[...] = res_ref[...]

    for c in range(o_ref.shape[1] // MM_CHUNK):
        cols = slice(c * MM_CHUNK, (c + 1) * MM_CHUNK)
        o_ref[:, cols] = (jnp.dot(a_ref[...], w_ref[:, cols], preferred_element_type=jnp.float32)
                          + o_ref[:, cols])


def _matmul_ksplit(a, w, res, *, bm, bn, bk, name):
    m, kdim = a.shape
    n = w.shape[1]
    return pl.pallas_call(
        _mm_ksplit_kernel,
        out_shape=jax.ShapeDtypeStruct((m, n), jnp.float32),
        grid=(m // bm, n // bn, kdim // bk),
        in_specs=[pl.BlockSpec((bm, bk), lambda i, j, k: (i, k)),
                  pl.BlockSpec((bk, bn), lambda i, j, k: (k, j)),
                  pl.BlockSpec((bm, bn), lambda i, j, k: (i, j))],
        out_specs=pl.BlockSpec((bm, bn), lambda i, j, k: (i, j)),
        compiler_params=_params("parallel", "parallel", "arbitrary"),
        name=name,
    )(a, w, res)


def _inproj_kernel(h_ref, w_ref, cos_ref, sa_ref, sb_ref, o_ref, acc_ref):
    @pl.when(pl.program_id(0) == 0)
    def _():
        acc_ref[...] = jnp.zeros_like(acc_ref)

    half_rows = acc_ref.shape[0] // 2
    for r0 in (0, half_rows):
        rows = slice(r0, r0 + half_rows)
        cos, sa, sb = cos_ref[rows, :], sa_ref[rows, :], sb_ref[rows, :]
        for c in range(o_ref.shape[1] // HEAD_DIM):
            cols = slice(c * HEAD_DIM, (c + 1) * HEAD_DIM)
            a = acc_ref[rows, cols]
            r = (a * cos + pltpu.roll(a, HEAD_DIM - ROT_DIM // 2, 1) * sa
                 + pltpu.roll(a, ROT_DIM // 2, 1) * sb)
            o_ref[rows, cols] = r.astype(o_ref.dtype)
    acc_ref[...] = jnp.dot(h_ref[...], w_ref[...], preferred_element_type=jnp.float32)


def _rope_tables(seq_len):
    half = ROT_DIM // 2
    pos = jnp.arange(seq_len, dtype=jnp.float32)
    inv_freq = ROPE_THETA ** (-jnp.arange(0, ROT_DIM, 2, dtype=jnp.float32) / ROT_DIM)
    ang = pos[:, None] * inv_freq[None, :]
    cos, sin = jnp.cos(ang), jnp.sin(ang)
    z = lambda w: jnp.zeros((seq_len, w), jnp.float32)
    cos_t = jnp.concatenate([cos, cos, jnp.ones((seq_len, HEAD_DIM - ROT_DIM), jnp.float32)], axis=1)
    sin_a = jnp.concatenate([-sin, z(HEAD_DIM - half)], axis=1)
    sin_b = jnp.concatenate([z(half), sin, z(HEAD_DIM - ROT_DIM)], axis=1)
    return cos_t, sin_a, sin_b


def _inproj_qk(h, w, seq_len, group_width, bm=1024, bn=1024):
    m, kdim = h.shape
    bpg = group_width // bn
    nj = 4 * bpg
    n_blocks = (m // bm) * nj
    cos_t, sin_a, sin_b = (jnp.stack([t * QK_SCALE, t]) for t in _rope_tables(seq_len))

    mul = lambda t: jnp.minimum(t, n_blocks - 1)
    fin = lambda t: jnp.maximum(t - 1, 0)

    def w_block(t):
        out_group = (mul(t) % nj) // bpg
        return 0, (out_group + out_group // 2) * bpg + mul(t) % bpg

    tspec = pl.BlockSpec(
        (None, bm, HEAD_DIM),
        lambda t: (((fin(t) % nj) // bpg) % 2, (fin(t) // nj) % (seq_len // bm), 0))
    return pl.pallas_call(
        _inproj_kernel,
        out_shape=jax.ShapeDtypeStruct((m, 4 * group_width), jnp.bfloat16),
        grid=(n_blocks + 1,),
        in_specs=[pl.BlockSpec((bm, kdim), lambda t: (mul(t) // nj, 0)),
                  pl.BlockSpec((kdim, bn), w_block),
                  tspec, tspec, tspec],
        out_specs=pl.BlockSpec((bm, bn), lambda t: (fin(t) // nj, fin(t) % nj)),
        scratch_shapes=[pltpu.VMEM((bm, bn), jnp.float32)],
        compiler_params=_params("arbitrary"),
        name="inproj_qk_rope",
    )(h, w, cos_t, sin_a, sin_b)


VT_PAD = 16


def _scores(k, q):
    return lax.dot_general(k, q, NT_DIMS, preferred_element_type=jnp.float32)


def _transposed_values(v_blk):
    vt = v_blk.astype(jnp.float32).T.astype(jnp.bfloat16)
    row = lax.broadcasted_iota(jnp.int32, (VT_PAD, v_blk.shape[0]), 0)
    return jnp.concatenate([vt, jnp.where(row == 0, 1.0, 0.0).astype(jnp.bfloat16)], axis=0)


def _weights(x):
    return jnp.exp2(x).astype(jnp.bfloat16)


def _diag_state(s, vt):
    key = lax.broadcasted_iota(jnp.int32, s.shape, 0)
    qry = lax.broadcasted_iota(jnp.int32, s.shape, 1)
    s = jnp.where(key <= qry, s, NEG_BIG)
    m = jnp.max(s, axis=0, keepdims=True)
    return m, jnp.dot(vt, _weights(s - m), preferred_element_type=jnp.float32)


def _normalised(acc):
    v_dim = acc.shape[0] - VT_PAD
    return acc[:v_dim] / acc[v_dim:v_dim + 1]


def _update_state(pieces, biases, vts, state):
    m, acc = state
    m_new = m
    for s, b in zip(pieces, biases):
        m_new = jnp.maximum(m_new, jnp.max(s, axis=0, keepdims=True) + b)
    acc = jnp.exp2(m - m_new) * acc
    for s, b, vt in zip(pieces, biases, vts):
        acc = acc + jnp.dot(vt, _weights(s - (m_new - b)), preferred_element_type=jnp.float32)
    return m_new, acc


STALE_MAX_CAP = 40.0


def _update_state_stale(pieces, biases, vts, state):
    m, acc = state
    m_new = m
    for s, b, vt in zip(pieces, biases, vts):
        m_new = jnp.maximum(m_new, jnp.max(s, axis=0, keepdims=True) + b)
        acc = acc + jnp.dot(vt, _weights(s - (m - b)), preferred_element_type=jnp.float32)
    rise = m_new - m
    return (m_new, jnp.exp2(-rise) * acc), rise


def _attend_groups(n_blocks, group_sizes, group_inputs, states, fin_ref):
    def fast(inputs, carry):
        states, worst = carry
        new, rises = zip(*(_update_state_stale(*inp, st) for inp, st in zip(inputs, states)))
        return tuple(new), functools.reduce(jnp.maximum, rises + (worst,))

    def exact(inputs, states):
        return tuple(_update_state(*inp, st) for inp, st in zip(inputs, states))

    def fold(update, carry):
        first = 0
        for idx, size in enumerate(group_sizes):
            slack = size - 1 if idx == len(group_sizes) - 1 else 0
            trips = lax.div(n_blocks - first + slack, size)
            carry = lax.fori_loop(
                0, trips,
                lambda i, c, size=size, first=first: update(group_inputs(first + i * size, size), c),
                carry)
            first = first + trips * size
        return carry

    def store(states):
        for c, (_, acc) in enumerate(states):
            fin_ref[c] = acc

    done, worst = fold(fast, (states, jnp.zeros_like(states[0][0])))
    store(done)

    @pl.when(jnp.max(worst) > STALE_MAX_CAP)
    def _():
        store(fold(exact, states))


class _LaggedGrid:
    def __init__(self, n_steps, n_groups, n_blocks):
        self.n_steps, self.n_groups, self.n_blocks = n_steps, n_groups, n_blocks

    def _decode(self, item, index):
        per_batch = self.n_groups * self.n_blocks
        return index(item // per_batch, (item // self.n_blocks) % self.n_groups,
                     item % self.n_blocks)

    def cur(self, index):
        return lambda t: self._decode(jnp.minimum(t, self.n_steps - 1), index)

    def prev(self, index):
        return lambda t: self._decode(jnp.maximum(t - 1, 0), index)


def _key_rows(k_ref, first_block, n_blocks, cols=slice(None)):
    rows = n_blocks * ATT_BLOCK
    return k_ref[pl.ds(pl.multiple_of(first_block * ATT_BLOCK, ATT_BLOCK), rows), cols]


def _moba_kernel(q_ref, k_ref, v_ref, g_ref, o_ref, kmh_ref, kml_ref, vt_ref, bias_ref, fin_ref,
                 *, n_blocks, n_heads, n_steps):
    t = pl.program_id(0)
    qi = lax.rem(jnp.minimum(t, n_steps - 1), n_blocks)
    heads = [slice(h * HEAD_DIM, (h + 1) * HEAD_DIM) for h in range(n_heads)]

    @pl.when(t == 0)
    def _():
        fin_ref[...] = jnp.ones_like(fin_ref)

    @pl.when(qi == 0)
    def _():
        for h, cols in enumerate(heads):
            kf = k_ref[:, cols].astype(jnp.float32).reshape(n_blocks, MOBA_BLOCK, HEAD_DIM)
            km = jnp.sum(kf, axis=1) * (1.0 / MOBA_BLOCK)
            hi = km.astype(jnp.bfloat16)
            kmh_ref[h] = hi
            kml_ref[h] = (km - hi.astype(jnp.float32)).astype(jnp.bfloat16)
            for n in range(n_blocks):
                vt_ref[h * n_blocks + n] = _transposed_values(
                    v_ref[n * ATT_BLOCK:(n + 1) * ATT_BLOCK, cols])

    for h, cols in enumerate(heads):
        o_ref[:, cols] = _rms(_normalised(fin_ref[h]).T, g_ref[:, cols]).astype(o_ref.dtype)

    qs = [q_ref[:, cols] for cols in heads]
    for h, q in enumerate(qs):
        gate = (lax.dot_general(kmh_ref[h], q, NT_DIMS, preferred_element_type=jnp.float32)
                + lax.dot_general(kml_ref[h], q, NT_DIMS, preferred_element_type=jnp.float32))
        blk = lax.broadcasted_iota(jnp.int32, gate.shape, 0)
        past = blk < qi
        g = jnp.where(past, gate, -jnp.inf)
        sel = jnp.zeros(gate.shape, jnp.float32)
        for _ in range(MOBA_TOPK):
            mx = jnp.max(g, axis=0, keepdims=True)
            first = jnp.min(jnp.where(g == mx, blk, n_blocks), axis=0, keepdims=True)
            hit = blk == first
            sel = jnp.where(hit, jnp.where(past, 1.0, 0.0), sel)
            g = jnp.where(hit, -jnp.inf, g)
        bias_ref[h] = jnp.where(sel > 0.0, 0.0, NEG_BIG)

    diag = [_scores(_key_rows(k_ref, qi, 1, cols), q) for q, cols in zip(qs, heads)]
    states = tuple(_diag_state(s, vt_ref[h * n_blocks + qi]) for h, s in enumerate(diag))

    def group_inputs(first, size):
        scores = [_scores(_key_rows(k_ref, first, size, cols), q) for q, cols in zip(qs, heads)]
        inputs = []
        for h, s in enumerate(scores):
            pieces = [s[n * ATT_BLOCK:(n + 1) * ATT_BLOCK] for n in range(size)]
            biases = [bias_ref[h, pl.ds(first + n, 1), :] for n in range(size)]
            vts = [vt_ref[h * n_blocks + first + n] for n in range(size)]
            inputs.append((pieces, biases, vts))
        return inputs

    _attend_groups(qi, MOBA_GROUPS, group_inputs, states, fin_ref)


def _moba_attention(qk, vproj, out_g, n_heads, heads_per_step=8):
    b, s, _ = qk.shape
    n_blocks = s // MOBA_BLOCK
    hp = heads_per_step
    w = hp * HEAD_DIM
    n_groups = n_heads // hp
    n_steps = b * n_groups * n_blocks
    grid = _LaggedGrid(n_steps, n_groups, n_blocks)
    return pl.pallas_call(
        functools.partial(_moba_kernel, n_blocks=n_blocks, n_heads=hp, n_steps=n_steps),
        out_shape=jax.ShapeDtypeStruct((b, s, n_heads * HEAD_DIM), jnp.bfloat16),
        grid=(n_steps + 1,),
        in_specs=[pl.BlockSpec((None, MOBA_BLOCK, w), grid.cur(lambda bi, h, qi: (bi, qi, h))),
                  pl.BlockSpec((None, s, w), grid.cur(lambda bi, h, qi: (bi, 0, n_groups + h))),
                  pl.BlockSpec((None, s, w), grid.cur(lambda bi, h, qi: (bi, 0, h))),
                  pl.BlockSpec((None, 1, w), grid.prev(lambda bi, h, qi: (h, 0, 0)))],
        out_specs=pl.BlockSpec((None, MOBA_BLOCK, w), grid.prev(lambda bi, h, qi: (bi, qi, h))),
        scratch_shapes=[pltpu.VMEM((hp, n_blocks, HEAD_DIM), jnp.bfloat16),
                        pltpu.VMEM((hp, n_blocks, HEAD_DIM), jnp.bfloat16),
                        pltpu.VMEM((hp * n_blocks, HEAD_DIM + VT_PAD, MOBA_BLOCK), jnp.bfloat16),
                        pltpu.VMEM((hp, n_blocks, MOBA_BLOCK), jnp.float32),
                        pltpu.VMEM((hp, HEAD_DIM + VT_PAD, MOBA_BLOCK), jnp.float32)],
        compiler_params=_params("arbitrary"),
        name="moba_attention",
    )(qk, qk, vproj, out_g.reshape(n_groups, 1, w))


def _diff_kernel(lq1_ref, lk1_ref, lq2_ref, lk2_ref, q_ref, k_ref, v_ref, g_ref,
                 o_ref, vt_ref, fin_ref, *, lambda_init, n_heads, n_steps):
    t = pl.program_id(0)
    n_blocks = vt_ref.shape[0] // n_heads
    qi = lax.rem(jnp.minimum(t, n_steps - 1), n_blocks)
    vw = 2 * HEAD_DIM

    @pl.when(t == 0)
    def _():
        fin_ref[...] = jnp.ones_like(fin_ref)

    @pl.when(qi == 0)
    def _():
        for h in range(n_heads):
            for n in range(n_blocks):
                vt_ref[h * n_blocks + n] = _transposed_values(
                    v_ref[n * ATT_BLOCK:(n + 1) * ATT_BLOCK, h * vw:(h + 1) * vw])

    lam = (jnp.exp(jnp.sum(lq1_ref[...] * lk1_ref[...], axis=1, keepdims=True))
           - jnp.exp(jnp.sum(lq2_ref[...] * lk2_ref[...], axis=1, keepdims=True))
           + lambda_init)
    for h in range(n_heads):
        o = (_normalised(fin_ref[2 * h]) - lam * _normalised(fin_ref[2 * h + 1])).T
        o_ref[:, h * vw:(h + 1) * vw] = (_rms(o, g_ref[...]) * (1.0 - lambda_init)).astype(o_ref.dtype)

    subs = [slice(j * HEAD_DIM, (j + 1) * HEAD_DIM) for j in range(2 * n_heads)]
    qs = [q_ref[:, cols] for cols in subs]
    diag = [_scores(_key_rows(k_ref, qi, 1, cols), q) for q, cols in zip(qs, subs)]
    states = tuple(_diag_state(s, vt_ref[(j // 2) * n_blocks + qi]) for j, s in enumerate(diag))

    def group_inputs(first, size):
        scores = [_scores(_key_rows(k_ref, first, size, cols), q) for q, cols in zip(qs, subs)]
        biases = [jnp.where(first + n < qi, 0.0, NEG_BIG) for n in range(size)]
        inputs = []
        for j, s in enumerate(scores):
            vts = [vt_ref[(j // 2) * n_blocks + first + n] for n in range(size)]
            pieces = [s[n * ATT_BLOCK:(n + 1) * ATT_BLOCK] for n in range(size)]
            inputs.append((pieces, biases, vts))
        return inputs

    _attend_groups(qi, DIFF_GROUPS, group_inputs, states, fin_ref)


def _diff_attention(qk, vproj, lq1, lk1, lq2, lk2, subln_g, n_heads, lambda_init,
                    heads_per_step=4):
    b, s, qk_width = qk.shape
    v_col0 = vproj.shape[2] // 2
    col0 = qk_width // 2
    hp = heads_per_step
    w = hp * 2 * HEAD_DIM
    n_groups = n_heads // hp
    tq = ATT_BLOCK
    n_steps = b * n_groups * (s // tq)
    grid = _LaggedGrid(n_steps, n_groups, s // tq)
    vec = lambda x: x.reshape(1, HEAD_DIM).astype(jnp.float32)
    vspec = pl.BlockSpec((1, HEAD_DIM), lambda t: (0, 0))
    qb, kb, vb = col0 // w, col0 // w + n_groups, v_col0 // w
    return pl.pallas_call(
        functools.partial(_diff_kernel, lambda_init=lambda_init, n_heads=hp, n_steps=n_steps),
        out_shape=jax.ShapeDtypeStruct((b, s, n_heads * 2 * HEAD_DIM), jnp.bfloat16),
        grid=(n_steps + 1,),
        in_specs=[vspec, vspec, vspec, vspec,
                  pl.BlockSpec((None, tq, w), grid.cur(lambda bi, h, qi: (bi, qi, qb + h))),
                  pl.BlockSpec((None, s, w), grid.cur(lambda bi, h, qi: (bi, 0, kb + h))),
                  pl.BlockSpec((None, s, w), grid.cur(lambda bi, h, qi: (bi, 0, vb + h)),
                               pipeline_mode=pl.Buffered(1)),
                  pl.BlockSpec((1, 2 * HEAD_DIM), lambda t: (0, 0))],
        out_specs=pl.BlockSpec((None, tq, w), grid.prev(lambda bi, h, qi: (bi, qi, h))),
        scratch_shapes=[pltpu.VMEM((hp * (s // ATT_BLOCK), 2 * HEAD_DIM + VT_PAD, ATT_BLOCK),
                                   jnp.bfloat16),
                        pltpu.VMEM((2 * hp, 2 * HEAD_DIM + VT_PAD, ATT_BLOCK), jnp.float32)],
        compiler_params=_params("arbitrary"),
        name="diff_attention",
    )(vec(lq1), vec(lk1), vec(lq2), vec(lk2), qk, qk, vproj, subln_g.reshape(1, 2 * HEAD_DIM))


def _cross_kernel(x_ref, gc_ref, wq_ref, kv_ref, wo_ref, gm_ref, x2_ref, hm_ref):
    c = QK_SCALE
    x = x_ref[...]
    hc = _rms(x, gc_ref[...]).astype(jnp.bfloat16)
    q = jnp.dot(hc, wq_ref[...], preferred_element_type=jnp.float32).astype(jnp.bfloat16)
    kv_w = MEM_HEADS * HEAD_DIM
    outs = []
    for h in range(MEM_HEADS):
        k = kv_ref[:, h * HEAD_DIM:(h + 1) * HEAD_DIM]
        v = kv_ref[:, kv_w + h * HEAD_DIM:kv_w + (h + 1) * HEAD_DIM]
        s = lax.dot_general(q[:, h * HEAD_DIM:(h + 1) * HEAD_DIM], k, NT_DIMS,
                            preferred_element_type=jnp.float32)
        m = jnp.max(s, axis=1, keepdims=True)
        p = jnp.exp2((s - m) * c)
        l = jnp.sum(p, axis=1, keepdims=True)
        o = jnp.dot(p.astype(jnp.bfloat16), v, preferred_element_type=jnp.float32) / l
        outs.append(o.astype(jnp.bfloat16))
    o = jnp.concatenate(outs, axis=1)
    x2 = x + jnp.dot(o, wo_ref[...], preferred_element_type=jnp.float32)
    x2_ref[...] = x2
    hm_ref[...] = _rms(x2, gm_ref[...]).astype(hm_ref.dtype)


def _cross_sublayer(x, kv, g_cross, w_cq, w_co, g_mlp, seq_len, mem_len, tq=512):
    n, d = x.shape
    qw = w_cq.shape[1]
    const = lambda i: (0, 0)
    return pl.pallas_call(
        _cross_kernel,
        out_shape=(jax.ShapeDtypeStruct((n, d), jnp.float32),
                   jax.ShapeDtypeStruct((n, d), jnp.bfloat16)),
        grid=(n // tq,),
        in_specs=[pl.BlockSpec((tq, d), lambda i: (i, 0)),
                  pl.BlockSpec((1, d), const),
                  pl.BlockSpec((d, qw), const, pipeline_mode=pl.Buffered(1)),
                  pl.BlockSpec((mem_len, 2 * qw), lambda i: (i // (seq_len // tq), 0)),
                  pl.BlockSpec((qw, d), const, pipeline_mode=pl.Buffered(1)),
                  pl.BlockSpec((1, d), const)],
        out_specs=(pl.BlockSpec((tq, d), lambda i: (i, 0)),
                   pl.BlockSpec((tq, d), lambda i: (i, 0))),
        compiler_params=_params("parallel"),
        name="cross_sublayer",
    )(x, g_cross.reshape(1, d), w_cq, kv, w_co, g_mlp.reshape(1, d))


def kernel(x, mem, ln_mix_g, w_in, moba_out_g, lambda_q1, lambda_k1, lambda_q2, lambda_k2,
           diff_subln_g, w_out, ln_cross_g, ln_mem_g, w_cq, w_ckv, w_co, ln_mlp_g,
           w_up, w_down, final_g):
    b, s, d = x.shape
    mem_len = mem.shape[1]
    depth = w_in.shape[0]
    mix_width = w_out.shape[1]
    moba_width = mix_width // 2
    moba_heads = moba_width // HEAD_DIM
    diff_heads = (mix_width - moba_width) // (2 * HEAD_DIM)
    bf = jnp.bfloat16

    xf = x.reshape(b * s, d)
    memf = mem.reshape(b * mem_len, d)
    for l in range(depth):
        lambda_init = 0.8 - 0.6 * math.exp(-0.3 * l)

        h = _rmsnorm(xf, ln_mix_g[l], bf)
        w_in_bf = w_in[l].astype(bf)
        qk = _inproj_qk(h, w_in_bf, s, moba_width).reshape(b, s, -1)
        bpg = moba_width // 1024
        vproj, w_out_bf, w_up_bf = _matmul(
            h, w_in_bf, bm=1024, bn=1024, out_dtype=bf, n_out=2 * moba_width,
            w_col_block=lambda j: (2 + 3 * (j // bpg)) * bpg + j % bpg,
            round_also=(w_out, w_up), layer=l, name="inproj_v")
        vproj = vproj.reshape(b, s, -1)
        mo = _moba_attention(qk, vproj, moba_out_g[l], moba_heads)
        do = _diff_attention(qk, vproj, lambda_q1[l], lambda_k1[l], lambda_q2[l], lambda_k2[l],
                             diff_subln_g[l], diff_heads, lambda_init)
        x1 = _matmul((mo.reshape(b * s, -1), do.reshape(b * s, -1)), w_out_bf, bm=1024, bn=1024,
                     out_dtype=jnp.float32, res=xf, name="outproj")

        kv = _norm_matmul(memf, ln_mem_g[l], w_ckv, l, bm=512, bn=512, name="mem_norm_kv")
        x2, hm = _cross_sublayer(x1, kv, ln_cross_g[l], w_cq[l].astype(bf), w_co[l].astype(bf),
                                 ln_mlp_g[l], s, mem_len)

        u, w_down_bf = _matmul(hm, w_up_bf, bm=1024, bn=1024, out_dtype=bf, act="relu2",
                               round_also=(w_down,), layer=l, name="mlp_up")
        xf = _matmul_ksplit(u, w_down_bf, x2, bm=1024, bn=1024, bk=4096, name="mlp_down")
    return _rmsnorm(xf, final_g, jnp.float32).reshape(b, s, d)
```
